```python
import jax, jax.numpy as jnp
from jax import lax
import numpy as np

D_MODEL = 2048
BATCH = 1
SEQ = 8192
DEPTH = 1

HEAD_DIM = 64
MIX_WIDTH = D_MODEL
RWKV_WIDTH = MIX_WIDTH // 2
RWKV_HEADS = RWKV_WIDTH // HEAD_DIM
ATTN_WIDTH = MIX_WIDTH - RWKV_WIDTH
ATTN_HEADS = ATTN_WIDTH // HEAD_DIM
KV_HEADS = max(1, ATTN_HEADS // 8)
KV_GROUP = ATTN_HEADS // KV_HEADS
KV_WIDTH = KV_HEADS * HEAD_DIM
W_LORA = 64
A_LORA = 64
G_LORA = 160
WINDOW = 128
BLOCK = WINDOW
ROPE_THETA = 500000.0
ROT_DIM = HEAD_DIM // 4
D_FF = 5632
CONV_WIDTH = 3
NORM_EPS = 1e-5
LNX_EPS = 64e-5
SPLIT_SIZES = (RWKV_WIDTH, RWKV_WIDTH, RWKV_WIDTH, W_LORA, A_LORA, G_LORA, ATTN_WIDTH, KV_WIDTH, KV_WIDTH)
RWKV_IN = 3 * RWKV_WIDTH + W_LORA + A_LORA + G_LORA
ATTN_IN = ATTN_WIDTH + 2 * KV_WIDTH
IN_WIDTH = RWKV_IN + ATTN_IN

kernel_name = "hymba_rwkv7_swa_sink_convffn"


def rms_norm(x, g, eps=NORM_EPS):
    xf = x.astype(jnp.float32)
    y = xf * lax.rsqrt(jnp.mean(xf * xf, axis=-1, keepdims=True) + eps)
    return (y * g.astype(jnp.float32)).astype(x.dtype)


def token_shift(z):
    return jnp.pad(z, ((0, 0), (1, 0), (0, 0)))[:, :-1]


def rope_tables(T):
    inv_freq = ROPE_THETA ** (-jnp.arange(0, ROT_DIM, 2, dtype=jnp.float32) / ROT_DIM)
    ang = jnp.arange(T, dtype=jnp.float32)[:, None] * inv_freq[None, :]
    return jnp.cos(ang), jnp.sin(ang)


def partial_rope(x, cos, sin):
    half = ROT_DIM // 2
    c = cos[None, :, None, :].astype(x.dtype)
    s = sin[None, :, None, :].astype(x.dtype)
    x1, x2, xp = x[..., :half], x[..., half:ROT_DIM], x[..., ROT_DIM:]
    return jnp.concatenate([x1 * c - x2 * s, x2 * c + x1 * s, xp], axis=-1)


def rwkv7_mix(r, k, v, xw, xa, xg, w0, w2, a0, a2, g2, k_k, k_a, r_k, lnx_g, lnx_b):
    B, T, _ = r.shape
    heads = lambda t: t.reshape(B, T, RWKV_HEADS, HEAD_DIM)
    f32 = jnp.float32
    w_log = -jax.nn.softplus(-(w0 + jnp.tanh(xw) @ w2).astype(f32)) - 0.5
    decay = jnp.exp(-jnp.exp(w_log))
    a = jax.nn.sigmoid(a0 + xa @ a2)
    g = jax.nn.sigmoid(xg) @ g2
    kk = heads(k * k_k).astype(f32)
    kk = kk / jnp.maximum(jnp.sqrt(jnp.sum(kk * kk, axis=-1, keepdims=True)), 1e-12)
    k = k * (1.0 + (a - 1.0) * k_a)
    rh, kh, vh, ah, dh = heads(r), heads(k), heads(v), heads(a), heads(decay)

    def step(S, inp):
        r_t, w_t, k_t, v_t, kk_t, b_t = inp
        sa = jnp.einsum('bhij,bhj->bhi', S, -kk_t)
        S = S * w_t[:, :, None, :] + sa[..., None] * b_t[:, :, None, :] + v_t[..., None] * k_t[:, :, None, :]
        return S, jnp.einsum('bhij,bhj->bhi', S, r_t)

    xs = tuple(jnp.moveaxis(t.astype(f32), 1, 0) for t in (rh, dh, kh, vh, kk, kk * ah.astype(f32)))
    S0 = jnp.zeros((B, RWKV_HEADS, HEAD_DIM, HEAD_DIM), f32)
    _, ys = lax.scan(step, S0, xs)
    y = jnp.moveaxis(ys, 0, 1)
    mu = jnp.mean(y, axis=-1, keepdims=True)
    var = jnp.mean(jnp.square(y - mu), axis=-1, keepdims=True)
    y = ((y - mu) * lax.rsqrt(var + LNX_EPS)).reshape(B, T, RWKV_WIDTH)
    y = y * lnx_g.astype(f32) + lnx_b.astype(f32)
    bonus = jnp.sum((rh * kh * r_k).astype(f32), axis=-1, keepdims=True) * vh.astype(f32)
    y = y + bonus.reshape(B, T, RWKV_WIDTH)
    return y.astype(r.dtype) * g


def swa_sink_attention(q, k, v, sinks, cos, sin):
    B, T, _ = q.shape
    NB = T // BLOCK
    q = partial_rope(q.reshape(B, T, ATTN_HEADS, HEAD_DIM), cos, sin)
    k = partial_rope(k.reshape(B, T, KV_HEADS, HEAD_DIM), cos, sin)
    v = v.reshape(B, T, KV_HEADS, HEAD_DIM)
    qb = q.reshape(B, NB, BLOCK, KV_HEADS, KV_GROUP, HEAD_DIM)

    def with_prev(t):
        t = t.reshape(B, NB, BLOCK, KV_HEADS, HEAD_DIM)
        prev = jnp.pad(t, ((0, 0), (1, 0), (0, 0), (0, 0), (0, 0)))[:, :-1]
        return jnp.concatenate([prev, t], axis=2)

    kw, vw = with_prev(k), with_prev(v)
    s = jnp.einsum('bnqkgd,bnskd->bnkgqs', qb, kw).astype(jnp.float32) * (HEAD_DIM ** -0.5)
    qi = jnp.arange(BLOCK)[:, None] + BLOCK
    si = jnp.arange(2 * BLOCK)[None, :]
    diff = qi - si
    key_pos = (jnp.arange(NB) * BLOCK)[:, None, None] - BLOCK + si[None]
    allowed = (diff >= 0)[None] & (diff < WINDOW)[None] & (key_pos >= 0)
    s = jnp.where(allowed[None, :, None, None], s, -jnp.inf)
    sink = jnp.broadcast_to(sinks.astype(jnp.float32).reshape(KV_HEADS, KV_GROUP)[None, None, :, :, None, None],
                            s.shape[:-1] + (1,))
    p = jax.nn.softmax(jnp.concatenate([s, sink], axis=-1), axis=-1)[..., :-1]
    o = jnp.einsum('bnkgqs,bnskd->bnqkgd', p.astype(v.dtype), vw)
    return o.reshape(B, T, ATTN_WIDTH)


def causal_dwconv(u, w, b):
    T = u.shape[1]
    up = jnp.pad(u, ((0, 0), (CONV_WIDTH - 1, 0), (0, 0)))
    out = b
    for j in range(CONV_WIDTH):
        out = out + w[j] * up[:, j:j + T]
    return out


def setup_inputs(seed: int = 0) -> dict:
    key = jax.random.key(seed)
    ks = jax.random.split(key, 26)
    n = lambda i, shape, scale: scale * jax.random.normal(ks[i], shape, jnp.float32)
    L = DEPTH
    return {
        "x": jax.random.normal(ks[0], (BATCH, SEQ, D_MODEL), jnp.float32),
        "ln_mix_g": 1.0 + n(1, (L, D_MODEL), 0.02),
        "w_in": n(2, (L, D_MODEL, IN_WIDTH), D_MODEL ** -0.5),
        "b_attn_qkv": n(3, (L, ATTN_IN), 0.02),
        "rwkv_shift_mu": jax.random.uniform(ks[4], (L, RWKV_IN), jnp.float32),
        "rwkv_w0": jax.random.uniform(ks[5], (L, RWKV_WIDTH), jnp.float32, -4.0, 0.0),
        "rwkv_w2": n(6, (L, W_LORA, RWKV_WIDTH), 0.5 * W_LORA ** -0.5),
        "rwkv_a0": n(7, (L, RWKV_WIDTH), 0.3),
        "rwkv_a2": n(8, (L, A_LORA, RWKV_WIDTH), 0.5 * A_LORA ** -0.5),
        "rwkv_g2": n(9, (L, G_LORA, RWKV_WIDTH), G_LORA ** -0.5),
        "rwkv_k_k": 0.85 + n(10, (L, RWKV_WIDTH), 0.05),
        "rwkv_k_a": 1.0 + n(11, (L, RWKV_WIDTH), 0.05),
        "rwkv_r_k": n(12, (L, RWKV_HEADS, HEAD_DIM), 0.1),
        "rwkv_lnx_g": 1.0 + n(13, (L, RWKV_WIDTH), 0.02),
        "rwkv_lnx_b": n(14, (L, RWKV_WIDTH), 0.02),
        "attn_sinks": n(15, (L, ATTN_HEADS), 0.5),
        "attn_out_g": 1.0 + n(16, (L, ATTN_WIDTH), 0.02),
        "w_out": n(17, (L, MIX_WIDTH, D_MODEL), MIX_WIDTH ** -0.5),
        "ln_ffn_g": 1.0 + n(18, (L, D_MODEL), 0.02),
        "ffn_w_up": n(19, (L, D_MODEL, D_FF), D_MODEL ** -0.5),
        "ffn_w_gate": n(20, (L, D_MODEL, D_FF), D_MODEL ** -0.5),
        "ffn_conv_w": n(21, (L, CONV_WIDTH, D_FF), CONV_WIDTH ** -0.5),
        "ffn_conv_b": n(22, (L, D_FF), 0.02),
        "ffn_w_down": n(23, (L, D_FF, D_MODEL), D_FF ** -0.5),
        "ln_final_g": 1.0 + n(24, (D_MODEL,), 0.02),
    }


def reference(x, ln_mix_g, w_in, b_attn_qkv, rwkv_shift_mu, rwkv_w0, rwkv_w2, rwkv_a0, rwkv_a2,
              rwkv_g2, rwkv_k_k, rwkv_k_a, rwkv_r_k, rwkv_lnx_g, rwkv_lnx_b, attn_sinks, attn_out_g,
              w_out, ln_ffn_g, ffn_w_up, ffn_w_gate, ffn_conv_w, ffn_conv_b, ffn_w_down, ln_final_g):
    T = x.shape[1]
    cos, sin = rope_tables(T)
    split_idx = [int(i) for i in np.cumsum(SPLIT_SIZES)[:-1]]
    for l in range(DEPTH):
        h = rms_norm(x, ln_mix_g[l])
        z = h @ w_in[l]
        z_rwkv, z_attn = z[..., :RWKV_IN], z[..., RWKV_IN:] + b_attn_qkv[l]
        mu = rwkv_shift_mu[l]
        z_rwkv = z_rwkv + mu * (token_shift(z_rwkv) - z_rwkv)
        z = jnp.concatenate([z_rwkv, z_attn], axis=-1)
        r, k, v, xw, xa, xg, q, ka, va = jnp.split(z, split_idx, axis=-1)
        y_rwkv = rwkv7_mix(r, k, v, xw, xa, xg, rwkv_w0[l], rwkv_w2[l], rwkv_a0[l], rwkv_a2[l],
                           rwkv_g2[l], rwkv_k_k[l], rwkv_k_a[l], rwkv_r_k[l], rwkv_lnx_g[l], rwkv_lnx_b[l])
        y_attn = rms_norm(swa_sink_attention(q, ka, va, attn_sinks[l], cos, sin), attn_out_g[l])
        x = x + jnp.concatenate([y_rwkv, y_attn], axis=-1) @ w_out[l]
        h = rms_norm(x, ln_ffn_g[l])
        u = causal_dwconv(h @ ffn_w_up[l], ffn_conv_w[l], ffn_conv_b[l])
        x = x + (jax.nn.silu(u) * (h @ ffn_w_gate[l])) @ ffn_w_down[l]
    return rms_norm(x, ln_final_g)
```

```python
import functools
import math

import jax
import jax.numpy as jnp
from jax import lax
from jax.experimental import pallas as pl
from jax.experimental.pallas import tpu as pltpu

F32 = jnp.float32
BF16 = jnp.bfloat16

D_MODEL = 2048
HEAD_DIM = 64
RWKV_WIDTH = 1024
ATTN_WIDTH = 1024
ATTN_HEADS = 16
KV_HEADS = 2
KV_WIDTH = KV_HEADS * HEAD_DIM
W_LORA = 64
A_LORA = 64
G_LORA = 160
WINDOW = 128
ROPE_THETA = 500000.0
ROT_DIM = 16
D_FF = 5632
NORM_EPS = 1e-5
LNX_EPS = 64e-5

LANES = 128
PREV_ROWS = 16
LORA_PAD = 512
GROUP = 4
GROUP_W = GROUP * HEAD_DIM
CHUNK = 64
VMEM_LIMIT = 56 * 1024 * 1024


def _dot(a, b):
    return jnp.dot(a, b, preferred_element_type=F32)


def _dot_nt(a, b):
    return lax.dot_general(a, b, (((1,), (1,)), ((), ())), preferred_element_type=F32)


def _dot_tn(a, b):
    return lax.dot_general(a, b, (((0,), (0,)), ((), ())), preferred_element_type=F32)


def _sigmoid(x):
    return 1.0 / (1.0 + jnp.exp(-x))


def _rms(xv, g):
    ms = jnp.mean(xv * xv, axis=-1, keepdims=True)
    return xv * lax.rsqrt(ms + NORM_EPS) * g


def _resident(shape):
    nd = len(shape)
    return pl.BlockSpec(shape, lambda *_: (0,) * nd, pipeline_mode=pl.Buffered(1))


def _in_proj_kernel(x_ref, xp_ref, g_ref, wrkv_ref, wlora_ref, wattn_ref, murkv_ref, mulora_ref,
                    battn_ref, rkv_ref, lora_ref, attn_ref, hext_ref, *, col_chunk):
    i = pl.program_id(0)
    g = g_ref[...]
    hext_ref[PREV_ROWS:, :] = _rms(x_ref[...], g).astype(BF16)
    hp = jnp.where(i > 0, _rms(xp_ref[...], g), 0.0)
    hext_ref[:PREV_ROWS, :] = hp.astype(BF16)
    hx = hext_ref[...]
    for w_ref, mu_ref, o_ref in ((wrkv_ref, murkv_ref, rkv_ref), (wlora_ref, mulora_ref, lora_ref)):
        width = o_ref.shape[1]
        for c0 in range(0, width, col_chunk):
            c1 = min(c0 + col_chunk, width)
            z = _dot(hx, w_ref[:, c0:c1])
            zp = pltpu.roll(z, 1, axis=0)
            zm = z + mu_ref[:, c0:c1] * (zp - z)
            o_ref[:, c0:c1] = zm[PREV_ROWS:, :]
    h = hext_ref[PREV_ROWS:, :]
    width = attn_ref.shape[1]
    for c0 in range(0, width, col_chunk):
        c1 = min(c0 + col_chunk, width)
        attn_ref[:, c0:c1] = _dot(h, wattn_ref[:, c0:c1]) + battn_ref[:, c0:c1]


def _in_proj(x2d, g, w_rkv, w_lora, w_attn, mu_rkv, mu_lora, b_attn, *, tm=256, col_chunk=512):
    T = x2d.shape[0]
    n_attn = w_attn.shape[1]
    grid = (T // tm,)
    prev_blocks = tm // PREV_ROWS
    return pl.pallas_call(
        functools.partial(_in_proj_kernel, col_chunk=col_chunk),
        grid=grid,
        in_specs=[
            pl.BlockSpec((tm, D_MODEL), lambda i: (i, 0)),
            pl.BlockSpec((PREV_ROWS, D_MODEL), lambda i: (jnp.maximum(i * prev_blocks - 1, 0), 0)),
            _resident((1, D_MODEL)),
            _resident(w_rkv.shape), _resident(w_lora.shape), _resident(w_attn.shape),
            _resident(mu_rkv.shape), _resident(mu_lora.shape), _resident(b_attn.shape),
        ],
        out_specs=[
            pl.BlockSpec((tm, 3 * RWKV_WIDTH), lambda i: (i, 0)),
            pl.BlockSpec((tm, LORA_PAD), lambda i: (i, 0)),
            pl.BlockSpec((tm, n_attn), lambda i: (i, 0)),
        ],
        out_shape=[
            jax.ShapeDtypeStruct((T, 3 * RWKV_WIDTH), F32),
            jax.ShapeDtypeStruct((T, LORA_PAD), F32),
            jax.ShapeDtypeStruct((T, n_attn), F32),
        ],
        scratch_shapes=[pltpu.VMEM((tm + PREV_ROWS, D_MODEL), BF16)],
        compiler_params=pltpu.CompilerParams(dimension_semantics=("arbitrary",),
                                             vmem_limit_bytes=VMEM_LIMIT),
        name="in_proj",
    )(x2d, x2d, g, w_rkv, w_lora, w_attn, mu_rkv, mu_lora, b_attn)


def _split_bf16(x):
    hi = x.astype(BF16)
    lo = (x - hi.astype(F32)).astype(BF16)
    return hi, lo


def _block_diag(x):
    blk = lax.broadcasted_iota(jnp.int32, x.shape, 1) // HEAD_DIM
    zero = jnp.zeros_like(x)
    return jnp.concatenate([jnp.where(blk == h, x, zero) for h in range(GROUP)], axis=0)


def _head_sums(x, ones_bd):
    rows = x.shape[0]
    outs = []
    for q in range(RWKV_WIDTH // GROUP_W):
        hi, lo = _split_bf16(x[:, q * GROUP_W:(q + 1) * GROUP_W])
        s = _dot(jnp.concatenate([hi, lo], axis=0), ones_bd)
        outs.append(s[:rows] + s[rows:])
    return jnp.concatenate(outs, axis=1)


def _rwkv_kernel(r_ref, k_ref, v_ref, lora_ref, w0_ref, w2_ref, a0_ref, a2_ref, g2_ref, kk_ref,
                 ka_ref, rk_ref, lng_ref, lnb_ref, y_ref, s_ref, *, n_chunks):
    C = CHUNK

    @pl.when(pl.program_id(0) == 0)
    def _():
        s_ref[...] = jnp.zeros_like(s_ref)

    row = lax.broadcasted_iota(jnp.int32, (C, GROUP_W), 0)
    col = lax.broadcasted_iota(jnp.int32, (C, GROUP_W), 1) % C
    strict = col < row
    incl = col <= row
    eye = (col == row).astype(F32)
    r_i = lax.broadcasted_iota(jnp.int32, (GROUP_W, GROUP_W), 0) // HEAD_DIM
    c_i = lax.broadcasted_iota(jnp.int32, (GROUP_W, GROUP_W), 1) // HEAD_DIM
    same_head = r_i == c_i
    ones_bd = same_head.astype(BF16)
    tri = (lax.broadcasted_iota(jnp.int32, (C, C), 1) <= lax.broadcasted_iota(jnp.int32, (C, C), 0)).astype(BF16)
    decay_scale = -math.exp(-0.5)

    def chunk_body(c, carry):
        rows = pl.ds(pl.multiple_of(c * C, C), C)
        r = r_ref[rows, :]
        k = k_ref[rows, :]
        v = v_ref[rows, :]
        lo = lora_ref[rows, :]
        xw = lo[:, 0:LANES]
        xa = lo[:, LANES:2 * LANES]
        xg = lo[:, 2 * LANES:LORA_PAD]
        u = w0_ref[...] + _dot(jnp.tanh(xw).astype(BF16), w2_ref[...])
        lw = decay_scale * _sigmoid(u)
        a = _sigmoid(a0_ref[...] + _dot(xa.astype(BF16), a2_ref[...]))
        g = _dot(_sigmoid(xg).astype(BF16), g2_ref[...])
        kkraw = k * kk_ref[...]
        k2 = k * (1.0 + (a - 1.0) * ka_ref[...])
        sums = _head_sums(jnp.concatenate([kkraw * kkraw, r * k2 * rk_ref[...]], axis=0), ones_bd)
        kk = kkraw / jnp.maximum(jnp.sqrt(sums[:C]), 1e-12)
        bonus = sums[C:] * v
        b = kk * a
        lw_hi, lw_lo = _split_bf16(lw)
        cl = _dot(tri, lw_hi) + _dot(tri, lw_lo)
        cl_last = cl[C - 1:C, :]
        a_t = -kk * jnp.exp(cl - lw)
        e_neg = jnp.exp(-cl)
        k_h = k2 * e_neg
        b_h = b * e_neg
        r_t = r * jnp.exp(cl)
        e_end = jnp.exp(cl_last - cl)
        k_e = k2 * e_end
        b_e = b * e_end
        p_c = jnp.exp(cl_last)

        ys = []
        for q in range(RWKV_WIDTH // GROUP_W):
            sl = slice(q * GROUP_W, (q + 1) * GROUP_W)
            s0 = s_ref[q]
            ar = jnp.concatenate([a_t[:, sl], r_t[:, sl]], axis=0).astype(BF16)
            ak = _dot_nt(ar, _block_diag(k_h[:, sl].astype(BF16)))
            ab = _dot_nt(ar, _block_diag(b_h[:, sl].astype(BF16)))
            a_ak = jnp.where(strict, ak[:C], 0.0)
            a_rk = jnp.where(incl, ak[C:], 0.0)
            l_ab = jnp.where(strict, ab[:C], 0.0)
            a_rb = jnp.where(incl, ab[C:], 0.0)
            t_inv = eye + l_ab
            p = l_ab
            n_lvl = int(math.log2(C))
            for lvl in range(1, n_lvl):
                pb = _block_diag(p.astype(BF16))
                if lvl == 1:
                    p = _dot(p.astype(BF16), pb)
                else:
                    tp = _dot(jnp.concatenate([t_inv, p], axis=0).astype(BF16), pb)
                    t_inv = t_inv + tp[:C]
                    p = tp[C:]
            t_inv = t_inv + _dot(t_inv.astype(BF16), _block_diag(p.astype(BF16)))
            vq = v[:, sl]
            s0b = s0.astype(BF16)
            xs = _dot_nt(ar, s0b)
            av = _dot(jnp.concatenate([a_ak, a_rk], axis=0).astype(BF16), _block_diag(vq.astype(BF16)))
            x_u = xs[:C] + av[:C]
            u_q = _dot(t_inv.astype(BF16), _block_diag(x_u.astype(BF16)))
            y_q = xs[C:] + av[C:] + _dot(a_rb.astype(BF16), _block_diag(u_q.astype(BF16)))
            ys.append(y_q)
            vu = jnp.concatenate([vq, u_q], axis=0).astype(BF16)
            kb = jnp.concatenate([k_e[:, sl], b_e[:, sl]], axis=0).astype(BF16)
            upd = _dot_tn(vu, kb)
            s_ref[q] = s0 * p_c[:, sl] + jnp.where(same_head, upd, 0.0)

        y = jnp.concatenate(ys, axis=1)
        mu = _head_sums(y, ones_bd) * (1.0 / HEAD_DIM)
        d = y - mu
        var = _head_sums(d * d, ones_bd) * (1.0 / HEAD_DIM)
        yn = d * lax.rsqrt(var + LNX_EPS) * lng_ref[...] + lnb_ref[...]
        y_ref[rows, :] = ((yn + bonus) * g).astype(y_ref.dtype)
        return carry

    lax.fori_loop(0, n_chunks, chunk_body, 0)


def _rwkv(rkv, lora, w0, w2, a0, a2, g2, k_k, k_a, r_k, ln_g, ln_b, *, tb=512):
    T = rkv.shape[0]
    grid = (T // tb,)
    vec = _resident((1, RWKV_WIDTH))
    return pl.pallas_call(
        functools.partial(_rwkv_kernel, n_chunks=tb // CHUNK),
        grid=grid,
        in_specs=[
            pl.BlockSpec((tb, RWKV_WIDTH), lambda i: (i, 0)),
            pl.BlockSpec((tb, RWKV_WIDTH), lambda i: (i, 1)),
            pl.BlockSpec((tb, RWKV_WIDTH), lambda i: (i, 2)),
            pl.BlockSpec((tb, LORA_PAD), lambda i: (i, 0)),
            vec, _resident(w2.shape), vec, _resident(a2.shape), _resident(g2.shape),
            vec, vec, vec, vec, vec,
        ],
        out_specs=pl.BlockSpec((tb, RWKV_WIDTH), lambda i: (i, 0)),
        out_shape=jax.ShapeDtypeStruct((T, RWKV_WIDTH), BF16),
        scratch_shapes=[pltpu.VMEM((RWKV_WIDTH // GROUP_W, GROUP_W, GROUP_W), F32)],
        compiler_params=pltpu.CompilerParams(dimension_semantics=("arbitrary",),
                                             vmem_limit_bytes=VMEM_LIMIT),
        name="rwkv7",
    )(rkv, rkv, rkv, lora, w0, w2, a0, a2, g2, k_k, k_a, r_k, ln_g, ln_b)


def _rope(x, cos_l, sin_a, sin_b):
    return x * cos_l + pltpu.roll(x, LANES - ROT_DIM // 2, axis=1) * sin_a + pltpu.roll(x, ROT_DIM // 2, axis=1) * sin_b


def _attn_kernel(sink_ref, q_ref, kv_ref, kvp_ref, cos_ref, sa_ref, sb_ref, cosp_ref, sap_ref, sbp_ref,
                 g_ref, o_ref):
    n = pl.program_id(0)
    B = WINDOW
    cos_l, sin_a, sin_b = cos_ref[...], sa_ref[...], sb_ref[...]
    kv = kv_ref[...]
    kvp = kvp_ref[...]
    k_cur = _rope(kv[:, :LANES], cos_l, sin_a, sin_b)
    k_prev = _rope(kvp[:, :LANES], cosp_ref[...], sap_ref[...], sbp_ref[...])
    k_win = jnp.concatenate([k_prev, k_cur], axis=0)
    v_win = jnp.concatenate([kvp[:, LANES:], kv[:, LANES:]], axis=0)
    lane = lax.broadcasted_iota(jnp.int32, (2 * B, LANES), 1)
    low = lane < HEAD_DIM
    k_sw = pltpu.roll(k_win, HEAD_DIM, axis=1)
    v_sw = pltpu.roll(v_win, HEAD_DIM, axis=1)
    k_dup = [jnp.where(low, k_win, k_sw).astype(BF16), jnp.where(low, k_sw, k_win).astype(BF16)]
    v_dup = [jnp.where(low, v_win, v_sw).astype(BF16), jnp.where(low, v_sw, v_win).astype(BF16)]

    qi = lax.broadcasted_iota(jnp.int32, (2 * B, 2 * B), 0) % B
    si = lax.broadcasted_iota(jnp.int32, (2 * B, 2 * B), 1)
    allowed = (si > qi) & (si <= qi + B) & ((si >= B) | (n > 0))
    top = lax.broadcasted_iota(jnp.int32, (2 * B, 1), 0) < B
    qlow = lax.broadcasted_iota(jnp.int32, (B, LANES), 1) < HEAD_DIM
    pairs_per_kv = (ATTN_HEADS // KV_HEADS) // 2
    outs = []
    for p in range(ATTN_HEADS // 2):
        kvh = p // pairs_per_kv
        q = _rope(q_ref[:, p * LANES:(p + 1) * LANES], cos_l, sin_a, sin_b) * (HEAD_DIM ** -0.5)
        zero = jnp.zeros_like(q)
        q_st = jnp.concatenate([jnp.where(qlow, q, zero), jnp.where(qlow, zero, q)], axis=0).astype(BF16)
        s = _dot_nt(q_st, k_dup[kvh])
        s = jnp.where(allowed, s, -1e30)
        sink = jnp.where(top, sink_ref[2 * p], sink_ref[2 * p + 1])
        m = jnp.maximum(jnp.max(s, axis=-1, keepdims=True), sink)
        e = jnp.exp(s - m)
        denom = jnp.sum(e, axis=-1, keepdims=True) + jnp.exp(sink - m)
        o_st = _dot(e.astype(BF16), v_dup[kvh]) / denom
        outs.append(jnp.where(qlow, o_st[:B], o_st[B:]))
    o = jnp.concatenate(outs, axis=1)
    o_ref[...] = _rms(o, g_ref[...]).astype(o_ref.dtype)


def _attn(attn_in, sinks, cos_l, sin_a, sin_b, g):
    T = attn_in.shape[0]
    B = WINDOW
    grid = (T // B,)
    kv_blk = ATTN_WIDTH // (2 * KV_WIDTH)
    cur = lambda n: (n, 0)
    prev = lambda n: (jnp.maximum(n - 1, 0), 0)
    tab = lambda im: pl.BlockSpec((B, LANES), im)
    return pl.pallas_call(
        _attn_kernel,
        grid=grid,
        in_specs=[
            pl.BlockSpec(memory_space=pltpu.SMEM),
            pl.BlockSpec((B, ATTN_WIDTH), cur),
            pl.BlockSpec((B, 2 * KV_WIDTH), lambda n: (n, kv_blk)),
            pl.BlockSpec((B, 2 * KV_WIDTH), lambda n: (jnp.maximum(n - 1, 0), kv_blk)),
            tab(cur), tab(cur), tab(cur), tab(prev), tab(prev), tab(prev),
            _resident((1, ATTN_WIDTH)),
        ],
        out_specs=pl.BlockSpec((B, ATTN_WIDTH), cur),
        out_shape=jax.ShapeDtypeStruct((T, ATTN_WIDTH), BF16),
        compiler_params=pltpu.CompilerParams(dimension_semantics=("arbitrary",),
                                             vmem_limit_bytes=VMEM_LIMIT),
        name="swa_attn",
    )(sinks, attn_in, attn_in, attn_in, cos_l, sin_a, sin_b, cos_l, sin_a, sin_b, g)


def _out_proj_kernel(x_ref, yr_ref, ya_ref, wr_ref, wa_ref, o_ref):
    o_ref[...] = x_ref[...] + _dot(yr_ref[...], wr_ref[...]) + _dot(ya_ref[...], wa_ref[...])


def _out_proj(x2d, y_rwkv, y_attn, w_r, w_a, *, tm=512):
    T = x2d.shape[0]
    row = lambda w: pl.BlockSpec((tm, w), lambda i: (i, 0))
    return pl.pallas_call(
        _out_proj_kernel,
        grid=(T // tm,),
        in_specs=[row(D_MODEL), row(RWKV_WIDTH), row(ATTN_WIDTH), _resident(w_r.shape), _resident(w_a.shape)],
        out_specs=row(D_MODEL),
        out_shape=jax.ShapeDtypeStruct((T, D_MODEL), F32),
        compiler_params=pltpu.CompilerParams(dimension_semantics=("arbitrary",),
                                             vmem_limit_bytes=VMEM_LIMIT),
        name="out_proj",
    )(x2d, y_rwkv, y_attn, w_r, w_a)


def _ffn_kernel(x_ref, xp_ref, g_ref, wup_ref, wgate_ref, cw_ref, cb_ref, wdown_ref, gf_ref, o_ref,
                hext_ref, acc_ref):
    i = pl.program_id(0)
    f = pl.program_id(1)

    @pl.when(f == 0)
    def _():
        g = g_ref[...]
        hext_ref[PREV_ROWS:, :] = _rms(x_ref[...], g).astype(BF16)
        hp = jnp.where(i > 0, _rms(xp_ref[...], g), 0.0)
        hext_ref[:PREV_ROWS, :] = hp.astype(BF16)
        acc_ref[...] = jnp.zeros_like(acc_ref)

    hx = hext_ref[...]
    u = _dot(hx, wup_ref[...])
    cw = cw_ref[...]
    conv = cb_ref[...] + cw[2:3, :] * u + cw[1:2, :] * pltpu.roll(u, 1, axis=0) + cw[0:1, :] * pltpu.roll(u, 2, axis=0)
    conv = conv[PREV_ROWS:, :]
    gate = _dot(hext_ref[PREV_ROWS:, :], wgate_ref[...])
    act = conv * _sigmoid(conv) * gate
    acc_ref[...] += _dot(act.astype(BF16), wdown_ref[...])

    @pl.when(f == pl.num_programs(1) - 1)
    def _():
        o_ref[...] = _rms(x_ref[...] + acc_ref[...], gf_ref[...])


def _ffn(x1, g, w_up, w_gate, conv_w, conv_b, w_down, g_final, *, tm=512, tf=512):
    T = x1.shape[0]
    prev_blocks = tm // PREV_ROWS
    return pl.pallas_call(
        _ffn_kernel,
        grid=(T // tm, D_FF // tf),
        in_specs=[
            pl.BlockSpec((tm, D_MODEL), lambda i, f: (i, 0)),
            pl.BlockSpec((PREV_ROWS, D_MODEL), lambda i, f: (jnp.maximum(i * prev_blocks - 1, 0), 0)),
            pl.BlockSpec((1, D_MODEL), lambda i, f: (0, 0)),
            pl.BlockSpec((D_MODEL, tf), lambda i, f: (0, f)),
            pl.BlockSpec((D_MODEL, tf), lambda i, f: (0, f)),
            pl.BlockSpec((3, tf), lambda i, f: (0, f)),
            pl.BlockSpec((1, tf), lambda i, f: (0, f)),
            pl.BlockSpec((tf, D_MODEL), lambda i, f: (f, 0)),
            pl.BlockSpec((1, D_MODEL), lambda i, f: (0, 0)),
        ],
        out_specs=pl.BlockSpec((tm, D_MODEL), lambda i, f: (i, 0)),
        out_shape=jax.ShapeDtypeStruct((T, D_MODEL), F32),
        scratch_shapes=[pltpu.VMEM((tm + PREV_ROWS, D_MODEL), BF16), pltpu.VMEM((tm, D_MODEL), F32)],
        compiler_params=pltpu.CompilerParams(dimension_semantics=("arbitrary", "arbitrary"),
                                             vmem_limit_bytes=VMEM_LIMIT),
        name="ffn",
    )(x1, x1, g, w_up, w_gate, conv_w, conv_b, w_down, g_final)


def _pad_cols(a, width):
    return jnp.pad(a, ((0, 0), (0, width - a.shape[1])))


def _pad_rows(a, rows):
    return jnp.pad(a, ((0, rows - a.shape[0]), (0, 0)))


def _rope_lane_tables(T):
    half = ROT_DIM // 2
    inv_freq = ROPE_THETA ** (-jnp.arange(0, ROT_DIM, 2, dtype=F32) / ROT_DIM)
    ang = jnp.arange(T, dtype=F32)[:, None] * inv_freq[None, :]
    cos, sin = jnp.cos(ang), jnp.sin(ang)
    ones = jnp.ones((T, HEAD_DIM - ROT_DIM), F32)
    zeros_h = jnp.zeros((T, half), F32)
    zeros_r = jnp.zeros((T, HEAD_DIM - ROT_DIM), F32)
    cos_h = jnp.concatenate([cos, cos, ones], axis=1)
    sa_h = jnp.concatenate([-sin, zeros_h, zeros_r], axis=1)
    sb_h = jnp.concatenate([zeros_h, sin, zeros_r], axis=1)
    rep = LANES // HEAD_DIM
    return jnp.tile(cos_h, (1, rep)), jnp.tile(sa_h, (1, rep)), jnp.tile(sb_h, (1, rep))


def kernel(x, ln_mix_g, w_in, b_attn_qkv, rwkv_shift_mu, rwkv_w0, rwkv_w2, rwkv_a0, rwkv_a2, rwkv_g2, rwkv_k_k, rwkv_k_a, rwkv_r_k, rwkv_lnx_g, rwkv_lnx_b, attn_sinks, attn_out_g, w_out, ln_ffn_g, ffn_w_up, ffn_w_gate, ffn_conv_w, ffn_conv_b, ffn_w_down, ln_final_g):
    B, T, _ = x.shape
    assert B == 1 and ln_mix_g.shape[0] == 1
    l = 0
    x2d = x.reshape(T, D_MODEL)
    row = lambda a: a.reshape(1, -1)

    w = w_in[l]
    mu = rwkv_shift_mu[l]
    o_w, o_a, o_g = 3 * RWKV_WIDTH, 3 * RWKV_WIDTH + W_LORA, 3 * RWKV_WIDTH + W_LORA + A_LORA
    o_q = o_g + G_LORA

    def lora_cols(a):
        return jnp.concatenate([_pad_cols(a[:, o_w:o_a], LANES), _pad_cols(a[:, o_a:o_g], LANES),
                                _pad_cols(a[:, o_g:o_q], 2 * LANES)], axis=1)

    w_rkv = w[:, :o_w].astype(BF16)
    w_lora = lora_cols(w).astype(BF16)
    w_attn = w[:, o_q:].astype(BF16)
    mu2 = row(mu)
    rkv, lora, attn_in = _in_proj(x2d, row(ln_mix_g[l]), w_rkv, w_lora, w_attn, mu2[:, :o_w],
                                  lora_cols(mu2), row(b_attn_qkv[l]))

    y_rwkv = _rwkv(rkv, lora, row(rwkv_w0[l]), _pad_rows(rwkv_w2[l], LANES).astype(BF16),
                   row(rwkv_a0[l]), _pad_rows(rwkv_a2[l], LANES).astype(BF16),
                   _pad_rows(rwkv_g2[l], 2 * LANES).astype(BF16), row(rwkv_k_k[l]), row(rwkv_k_a[l]),
                   row(rwkv_r_k[l]), row(rwkv_lnx_g[l]), row(rwkv_lnx_b[l]))

    cos_l, sin_a, sin_b = _rope_lane_tables(T)
    y_attn = _attn(attn_in, attn_sinks[l], cos_l, sin_a, sin_b, row(attn_out_g[l]))

    wo = w_out[l].astype(BF16)
    x1 = _out_proj(x2d, y_rwkv, y_attn, wo[:RWKV_WIDTH], wo[RWKV_WIDTH:])

    out = _ffn(x1, row(ln_ffn_g[l]), ffn_w_up[l].astype(BF16), ffn_w_gate[l].astype(BF16),
               ffn_conv_w[l], row(ffn_conv_b[l]), ffn_w_down[l].astype(BF16), row(ln_final_g))
    return out.reshape(B, T, D_MODEL)
```

```python
import functools
import math

import jax
import jax.numpy as jnp
from jax import lax
from jax.experimental import pallas as pl
from jax.experimental.pallas import tpu as pltpu

F32 = jnp.float32
BF16 = jnp.bfloat16

D_MODEL = 2048
HEAD_DIM = 64
RWKV_WIDTH = 1024
ATTN_WIDTH = 1024
ATTN_HEADS = 16
KV_HEADS = 2
KV_WIDTH = KV_HEADS * HEAD_DIM
W_LORA = 64
A_LORA = 64
G_LORA = 160
WINDOW = 128
ROPE_THETA = 500000.0
ROT_DIM = 16
D_FF = 5632
NORM_EPS = 1e-5
LNX_EPS = 64e-5

LANES = 128
PREV_ROWS = 16
LORA_PAD = 512
GROUP = 4
GROUP_W = GROUP * HEAD_DIM
CHUNK = 64
VMEM_LIMIT = 56 * 1024 * 1024


def _dot(a, b):
    return jnp.dot(a, b, preferred_element_type=F32)


def _dot_nt(a, b):
    return lax.dot_general(a, b, (((1,), (1,)), ((), ())), preferred_element_type=F32)


def _dot_tn(a, b):
    return lax.dot_general(a, b, (((0,), (0,)), ((), ())), preferred_element_type=F32)


def _sigmoid(x):
    return 1.0 / (1.0 + jnp.exp(-x))


def _rms(xv, g):
    ms = jnp.mean(xv * xv, axis=-1, keepdims=True)
    return xv * lax.rsqrt(ms + NORM_EPS) * g


def _resident(shape):
    nd = len(shape)
    return pl.BlockSpec(shape, lambda *_: (0,) * nd, pipeline_mode=pl.Buffered(1))


def _in_proj_kernel(x_ref, xp_ref, g_ref, wrkv_ref, wlora_ref, wattn_ref, murkv_ref, mulora_ref,
                    battn_ref, rkv_ref, lora_ref, attn_ref, hext_ref, *, col_chunk):
    i = pl.program_id(0)
    g = g_ref[...]
    hext_ref[PREV_ROWS:, :] = _rms(x_ref[...], g).astype(BF16)
    hp = jnp.where(i > 0, _rms(xp_ref[...], g), 0.0)
    hext_ref[:PREV_ROWS, :] = hp.astype(BF16)
    hx = hext_ref[...]
    for w_ref, mu_ref, o_ref in ((wrkv_ref, murkv_ref, rkv_ref), (wlora_ref, mulora_ref, lora_ref)):
        width = o_ref.shape[1]
        for c0 in range(0, width, col_chunk):
            c1 = min(c0 + col_chunk, width)
            z = _dot(hx, w_ref[:, c0:c1])
            zp = pltpu.roll(z, 1, axis=0)
            zm = z + mu_ref[:, c0:c1] * (zp - z)
            o_ref[:, c0:c1] = zm[PREV_ROWS:, :]
    h = hext_ref[PREV_ROWS:, :]
    width = attn_ref.shape[1]
    for c0 in range(0, width, col_chunk):
        c1 = min(c0 + col_chunk, width)
        attn_ref[:, c0:c1] = _dot(h, wattn_ref[:, c0:c1]) + battn_ref[:, c0:c1]


def _in_proj(x2d, g, w_rkv, w_lora, w_attn, mu_rkv, mu_lora, b_attn, *, tm=256, col_chunk=512):
    T = x2d.shape[0]
    n_attn = w_attn.shape[1]
    grid = (T // tm,)
    prev_blocks = tm // PREV_ROWS
    return pl.pallas_call(
        functools.partial(_in_proj_kernel, col_chunk=col_chunk),
        grid=grid,
        in_specs=[
            pl.BlockSpec((tm, D_MODEL), lambda i: (i, 0)),
            pl.BlockSpec((PREV_ROWS, D_MODEL), lambda i: (jnp.maximum(i * prev_blocks - 1, 0), 0)),
            _resident((1, D_MODEL)),
            _resident(w_rkv.shape), _resident(w_lora.shape), _resident(w_attn.shape),
            _resident(mu_rkv.shape), _resident(mu_lora.shape), _resident(b_attn.shape),
        ],
        out_specs=[
            pl.BlockSpec((tm, 3 * RWKV_WIDTH), lambda i: (i, 0)),
            pl.BlockSpec((tm, LORA_PAD), lambda i: (i, 0)),
            pl.BlockSpec((tm, n_attn), lambda i: (i, 0)),
        ],
        out_shape=[
            jax.ShapeDtypeStruct((T, 3 * RWKV_WIDTH), F32),
            jax.ShapeDtypeStruct((T, LORA_PAD), F32),
            jax.ShapeDtypeStruct((T, n_attn), F32),
        ],
        scratch_shapes=[pltpu.VMEM((tm + PREV_ROWS, D_MODEL), BF16)],
        compiler_params=pltpu.CompilerParams(dimension_semantics=("arbitrary",),
                                             vmem_limit_bytes=VMEM_LIMIT),
        name="in_proj",
    )(x2d, x2d, g, w_rkv, w_lora, w_attn, mu_rkv, mu_lora, b_attn)


def _split_bf16(x):
    hi = x.astype(BF16)
    lo = (x - hi.astype(F32)).astype(BF16)
    return hi, lo


def _block_diag(x):
    blk = lax.broadcasted_iota(jnp.int32, x.shape, 1) // HEAD_DIM
    zero = jnp.zeros_like(x)
    return jnp.concatenate([jnp.where(blk == h, x, zero) for h in range(GROUP)], axis=0)


def _head_sums(x, ones_bd):
    rows = x.shape[0]
    outs = []
    for q in range(RWKV_WIDTH // GROUP_W):
        hi, lo = _split_bf16(x[:, q * GROUP_W:(q + 1) * GROUP_W])
        s = _dot(jnp.concatenate([hi, lo], axis=0), ones_bd)
        outs.append(s[:rows] + s[rows:])
    return jnp.concatenate(outs, axis=1)


def _rwkv_kernel(r_ref, k_ref, v_ref, lora_ref, w0_ref, w2_ref, a0_ref, a2_ref, g2_ref, kk_ref,
                 ka_ref, rk_ref, lng_ref, lnb_ref, y_ref, s_ref, *, n_chunks):
    C = CHUNK
    TB = n_chunks * C
    n_groups = RWKV_WIDTH // GROUP_W

    @pl.when(pl.program_id(0) == 0)
    def _():
        s_ref[...] = jnp.zeros_like(s_ref)

    row = lax.broadcasted_iota(jnp.int32, (C, GROUP_W), 0)
    col = lax.broadcasted_iota(jnp.int32, (C, GROUP_W), 1) % C
    strict = col < row
    incl = col <= row
    eye = (col == row).astype(F32)
    r_i = lax.broadcasted_iota(jnp.int32, (GROUP_W, GROUP_W), 0) // HEAD_DIM
    c_i = lax.broadcasted_iota(jnp.int32, (GROUP_W, GROUP_W), 1) // HEAD_DIM
    same_head = r_i == c_i
    ones_bd = same_head.astype(BF16)
    t_i = lax.broadcasted_iota(jnp.int32, (TB, TB), 0)
    t_j = lax.broadcasted_iota(jnp.int32, (TB, TB), 1)
    tri = ((t_j <= t_i) & (t_j // C == t_i // C)).astype(BF16)
    decay_scale = -math.exp(-0.5)

    r = r_ref[...]
    k = k_ref[...]
    v = v_ref[...]
    lo = lora_ref[...]
    xw = lo[:, 0:LANES]
    xa = lo[:, LANES:2 * LANES]
    xg = lo[:, 2 * LANES:LORA_PAD]
    u = w0_ref[...] + _dot(jnp.tanh(xw).astype(BF16), w2_ref[...])
    lw = decay_scale * _sigmoid(u)
    a = _sigmoid(a0_ref[...] + _dot(xa.astype(BF16), a2_ref[...]))
    g = _dot(_sigmoid(xg).astype(BF16), g2_ref[...])
    kkraw = k * kk_ref[...]
    k2 = k * (1.0 + (a - 1.0) * ka_ref[...])
    sums = _head_sums(jnp.concatenate([kkraw * kkraw, r * k2 * rk_ref[...]], axis=0), ones_bd)
    kk = kkraw / jnp.maximum(jnp.sqrt(sums[:TB]), 1e-12)
    bonus = sums[TB:] * v
    b = kk * a
    lw_hi, lw_lo = _split_bf16(lw)
    cl = _dot(tri, lw_hi) + _dot(tri, lw_lo)
    cl_last = jnp.concatenate(
        [jnp.broadcast_to(cl[(c + 1) * C - 1:(c + 1) * C, :], (C, RWKV_WIDTH)) for c in range(n_chunks)], axis=0)
    a_t = (-kk * jnp.exp(cl - lw)).astype(BF16)
    e_neg = jnp.exp(-cl)
    k_h = (k2 * e_neg).astype(BF16)
    b_h = (b * e_neg).astype(BF16)
    r_t = r * jnp.exp(cl)
    r_tb = r_t.astype(BF16)
    e_end = jnp.exp(cl_last - cl)
    k_e = (k2 * e_end).astype(BF16)
    b_e = (b * e_end).astype(BF16)
    p_c = jnp.exp(cl_last)
    vb = v.astype(BF16)

    streams = [(c, q) for c in range(n_chunks) for q in range(n_groups)]

    def blk(x, s):
        c, q = s
        return x[c * C:(c + 1) * C, q * GROUP_W:(q + 1) * GROUP_W]

    a_ak, a_rk, l_ab, a_rb = {}, {}, {}, {}
    for s in streams:
        ar = jnp.concatenate([blk(a_t, s), blk(r_tb, s)], axis=0)
        ak = _dot_nt(ar, _block_diag(blk(k_h, s)))
        ab = _dot_nt(ar, _block_diag(blk(b_h, s)))
        a_ak[s] = jnp.where(strict, ak[:C], 0.0)
        a_rk[s] = jnp.where(incl, ak[C:], 0.0)
        l_ab[s] = jnp.where(strict, ab[:C], 0.0)
        a_rb[s] = jnp.where(incl, ab[C:], 0.0).astype(BF16)
    t_inv = {s: eye + l_ab[s] for s in streams}
    pw = {}
    for s in streams:
        lb = l_ab[s].astype(BF16)
        pw[s] = _dot(lb, _block_diag(lb))
    for _ in range(int(math.log2(C)) - 2):
        for s in streams:
            tp = _dot(jnp.concatenate([t_inv[s], pw[s]], axis=0).astype(BF16), _block_diag(pw[s].astype(BF16)))
            t_inv[s] = t_inv[s] + tp[:C]
            pw[s] = tp[C:]
    for s in streams:
        t_inv[s] = t_inv[s] + _dot(t_inv[s].astype(BF16), _block_diag(pw[s].astype(BF16)))
    av = {}
    for s in streams:
        av[s] = _dot(jnp.concatenate([a_ak[s], a_rk[s]], axis=0).astype(BF16), _block_diag(blk(vb, s)))
    wu = {}
    for s in streams:
        rhs = jnp.concatenate([_block_diag(blk(a_t, s)), _block_diag(av[s][:C].astype(BF16))], axis=1)
        wu[s] = _dot(t_inv[s].astype(BF16), rhs).astype(BF16)
    q_t, y0, mw, nn = {}, {}, {}, {}
    for s in streams:
        rhs = jnp.concatenate([_block_diag(wu[s][:, :GROUP_W]), _block_diag(wu[s][:, GROUP_W:])], axis=1)
        qy = _dot(a_rb[s], rhs)
        q_t[s] = (blk(r_t, s) + qy[:, :GROUP_W]).astype(BF16)
        y0[s] = av[s][C:] + qy[:, GROUP_W:]
    for s in streams:
        mw[s] = jnp.where(same_head, _dot_tn(wu[s][:, :GROUP_W], blk(b_e, s)), 0.0).astype(BF16)
        uv = jnp.concatenate([wu[s][:, GROUP_W:], blk(vb, s)], axis=0)
        bk = jnp.concatenate([blk(b_e, s), blk(k_e, s)], axis=0)
        nn[s] = jnp.where(same_head, _dot_tn(uv, bk), 0.0)

    state = [s_ref[q] for q in range(n_groups)]
    ys = {}
    for c in range(n_chunks):
        for q in range(n_groups):
            s = (c, q)
            s0b = state[q].astype(BF16)
            ys[s] = y0[s] + _dot_nt(q_t[s], s0b)
            state[q] = state[q] * blk(p_c, s)[:1, :] + _dot(s0b, mw[s]) + nn[s]
    for q in range(n_groups):
        s_ref[q] = state[q]

    y = jnp.concatenate([jnp.concatenate([ys[(c, q)] for q in range(n_groups)], axis=1)
                         for c in range(n_chunks)], axis=0)
    mu = _head_sums(y, ones_bd) * (1.0 / HEAD_DIM)
    d = y - mu
    var = _head_sums(d * d, ones_bd) * (1.0 / HEAD_DIM)
    yn = d * lax.rsqrt(var + LNX_EPS) * lng_ref[...] + lnb_ref[...]
    y_ref[...] = ((yn + bonus) * g).astype(y_ref.dtype)


def _rwkv(rkv, lora, w0, w2, a0, a2, g2, k_k, k_a, r_k, ln_g, ln_b, *, tb=256):
    T = rkv.shape[0]
    grid = (T // tb,)
    vec = _resident((1, RWKV_WIDTH))
    return pl.pallas_call(
        functools.partial(_rwkv_kernel, n_chunks=tb // CHUNK),
        grid=grid,
        in_specs=[
            pl.BlockSpec((tb, RWKV_WIDTH), lambda i: (i, 0)),
            pl.BlockSpec((tb, RWKV_WIDTH), lambda i: (i, 1)),
            pl.BlockSpec((tb, RWKV_WIDTH), lambda i: (i, 2)),
            pl.BlockSpec((tb, LORA_PAD), lambda i: (i, 0)),
            vec, _resident(w2.shape), vec, _resident(a2.shape), _resident(g2.shape),
            vec, vec, vec, vec, vec,
        ],
        out_specs=pl.BlockSpec((tb, RWKV_WIDTH), lambda i: (i, 0)),
        out_shape=jax.ShapeDtypeStruct((T, RWKV_WIDTH), BF16),
        scratch_shapes=[pltpu.VMEM((RWKV_WIDTH // GROUP_W, GROUP_W, GROUP_W), F32)],
        compiler_params=pltpu.CompilerParams(dimension_semantics=("arbitrary",),
                                             vmem_limit_bytes=VMEM_LIMIT),
        name="rwkv7",
    )(rkv, rkv, rkv, lora, w0, w2, a0, a2, g2, k_k, k_a, r_k, ln_g, ln_b)


def _rope(x, cos_l, sin_a, sin_b):
    return x * cos_l + pltpu.roll(x, LANES - ROT_DIM // 2, axis=1) * sin_a + pltpu.roll(x, ROT_DIM // 2, axis=1) * sin_b


def _attn_kernel(sink_ref, q_ref, kv_ref, kvp_ref, cos_ref, sa_ref, sb_ref, cosp_ref, sap_ref, sbp_ref,
                 g_ref, o_ref):
    n = pl.program_id(0)
    B = WINDOW
    cos_l, sin_a, sin_b = cos_ref[...], sa_ref[...], sb_ref[...]
    kv = kv_ref[...]
    kvp = kvp_ref[...]
    k_cur = _rope(kv[:, :LANES], cos_l, sin_a, sin_b)
    k_prev = _rope(kvp[:, :LANES], cosp_ref[...], sap_ref[...], sbp_ref[...])
    k_win = jnp.concatenate([k_prev, k_cur], axis=0)
    v_win = jnp.concatenate([kvp[:, LANES:], kv[:, LANES:]], axis=0)
    lane = lax.broadcasted_iota(jnp.int32, (2 * B, LANES), 1)
    low = lane < HEAD_DIM
    k_sw = pltpu.roll(k_win, HEAD_DIM, axis=1)
    v_sw = pltpu.roll(v_win, HEAD_DIM, axis=1)
    k_dup = [jnp.where(low, k_win, k_sw).astype(BF16), jnp.where(low, k_sw, k_win).astype(BF16)]
    v_dup = [jnp.where(low, v_win, v_sw).astype(BF16), jnp.where(low, v_sw, v_win).astype(BF16)]

    qi = lax.broadcasted_iota(jnp.int32, (2 * B, 2 * B), 0) % B
    si = lax.broadcasted_iota(jnp.int32, (2 * B, 2 * B), 1)
    allowed = (si > qi) & (si <= qi + B) & ((si >= B) | (n > 0))
    top = lax.broadcasted_iota(jnp.int32, (2 * B, 1), 0) < B
    qlow = lax.broadcasted_iota(jnp.int32, (B, LANES), 1) < HEAD_DIM
    pairs_per_kv = (ATTN_HEADS // KV_HEADS) // 2
    outs = []
    for p in range(ATTN_HEADS // 2):
        kvh = p // pairs_per_kv
        q = _rope(q_ref[:, p * LANES:(p + 1) * LANES], cos_l, sin_a, sin_b) * (HEAD_DIM ** -0.5)
        zero = jnp.zeros_like(q)
        q_st = jnp.concatenate([jnp.where(qlow, q, zero), jnp.where(qlow, zero, q)], axis=0).astype(BF16)
        s = _dot_nt(q_st, k_dup[kvh])
        s = jnp.where(allowed, s, -1e30)
        sink = jnp.where(top, sink_ref[2 * p], sink_ref[2 * p + 1])
        m = jnp.maximum(jnp.max(s, axis=-1, keepdims=True), sink)
        e = jnp.exp(s - m)
        denom = jnp.sum(e, axis=-1, keepdims=True) + jnp.exp(sink - m)
        o_st = _dot(e.astype(BF16), v_dup[kvh]) / denom
        outs.append(jnp.where(qlow, o_st[:B], o_st[B:]))
    o = jnp.concatenate(outs, axis=1)
    o_ref[...] = _rms(o, g_ref[...]).astype(o_ref.dtype)


def _attn(attn_in, sinks, cos_l, sin_a, sin_b, g):
    T = attn_in.shape[0]
    B = WINDOW
    grid = (T // B,)
    kv_blk = ATTN_WIDTH // (2 * KV_WIDTH)
    cur = lambda n: (n, 0)
    prev = lambda n: (jnp.maximum(n - 1, 0), 0)
    tab = lambda im: pl.BlockSpec((B, LANES), im)
    return pl.pallas_call(
        _attn_kernel,
        grid=grid,
        in_specs=[
            pl.BlockSpec(memory_space=pltpu.SMEM),
            pl.BlockSpec((B, ATTN_WIDTH), cur),
            pl.BlockSpec((B, 2 * KV_WIDTH), lambda n: (n, kv_blk)),
            pl.BlockSpec((B, 2 * KV_WIDTH), lambda n: (jnp.maximum(n - 1, 0), kv_blk)),
            tab(cur), tab(cur), tab(cur), tab(prev), tab(prev), tab(prev),
            _resident((1, ATTN_WIDTH)),
        ],
        out_specs=pl.BlockSpec((B, ATTN_WIDTH), cur),
        out_shape=jax.ShapeDtypeStruct((T, ATTN_WIDTH), BF16),
        compiler_params=pltpu.CompilerParams(dimension_semantics=("arbitrary",),
                                             vmem_limit_bytes=VMEM_LIMIT),
        name="swa_attn",
    )(sinks, attn_in, attn_in, attn_in, cos_l, sin_a, sin_b, cos_l, sin_a, sin_b, g)


def _out_proj_kernel(x_ref, yr_ref, ya_ref, wr_ref, wa_ref, o_ref):
    o_ref[...] = x_ref[...] + _dot(yr_ref[...], wr_ref[...]) + _dot(ya_ref[...], wa_ref[...])


def _out_proj(x2d, y_rwkv, y_attn, w_r, w_a, *, tm=512):
    T = x2d.shape[0]
    row = lambda w: pl.BlockSpec((tm, w), lambda i: (i, 0))
    return pl.pallas_call(
        _out_proj_kernel,
        grid=(T // tm,),
        in_specs=[row(D_MODEL), row(RWKV_WIDTH), row(ATTN_WIDTH), _resident(w_r.shape), _resident(w_a.shape)],
        out_specs=row(D_MODEL),
        out_shape=jax.ShapeDtypeStruct((T, D_MODEL), F32),
        compiler_params=pltpu.CompilerParams(dimension_semantics=("arbitrary",),
                                             vmem_limit_bytes=VMEM_LIMIT),
        name="out_proj",
    )(x2d, y_rwkv, y_attn, w_r, w_a)


def _ffn_kernel(x_ref, xp_ref, g_ref, wup_ref, wgate_ref, cw_ref, cb_ref, wdown_ref, gf_ref, o_ref,
                hext_ref, acc_ref):
    i = pl.program_id(0)
    f = pl.program_id(1)

    @pl.when(f == 0)
    def _():
        g = g_ref[...]
        hext_ref[PREV_ROWS:, :] = _rms(x_ref[...], g).astype(BF16)
        hp = jnp.where(i > 0, _rms(xp_ref[...], g), 0.0)
        hext_ref[:PREV_ROWS, :] = hp.astype(BF16)
        acc_ref[...] = jnp.zeros_like(acc_ref)

    hx = hext_ref[...]
    u = _dot(hx, wup_ref[...])
    cw = cw_ref[...]
    conv = cb_ref[...] + cw[2:3, :] * u + cw[1:2, :] * pltpu.roll(u, 1, axis=0) + cw[0:1, :] * pltpu.roll(u, 2, axis=0)
    conv = conv[PREV_ROWS:, :]
    gate = _dot(hext_ref[PREV_ROWS:, :], wgate_ref[...])
    act = conv * _sigmoid(conv) * gate
    acc_ref[...] += _dot(act.astype(BF16), wdown_ref[...])

    @pl.when(f == pl.num_programs(1) - 1)
    def _():
        o_ref[...] = _rms(x_ref[...] + acc_ref[...], gf_ref[...])


def _ffn(x1, g, w_up, w_gate, conv_w, conv_b, w_down, g_final, *, tm=512, tf=512):
    T = x1.shape[0]
    prev_blocks = tm // PREV_ROWS
    return pl.pallas_call(
        _ffn_kernel,
        grid=(T // tm, D_FF // tf),
        in_specs=[
            pl.BlockSpec((tm, D_MODEL), lambda i, f: (i, 0)),
            pl.BlockSpec((PREV_ROWS, D_MODEL), lambda i, f: (jnp.maximum(i * prev_blocks - 1, 0), 0)),
            pl.BlockSpec((1, D_MODEL), lambda i, f: (0, 0)),
            pl.BlockSpec((D_MODEL, tf), lambda i, f: (0, f)),
            pl.BlockSpec((D_MODEL, tf), lambda i, f: (0, f)),
            pl.BlockSpec((3, tf), lambda i, f: (0, f)),
            pl.BlockSpec((1, tf), lambda i, f: (0, f)),
            pl.BlockSpec((tf, D_MODEL), lambda i, f: (f, 0)),
            pl.BlockSpec((1, D_MODEL), lambda i, f: (0, 0)),
        ],
        out_specs=pl.BlockSpec((tm, D_MODEL), lambda i, f: (i, 0)),
        out_shape=jax.ShapeDtypeStruct((T, D_MODEL), F32),
        scratch_shapes=[pltpu.VMEM((tm + PREV_ROWS, D_MODEL), BF16), pltpu.VMEM((tm, D_MODEL), F32)],
        compiler_params=pltpu.CompilerParams(dimension_semantics=("arbitrary", "arbitrary"),
                                             vmem_limit_bytes=VMEM_LIMIT),
        name="ffn",
    )(x1, x1, g, w_up, w_gate, conv_w, conv_b, w_down, g_final)


def _pad_cols(a, width):
    return jnp.pad(a, ((0, 0), (0, width - a.shape[1])))


def _pad_rows(a, rows):
    return jnp.pad(a, ((0, rows - a.shape[0]), (0, 0)))


def _rope_lane_tables(T):
    half = ROT_DIM // 2
    inv_freq = ROPE_THETA ** (-jnp.arange(0, ROT_DIM, 2, dtype=F32) / ROT_DIM)
    ang = jnp.arange(T, dtype=F32)[:, None] * inv_freq[None, :]
    cos, sin = jnp.cos(ang), jnp.sin(ang)
    ones = jnp.ones((T, HEAD_DIM - ROT_DIM), F32)
    zeros_h = jnp.zeros((T, half), F32)
    zeros_r = jnp.zeros((T, HEAD_DIM - ROT_DIM), F32)
    cos_h = jnp.concatenate([cos, cos, ones], axis=1)
    sa_h = jnp.concatenate([-sin, zeros_h, zeros_r], axis=1)
    sb_h = jnp.concatenate([zeros_h, sin, zeros_r], axis=1)
    rep = LANES // HEAD_DIM
    return jnp.tile(cos_h, (1, rep)), jnp.tile(sa_h, (1, rep)), jnp.tile(sb_h, (1, rep))


def kernel(x, ln_mix_g, w_in, b_attn_qkv, rwkv_shift_mu, rwkv_w0, rwkv_w2, rwkv_a0, rwkv_a2, rwkv_g2, rwkv_k_k, rwkv_k_a, rwkv_r_k, rwkv_lnx_g, rwkv_lnx_b, attn_sinks, attn_out_g, w_out, ln_ffn_g, ffn_w_up, ffn_w_gate, ffn_conv_w, ffn_conv_b, ffn_w_down, ln_final_g):
    B, T, _ = x.shape
    assert B == 1 and ln_mix_g.shape[0] == 1
    l = 0
    x2d = x.reshape(T, D_MODEL)
    row = lambda a: a.reshape(1, -1)

    w = w_in[l]
    mu = rwkv_shift_mu[l]
    o_w, o_a, o_g = 3 * RWKV_WIDTH, 3 * RWKV_WIDTH + W_LORA, 3 * RWKV_WIDTH + W_LORA + A_LORA
    o_q = o_g + G_LORA

    def lora_cols(a):
        return jnp.concatenate([_pad_cols(a[:, o_w:o_a], LANES), _pad_cols(a[:, o_a:o_g], LANES),
                                _pad_cols(a[:, o_g:o_q], 2 * LANES)], axis=1)

    w_rkv = w[:, :o_w].astype(BF16)
    w_lora = lora_cols(w).astype(BF16)
    w_attn = w[:, o_q:].astype(BF16)
    mu2 = row(mu)
    rkv, lora, attn_in = _in_proj(x2d, row(ln_mix_g[l]), w_rkv, w_lora, w_attn, mu2[:, :o_w],
                                  lora_cols(mu2), row(b_attn_qkv[l]))

    y_rwkv = _rwkv(rkv, lora, row(rwkv_w0[l]), _pad_rows(rwkv_w2[l], LANES).astype(BF16),
                   row(rwkv_a0[l]), _pad_rows(rwkv_a2[l], LANES).astype(BF16),
                   _pad_rows(rwkv_g2[l], 2 * LANES).astype(BF16), row(rwkv_k_k[l]), row(rwkv_k_a[l]),
                   row(rwkv_r_k[l]), row(rwkv_lnx_g[l]), row(rwkv_lnx_b[l]))

    cos_l, sin_a, sin_b = _rope_lane_tables(T)
    y_attn = _attn(attn_in, attn_sinks[l], cos_l, sin_a, sin_b, row(attn_out_g[l]))

    wo = w_out[l].astype(BF16)
    x1 = _out_proj(x2d, y_rwkv, y_attn, wo[:RWKV_WIDTH], wo[RWKV_WIDTH:])

    out = _ffn(x1, row(ln_ffn_g[l]), ffn_w_up[l].astype(BF16), ffn_w_gate[l].astype(BF16),
               ffn_conv_w[l], row(ffn_conv_b[l]), ffn_w_down[l].astype(BF16), row(ln_final_g))
    return out.reshape(B, T, D_MODEL)
```

```python
import functools
import math

import jax
import jax.numpy as jnp
from jax import lax
from jax.experimental import pallas as pl
from jax.experimental.pallas import tpu as pltpu

F32 = jnp.float32
BF16 = jnp.bfloat16

D_MODEL = 2048
HEAD_DIM = 64
RWKV_WIDTH = 1024
ATTN_WIDTH = 1024
ATTN_HEADS = 16
KV_HEADS = 2
KV_WIDTH = KV_HEADS * HEAD_DIM
W_LORA = 64
A_LORA = 64
G_LORA = 160
WINDOW = 128
ROPE_THETA = 500000.0
ROT_DIM = 16
D_FF = 5632
NORM_EPS = 1e-5
LNX_EPS = 64e-5

LANES = 128
SUBLANES = 8
PREV_ROWS = 16
LORA_PAD = 512
GROUP = 4
GROUP_W = GROUP * HEAD_DIM
CHUNK = 64
SUB_CHUNKS = 4
SUB_BLOCK = SUB_CHUNKS * CHUNK
VMEM_LIMIT = 56 * 1024 * 1024
FFN_VMEM_LIMIT = 60 * 1024 * 1024

assert CHUNK == HEAD_DIM and SUB_CHUNKS <= SUBLANES


def _dot(a, b):
    return jnp.dot(a, b, preferred_element_type=F32)


def _dot_nt(a, b):
    return lax.dot_general(a, b, (((1,), (1,)), ((), ())), preferred_element_type=F32)


def _dot_tn(a, b):
    return lax.dot_general(a, b, (((0,), (0,)), ((), ())), preferred_element_type=F32)


def _sigmoid(x):
    return 1.0 / (1.0 + jnp.exp(-x))


def _sigmoid_tanh(x):
    return 0.5 + 0.5 * jnp.tanh(0.5 * x)


def _rms(xv, g):
    ms = jnp.mean(xv * xv, axis=-1, keepdims=True)
    return xv * lax.rsqrt(ms + NORM_EPS) * g


def _resident(shape):
    nd = len(shape)
    return pl.BlockSpec(shape, lambda *_: (0,) * nd, pipeline_mode=pl.Buffered(1))


def _split_bf16(x):
    hi = x.astype(BF16)
    lo = (x - hi.astype(F32)).astype(BF16)
    return hi, lo


def _same_head_mask():
    r_i = lax.broadcasted_iota(jnp.int32, (GROUP_W, GROUP_W), 0) // HEAD_DIM
    c_i = lax.broadcasted_iota(jnp.int32, (GROUP_W, GROUP_W), 1) // HEAD_DIM
    return r_i == c_i


def _head_sums(x, ones_bd):
    xb = x.astype(BF16)
    return jnp.concatenate([_dot(xb[:, q * GROUP_W:(q + 1) * GROUP_W], ones_bd)
                            for q in range(RWKV_WIDTH // GROUP_W)], axis=1)


def _in_proj_kernel(x_ref, xp_ref, g_ref, w_ref, mu_ref, battn_ref, w0_ref, w2_ref, a0_ref, a2_ref, g2_ref,
                    kk_ref, ka_ref, rk_ref,
                    at_ref, kh_ref, bh_ref, rt_ref, ke_ref, be_ref, vb_ref, bonus_ref, gate_ref, pc_ref, attn_ref,
                    hext_ref, *, col_chunk):
    i = pl.program_id(0)
    tm = x_ref.shape[0]
    n_chunks = tm // CHUNK
    C = CHUNK
    g = g_ref[...]
    hext_ref[PREV_ROWS:, :] = _rms(x_ref[...], g).astype(BF16)
    hp = jnp.where(i > 0, _rms(xp_ref[...], g), 0.0)
    hext_ref[:PREV_ROWS, :] = hp.astype(BF16)
    hx = hext_ref[...]

    def shifted(base, width):
        outs = []
        for c0 in range(base, base + width, col_chunk):
            z = _dot(hx, w_ref[:, c0:c0 + col_chunk])
            zp = pltpu.roll(z, 1, axis=0)
            outs.append((z + mu_ref[:, c0:c0 + col_chunk] * (zp - z))[PREV_ROWS:, :])
        return jnp.concatenate(outs, axis=1)

    def attn_cols(c0):
        base = LORA_PAD + 3 * RWKV_WIDTH
        c1 = min(c0 + col_chunk, attn_ref.shape[1])
        attn_ref[:, c0:c1] = _dot(hext_ref[PREV_ROWS:, :], w_ref[:, base + c0:base + c1]) + battn_ref[:, c0:c1]

    ones_bd = _same_head_mask().astype(BF16)
    t_i = lax.broadcasted_iota(jnp.int32, (tm, tm), 0)
    t_j = lax.broadcasted_iota(jnp.int32, (tm, tm), 1)
    tri = ((t_j <= t_i) & (t_j // C == t_i // C)).astype(BF16)
    attn_chunks = list(range(0, attn_ref.shape[1], col_chunk))

    lo = shifted(0, LORA_PAD)
    k = shifted(LORA_PAD, RWKV_WIDTH)
    xw = lo[:, 0:LANES]
    xa = lo[:, LANES:2 * LANES]
    xg = lo[:, 2 * LANES:LORA_PAD]
    u = w0_ref[...] + _dot(jnp.tanh(xw).astype(BF16), w2_ref[...])
    lw = -math.exp(-0.5) * _sigmoid_tanh(u)
    a = _sigmoid_tanh(a0_ref[...] + _dot(xa.astype(BF16), a2_ref[...]))
    gate_ref[...] = _dot(_sigmoid_tanh(xg).astype(BF16), g2_ref[...]).astype(gate_ref.dtype)
    lw_hi, lw_lo = _split_bf16(lw)
    cl = _dot(tri, lw_hi) + _dot(tri, lw_lo)
    last_rows = [cl[(c + 1) * C - 1:(c + 1) * C, :] for c in range(n_chunks)]
    cl_last = jnp.concatenate([jnp.broadcast_to(row, (C, RWKV_WIDTH)) for row in last_rows], axis=0)
    pc_ref[...] = jnp.concatenate([jnp.exp(row) for row in last_rows]
                                  + [jnp.zeros((SUBLANES - n_chunks, RWKV_WIDTH), F32)], axis=0)

    r = shifted(LORA_PAD + RWKV_WIDTH, RWKV_WIDTH)
    kkraw = k * kk_ref[...]
    k2 = k * (1.0 + (a - 1.0) * ka_ref[...])

    v = shifted(LORA_PAD + 2 * RWKV_WIDTH, RWKV_WIDTH)
    sums = _head_sums(jnp.concatenate([kkraw * kkraw, r * k2 * rk_ref[...]], axis=0), ones_bd)
    kk = kkraw / jnp.maximum(jnp.sqrt(sums[:tm]), 1e-12)
    b = kk * a

    attn_cols(attn_chunks[0])
    at_ref[...] = (-kk * jnp.exp(cl - lw)).astype(at_ref.dtype)
    e_neg = jnp.exp(-cl)
    kh_ref[...] = (k2 * e_neg).astype(kh_ref.dtype)
    bh_ref[...] = (b * e_neg).astype(bh_ref.dtype)

    for c0 in attn_chunks[1:2]:
        attn_cols(c0)
    rt_ref[...] = (r * jnp.exp(cl)).astype(rt_ref.dtype)
    e_end = jnp.exp(cl_last - cl)
    ke_ref[...] = (k2 * e_end).astype(ke_ref.dtype)
    be_ref[...] = (b * e_end).astype(be_ref.dtype)

    for c0 in attn_chunks[2:]:
        attn_cols(c0)
    bonus_ref[...] = (sums[tm:] * v).astype(bonus_ref.dtype)
    vb_ref[...] = v.astype(vb_ref.dtype)


def _in_proj(x2d, g, w_all, mu_shift, b_attn, w0, w2, a0, a2, g2, k_k, k_a, r_k, *, tm=SUB_BLOCK, col_chunk=512):
    T = x2d.shape[0]
    n_attn = b_attn.shape[1]
    assert w_all.shape[1] == 3 * RWKV_WIDTH + LORA_PAD + n_attn and tm % CHUNK == 0
    grid = (T // tm,)
    prev_blocks = tm // PREV_ROWS
    vec = _resident((1, RWKV_WIDTH))
    tile = pl.BlockSpec((tm, RWKV_WIDTH), lambda i: (i, 0))
    tile_shape = jax.ShapeDtypeStruct((T, RWKV_WIDTH), BF16)
    return pl.pallas_call(
        functools.partial(_in_proj_kernel, col_chunk=col_chunk),
        grid=grid,
        in_specs=[
            pl.BlockSpec((tm, D_MODEL), lambda i: (i, 0)),
            pl.BlockSpec((PREV_ROWS, D_MODEL), lambda i: (jnp.maximum(i * prev_blocks - 1, 0), 0)),
            _resident((1, D_MODEL)),
            _resident(w_all.shape), _resident(mu_shift.shape), _resident(b_attn.shape),
            vec, _resident(w2.shape), vec, _resident(a2.shape), _resident(g2.shape), vec, vec, vec,
        ],
        out_specs=[tile] * 9 + [
            pl.BlockSpec((None, SUBLANES, RWKV_WIDTH), lambda i: (i, 0, 0)),
            pl.BlockSpec((tm, n_attn), lambda i: (i, 0)),
        ],
        out_shape=[tile_shape] * 9 + [
            jax.ShapeDtypeStruct((T // tm, SUBLANES, RWKV_WIDTH), F32),
            jax.ShapeDtypeStruct((T, n_attn), F32),
        ],
        scratch_shapes=[pltpu.VMEM((tm + PREV_ROWS, D_MODEL), BF16)],
        compiler_params=pltpu.CompilerParams(dimension_semantics=("arbitrary",),
                                             vmem_limit_bytes=VMEM_LIMIT),
        name="in_proj",
    )(x2d, x2d, g, w_all, mu_shift, b_attn, w0, w2, a0, a2, g2, k_k, k_a, r_k)


def _block_diag(x):
    blk = lax.broadcasted_iota(jnp.int32, x.shape, 1) // HEAD_DIM
    zero = jnp.zeros_like(x)
    return jnp.concatenate([jnp.where(blk == h, x, zero) for h in range(GROUP)], axis=0)


def _rwkv_kernel(at_ref, kh_ref, bh_ref, rt_ref, ke_ref, be_ref, vb_ref, bonus_ref, gate_ref, pc_ref,
                 lng_ref, lnb_ref, y_ref, s_ref, *, n_sub):
    C = CHUNK
    SB = SUB_BLOCK
    n_groups = RWKV_WIDTH // GROUP_W

    @pl.when(pl.program_id(0) == 0)
    def _():
        s_ref[...] = jnp.zeros_like(s_ref)

    row = lax.broadcasted_iota(jnp.int32, (C, GROUP_W), 0)
    col = lax.broadcasted_iota(jnp.int32, (C, GROUP_W), 1) % C
    strict = col < row
    incl = col <= row
    eye = (col == row).astype(F32)
    same_head = _same_head_mask()
    ones_bd = same_head.astype(BF16)
    streams = [(c, q) for c in range(SUB_CHUNKS) for q in range(n_groups)]
    state = [s_ref[q] for q in range(n_groups)]
    subs = [dict() for _ in range(n_sub)]

    def stream_pieces(h):
        H = subs[h]
        a_ak, a_rk, l_ab, a_rb, t_inv, pw, av, wu, q_t, y0, mw, nn, ys = ({} for _ in range(13))

        def blk(ref, s):
            c, q = s
            return ref[h * SB + c * C:h * SB + (c + 1) * C, q * GROUP_W:(q + 1) * GROUP_W]

        def scores():
            for s in streams:
                ar = jnp.concatenate([blk(at_ref, s), blk(rt_ref, s)], axis=0)
                ak = _dot_nt(ar, _block_diag(blk(kh_ref, s)))
                ab = _dot_nt(ar, _block_diag(blk(bh_ref, s)))
                a_ak[s] = jnp.where(strict, ak[:C], 0.0)
                a_rk[s] = jnp.where(incl, ak[C:], 0.0)
                l_ab[s] = jnp.where(strict, ab[:C], 0.0)
                a_rb[s] = jnp.where(incl, ab[C:], 0.0).astype(BF16)

        def inv_first():
            for s in streams:
                t_inv[s] = eye + l_ab[s]
                lb = l_ab[s].astype(BF16)
                pw[s] = _dot(lb, _block_diag(lb))

        def inv_level():
            for s in streams:
                tp = _dot(jnp.concatenate([t_inv[s], pw[s]], axis=0).astype(BF16),
                          _block_diag(pw[s].astype(BF16)))
                t_inv[s] = t_inv[s] + tp[:C]
                pw[s] = tp[C:]

        def inv_last():
            for s in streams:
                t_inv[s] = t_inv[s] + _dot(t_inv[s].astype(BF16), _block_diag(pw[s].astype(BF16)))

        def values():
            for s in streams:
                av[s] = _dot(jnp.concatenate([a_ak[s], a_rk[s]], axis=0).astype(BF16),
                             _block_diag(blk(vb_ref, s)))

        def solve():
            for s in streams:
                rhs = jnp.concatenate([_block_diag(blk(at_ref, s)), _block_diag(av[s][:C].astype(BF16))], axis=1)
                wu[s] = _dot(t_inv[s].astype(BF16), rhs).astype(BF16)

        def outputs():
            for s in streams:
                rhs = jnp.concatenate([_block_diag(wu[s][:, :GROUP_W]), _block_diag(wu[s][:, GROUP_W:])], axis=1)
                qy = _dot(a_rb[s], rhs)
                q_t[s] = (blk(rt_ref, s).astype(F32) + qy[:, :GROUP_W]).astype(BF16)
                y0[s] = av[s][C:] + qy[:, GROUP_W:]

        def transitions():
            for s in streams:
                mw[s] = jnp.where(same_head, _dot_tn(wu[s][:, :GROUP_W], blk(be_ref, s)), 0.0).astype(BF16)
                uv = jnp.concatenate([wu[s][:, GROUP_W:], blk(vb_ref, s)], axis=0)
                bk = jnp.concatenate([blk(be_ref, s), blk(ke_ref, s)], axis=0)
                nn[s] = jnp.where(same_head, _dot_tn(uv, bk), 0.0)

        def carry(c):
            def f():
                for q in range(n_groups):
                    s = (c, q)
                    s0b = state[q].astype(BF16)
                    ys[s] = y0[s] + _dot_nt(q_t[s], s0b)
                    p_c = pc_ref[h, c:c + 1, q * GROUP_W:(q + 1) * GROUP_W]
                    state[q] = state[q] * p_c + _dot(s0b, mw[s]) + nn[s]
                if c == SUB_CHUNKS - 1:
                    H["y"] = jnp.concatenate([jnp.concatenate([ys[(cc, q)] for q in range(n_groups)], axis=1)
                                              for cc in range(SUB_CHUNKS)], axis=0)
            return f

        return ([scores, inv_first] + [inv_level] * (int(math.log2(C)) - 2)
                + [inv_last, values, solve, outputs, transitions] + [carry(c) for c in range(SUB_CHUNKS)])

    def epilogue_pieces(h):
        H = subs[h]
        rows = slice(h * SB, (h + 1) * SB)

        def center():
            mu = _head_sums(H["y"], ones_bd) * (1.0 / HEAD_DIM)
            H["d"] = H["y"] - mu

        def scale():
            d = H["d"]
            var = _head_sums(d * d, ones_bd) * (1.0 / HEAD_DIM)
            yn = d * lax.rsqrt(var + LNX_EPS) * lng_ref[...] + lnb_ref[...]
            y_ref[rows, :] = ((yn + bonus_ref[rows, :].astype(F32)) * gate_ref[rows, :].astype(F32)).astype(y_ref.dtype)

        return [center, scale]

    def run_merged(main, *others):
        n = len(main)
        for i, piece in enumerate(main):
            piece()
            for other in others:
                for j, o in enumerate(other):
                    if j * n // len(other) == i:
                        o()

    for h in range(n_sub):
        run_merged(stream_pieces(h), *([epilogue_pieces(h - 1)] if h >= 1 else []))
    run_merged(epilogue_pieces(n_sub - 1))
    for q in range(n_groups):
        s_ref[q] = state[q]


def _rwkv(a_t, k_h, b_h, r_t, k_e, b_e, v_b, bonus, gate, p_c, ln_g, ln_b, *, tb=2 * SUB_BLOCK):
    T = a_t.shape[0]
    n_sub = tb // SUB_BLOCK
    assert p_c.shape == (T // SUB_BLOCK, SUBLANES, RWKV_WIDTH)
    vec = _resident((1, RWKV_WIDTH))
    tile = pl.BlockSpec((tb, RWKV_WIDTH), lambda i: (i, 0))
    return pl.pallas_call(
        functools.partial(_rwkv_kernel, n_sub=n_sub),
        grid=(T // tb,),
        in_specs=[tile] * 9 + [pl.BlockSpec((n_sub, SUBLANES, RWKV_WIDTH), lambda i: (i, 0, 0)), vec, vec],
        out_specs=tile,
        out_shape=jax.ShapeDtypeStruct((T, RWKV_WIDTH), BF16),
        scratch_shapes=[pltpu.VMEM((RWKV_WIDTH // GROUP_W, GROUP_W, GROUP_W), F32)],
        compiler_params=pltpu.CompilerParams(dimension_semantics=("arbitrary",),
                                             vmem_limit_bytes=VMEM_LIMIT),
        name="rwkv7",
    )(a_t, k_h, b_h, r_t, k_e, b_e, v_b, bonus, gate, p_c, ln_g, ln_b)


def _rope(x, cos_l, sin_a, sin_b):
    return x * cos_l + pltpu.roll(x, LANES - ROT_DIM // 2, axis=1) * sin_a + pltpu.roll(x, ROT_DIM // 2, axis=1) * sin_b


def _attn_kernel(sink_ref, q_ref, kv_ref, kvp_ref, cos_ref, sa_ref, sb_ref, cosp_ref, sap_ref, sbp_ref,
                 g_ref, o_ref):
    n = pl.program_id(0)
    B = WINDOW
    cos_l, sin_a, sin_b = cos_ref[...], sa_ref[...], sb_ref[...]
    kv = kv_ref[...]
    kvp = kvp_ref[...]
    k_cur = _rope(kv[:, :LANES], cos_l, sin_a, sin_b)
    k_prev = _rope(kvp[:, :LANES], cosp_ref[...], sap_ref[...], sbp_ref[...])
    k_win = jnp.concatenate([k_prev, k_cur], axis=0)
    v_win = jnp.concatenate([kvp[:, LANES:], kv[:, LANES:]], axis=0)
    lane = lax.broadcasted_iota(jnp.int32, (2 * B, LANES), 1)
    low = lane < HEAD_DIM
    k_sw = pltpu.roll(k_win, HEAD_DIM, axis=1)
    v_sw = pltpu.roll(v_win, HEAD_DIM, axis=1)
    k_dup = [jnp.where(low, k_win, k_sw).astype(BF16), jnp.where(low, k_sw, k_win).astype(BF16)]
    v_dup = [jnp.where(low, v_win, v_sw).astype(BF16), jnp.where(low, v_sw, v_win).astype(BF16)]

    qi = lax.broadcasted_iota(jnp.int32, (2 * B, 2 * B), 0) % B
    si = lax.broadcasted_iota(jnp.int32, (2 * B, 2 * B), 1)
    allowed = (si > qi) & (si <= qi + B) & ((si >= B) | (n > 0))
    top = lax.broadcasted_iota(jnp.int32, (2 * B, 1), 0) < B
    qlow = lax.broadcasted_iota(jnp.int32, (B, LANES), 1) < HEAD_DIM
    pairs_per_kv = (ATTN_HEADS // KV_HEADS) // 2
    pairs = range(ATTN_HEADS // 2)
    scores, es, denoms, outs = [], [], [], []
    for p in pairs:
        q = _rope(q_ref[:, p * LANES:(p + 1) * LANES], cos_l, sin_a, sin_b) * (HEAD_DIM ** -0.5)
        zero = jnp.zeros_like(q)
        q_st = jnp.concatenate([jnp.where(qlow, q, zero), jnp.where(qlow, zero, q)], axis=0).astype(BF16)
        scores.append(_dot_nt(q_st, k_dup[p // pairs_per_kv]))
    for p in pairs:
        s = jnp.where(allowed, scores[p], -1e30)
        sink = jnp.where(top, sink_ref[2 * p], sink_ref[2 * p + 1])
        m = jnp.maximum(jnp.max(s, axis=-1, keepdims=True), sink)
        e = jnp.exp(s - m)
        denoms.append(jnp.sum(e, axis=-1, keepdims=True) + jnp.exp(sink - m))
        es.append(e.astype(BF16))
    for p in pairs:
        o_st = _dot(es[p], v_dup[p // pairs_per_kv]) / denoms[p]
        outs.append(jnp.where(qlow, o_st[:B], o_st[B:]))
    o = jnp.concatenate(outs, axis=1)
    o_ref[...] = _rms(o, g_ref[...]).astype(o_ref.dtype)


def _attn(attn_in, sinks, cos_l, sin_a, sin_b, g):
    T = attn_in.shape[0]
    B = WINDOW
    grid = (T // B,)
    kv_blk = ATTN_WIDTH // (2 * KV_WIDTH)
    cur = lambda n: (n, 0)
    prev = lambda n: (jnp.maximum(n - 1, 0), 0)
    tab = lambda im: pl.BlockSpec((B, LANES), im)
    return pl.pallas_call(
        _attn_kernel,
        grid=grid,
        in_specs=[
            pl.BlockSpec(memory_space=pltpu.SMEM),
            pl.BlockSpec((B, ATTN_WIDTH), cur),
            pl.BlockSpec((B, 2 * KV_WIDTH), lambda n: (n, kv_blk)),
            pl.BlockSpec((B, 2 * KV_WIDTH), lambda n: (jnp.maximum(n - 1, 0), kv_blk)),
            tab(cur), tab(cur), tab(cur), tab(prev), tab(prev), tab(prev),
            _resident((1, ATTN_WIDTH)),
        ],
        out_specs=pl.BlockSpec((B, ATTN_WIDTH), cur),
        out_shape=jax.ShapeDtypeStruct((T, ATTN_WIDTH), BF16),
        compiler_params=pltpu.CompilerParams(dimension_semantics=("arbitrary",),
                                             vmem_limit_bytes=VMEM_LIMIT),
        name="swa_attn",
    )(sinks, attn_in, attn_in, attn_in, cos_l, sin_a, sin_b, cos_l, sin_a, sin_b, g)


def _out_proj_kernel(x_ref, yr_ref, ya_ref, w_ref, o_ref):
    o_ref[...] = (x_ref[...] + _dot(yr_ref[...], w_ref[:RWKV_WIDTH, :])
                  + _dot(ya_ref[...], w_ref[RWKV_WIDTH:, :]))


def _out_proj(x2d, y_rwkv, y_attn, w, *, tm=512):
    T = x2d.shape[0]
    row = lambda width: pl.BlockSpec((tm, width), lambda i: (i, 0))
    return pl.pallas_call(
        _out_proj_kernel,
        grid=(T // tm,),
        in_specs=[row(D_MODEL), row(RWKV_WIDTH), row(ATTN_WIDTH), _resident(w.shape)],
        out_specs=row(D_MODEL),
        out_shape=jax.ShapeDtypeStruct((T, D_MODEL), F32),
        compiler_params=pltpu.CompilerParams(dimension_semantics=("arbitrary",),
                                             vmem_limit_bytes=VMEM_LIMIT),
        name="out_proj",
    )(x2d, y_rwkv, y_attn, w)


def _ffn_kernel(x_ref, xp_ref, g_ref, wup_ref, wgate_ref, cw_ref, cb_ref, wdown_ref, gf_ref, o_ref, hext_ref):
    i = pl.program_id(0)
    f = pl.program_id(1)

    @pl.when(f == 0)
    def _():
        g = g_ref[...]
        x = x_ref[...]
        hext_ref[PREV_ROWS:, :] = _rms(x, g).astype(BF16)
        hp = jnp.where(i > 0, _rms(xp_ref[...], g), 0.0)
        hext_ref[:PREV_ROWS, :] = hp.astype(BF16)
        o_ref[...] = x

    hx = hext_ref[...]
    u = _dot(hx, wup_ref[...])
    cw = cw_ref[...]
    conv = cb_ref[...] + cw[2:3, :] * u + cw[1:2, :] * pltpu.roll(u, 1, axis=0) + cw[0:1, :] * pltpu.roll(u, 2, axis=0)
    conv = conv[PREV_ROWS:, :]
    gate = _dot(hext_ref[PREV_ROWS:, :], wgate_ref[...])
    act = conv * _sigmoid(conv) * gate
    o_ref[...] += _dot(act.astype(BF16), wdown_ref[...])

    @pl.when(f == pl.num_programs(1) - 1)
    def _():
        o_ref[...] = _rms(o_ref[...], gf_ref[...])


def _ffn(x1, g, w_up, w_gate, conv_w, conv_b, w_down, g_final, *, tm=1024, tf=512):
    T = x1.shape[0]
    prev_blocks = tm // PREV_ROWS
    return pl.pallas_call(
        _ffn_kernel,
        grid=(T // tm, D_FF // tf),
        in_specs=[
            pl.BlockSpec((tm, D_MODEL), lambda i, f: (i, 0)),
            pl.BlockSpec((PREV_ROWS, D_MODEL), lambda i, f: (jnp.maximum(i * prev_blocks - 1, 0), 0)),
            pl.BlockSpec((1, D_MODEL), lambda i, f: (0, 0)),
            pl.BlockSpec((D_MODEL, tf), lambda i, f: (0, f)),
            pl.BlockSpec((D_MODEL, tf), lambda i, f: (0, f)),
            pl.BlockSpec((3, tf), lambda i, f: (0, f)),
            pl.BlockSpec((1, tf), lambda i, f: (0, f)),
            pl.BlockSpec((tf, D_MODEL), lambda i, f: (f, 0)),
            pl.BlockSpec((1, D_MODEL), lambda i, f: (0, 0)),
        ],
        out_specs=pl.BlockSpec((tm, D_MODEL), lambda i, f: (i, 0)),
        out_shape=jax.ShapeDtypeStruct((T, D_MODEL), F32),
        scratch_shapes=[pltpu.VMEM((tm + PREV_ROWS, D_MODEL), BF16)],
        compiler_params=pltpu.CompilerParams(dimension_semantics=("arbitrary", "arbitrary"),
                                             vmem_limit_bytes=FFN_VMEM_LIMIT),
        name="ffn",
    )(x1, x1, g, w_up, w_gate, conv_w, conv_b, w_down, g_final)


def _pad_cols(a, width):
    return jnp.pad(a, ((0, 0), (0, width - a.shape[1])))


def _pad_rows(a, rows):
    return jnp.pad(a, ((0, rows - a.shape[0]), (0, 0)))


def _rope_lane_tables(T):
    half = ROT_DIM // 2
    inv_freq = ROPE_THETA ** (-jnp.arange(0, ROT_DIM, 2, dtype=F32) / ROT_DIM)
    ang = jnp.arange(T, dtype=F32)[:, None] * inv_freq[None, :]
    cos, sin = jnp.cos(ang), jnp.sin(ang)
    ones = jnp.ones((T, HEAD_DIM - ROT_DIM), F32)
    zeros_h = jnp.zeros((T, half), F32)
    zeros_r = jnp.zeros((T, HEAD_DIM - ROT_DIM), F32)
    cos_h = jnp.concatenate([cos, cos, ones], axis=1)
    sa_h = jnp.concatenate([-sin, zeros_h, zeros_r], axis=1)
    sb_h = jnp.concatenate([zeros_h, sin, zeros_r], axis=1)
    rep = LANES // HEAD_DIM
    return jnp.tile(cos_h, (1, rep)), jnp.tile(sa_h, (1, rep)), jnp.tile(sb_h, (1, rep))


_IN_PROJ_SEGMENTS = (
    (3 * RWKV_WIDTH, W_LORA, LANES),
    (3 * RWKV_WIDTH + W_LORA, A_LORA, LANES),
    (3 * RWKV_WIDTH + W_LORA + A_LORA, G_LORA, 2 * LANES),
    (RWKV_WIDTH, RWKV_WIDTH, RWKV_WIDTH),
    (0, RWKV_WIDTH, RWKV_WIDTH),
    (2 * RWKV_WIDTH, RWKV_WIDTH, RWKV_WIDTH),
    (3 * RWKV_WIDTH + W_LORA + A_LORA + G_LORA, ATTN_WIDTH + 2 * KV_WIDTH, ATTN_WIDTH + 2 * KV_WIDTH),
)


def _in_proj_cols(a):
    return jnp.concatenate([_pad_cols(a[:, src:src + width], slot) for src, width, slot in _IN_PROJ_SEGMENTS],
                           axis=1)


def _regroup_kernel(w_ref, o_ref):
    dst = 0
    for src, width, slot in _IN_PROJ_SEGMENTS:
        o_ref[:, dst:dst + width] = w_ref[:, src:src + width].astype(o_ref.dtype)
        if slot > width:
            o_ref[:, dst + width:dst + slot] = jnp.zeros((o_ref.shape[0], slot - width), o_ref.dtype)
        dst += slot


def _regroup_w_in(w, *, tr=256):
    rows, cols = w.shape
    out_cols = sum(slot for _, _, slot in _IN_PROJ_SEGMENTS)
    return pl.pallas_call(
        _regroup_kernel,
        grid=(rows // tr,),
        in_specs=[pl.BlockSpec((tr, cols), lambda i: (i, 0))],
        out_specs=pl.BlockSpec((tr, out_cols), lambda i: (i, 0)),
        out_shape=jax.ShapeDtypeStruct((rows, out_cols), BF16),
        compiler_params=pltpu.CompilerParams(dimension_semantics=("arbitrary",), vmem_limit_bytes=VMEM_LIMIT),
        name="regroup_w_in",
    )(w)


def kernel(x, ln_mix_g, w_in, b_attn_qkv, rwkv_shift_mu, rwkv_w0, rwkv_w2, rwkv_a0, rwkv_a2, rwkv_g2, rwkv_k_k, rwkv_k_a, rwkv_r_k, rwkv_lnx_g, rwkv_lnx_b, attn_sinks, attn_out_g, w_out, ln_ffn_g, ffn_w_up, ffn_w_gate, ffn_conv_w, ffn_conv_b, ffn_w_down, ln_final_g):
    B, T, _ = x.shape
    assert B == 1 and ln_mix_g.shape[0] == 1
    l = 0
    x2d = x.reshape(T, D_MODEL)
    row = lambda a: a.reshape(1, -1)

    w_all = _regroup_w_in(w_in[l])
    mu_shift = _in_proj_cols(row(rwkv_shift_mu[l]))
    (a_t, k_h, b_h, r_t, k_e, b_e, v_b, bonus, gate, p_c, attn_in) = _in_proj(
        x2d, row(ln_mix_g[l]), w_all, mu_shift, row(b_attn_qkv[l]),
        row(rwkv_w0[l]), _pad_rows(rwkv_w2[l], LANES).astype(BF16),
        row(rwkv_a0[l]), _pad_rows(rwkv_a2[l], LANES).astype(BF16),
        _pad_rows(rwkv_g2[l], 2 * LANES).astype(BF16), row(rwkv_k_k[l]), row(rwkv_k_a[l]), row(rwkv_r_k[l]))

    y_rwkv = _rwkv(a_t, k_h, b_h, r_t, k_e, b_e, v_b, bonus, gate, p_c,
                   row(rwkv_lnx_g[l]), row(rwkv_lnx_b[l]))

    cos_l, sin_a, sin_b = _rope_lane_tables(T)
    y_attn = _attn(attn_in, attn_sinks[l], cos_l, sin_a, sin_b, row(attn_out_g[l]))

    x1 = _out_proj(x2d, y_rwkv, y_attn, w_out[l].astype(BF16))

    out = _ffn(x1, row(ln_ffn_g[l]), ffn_w_up[l].astype(BF16), ffn_w_gate[l].astype(BF16),
               ffn_conv_w[l], row(ffn_conv_b[l]), ffn_w_down[l].astype(BF16), row(ln_final_g))
    return out.reshape(B, T, D_MODEL)
```

```python
import functools
import math

import jax
import jax.numpy as jnp
from jax import lax
from jax.experimental import pallas as pl
from jax.experimental.pallas import tpu as pltpu

F32 = jnp.float32
BF16 = jnp.bfloat16

D_MODEL = 2048
HEAD_DIM = 64
RWKV_WIDTH = 1024
ATTN_WIDTH = 1024
ATTN_HEADS = 16
KV_HEADS = 2
KV_WIDTH = KV_HEADS * HEAD_DIM
W_LORA = 64
A_LORA = 64
G_LORA = 160
WINDOW = 128
ROPE_THETA = 500000.0
ROT_DIM = 16
D_FF = 5632
NORM_EPS = 1e-5
LNX_EPS = 64e-5

LANES = 128
SUBLANES = 8
PREV_ROWS = 16
LORA_PAD = 512
GROUP = 4
GROUP_W = GROUP * HEAD_DIM
CHUNK = 64
SUB_CHUNKS = 4
SUB_BLOCK = SUB_CHUNKS * CHUNK
VMEM_LIMIT = 56 * 1024 * 1024
FFN_VMEM_LIMIT = 60 * 1024 * 1024

assert CHUNK == HEAD_DIM and SUB_CHUNKS <= SUBLANES


def _dot(a, b):
    return jnp.dot(a, b, preferred_element_type=F32)


def _dot_nt(a, b):
    return lax.dot_general(a, b, (((1,), (1,)), ((), ())), preferred_element_type=F32)


def _dot_tn(a, b):
    return lax.dot_general(a, b, (((0,), (0,)), ((), ())), preferred_element_type=F32)


def _sigmoid(x):
    return 1.0 / (1.0 + jnp.exp(-x))


def _sigmoid_tanh(x):
    return 0.5 + 0.5 * jnp.tanh(0.5 * x)


def _rms(xv, g):
    ms = jnp.mean(xv * xv, axis=-1, keepdims=True)
    return xv * lax.rsqrt(ms + NORM_EPS) * g


def _resident(shape):
    nd = len(shape)
    return pl.BlockSpec(shape, lambda *_: (0,) * nd, pipeline_mode=pl.Buffered(1))


def _split_bf16(x):
    hi = x.astype(BF16)
    lo = (x - hi.astype(F32)).astype(BF16)
    return hi, lo


def _same_head_mask():
    r_i = lax.broadcasted_iota(jnp.int32, (GROUP_W, GROUP_W), 0) // HEAD_DIM
    c_i = lax.broadcasted_iota(jnp.int32, (GROUP_W, GROUP_W), 1) // HEAD_DIM
    return r_i == c_i


def _head_sums(x, ones_bd):
    xb = x.astype(BF16)
    return jnp.concatenate([_dot(xb[:, q * GROUP_W:(q + 1) * GROUP_W], ones_bd)
                            for q in range(RWKV_WIDTH // GROUP_W)], axis=1)


def _in_proj_kernel(x_ref, xp_ref, g_ref, w_ref, mu_ref, battn_ref, w0_ref, w2_ref, a0_ref, a2_ref, g2_ref,
                    kk_ref, ka_ref, rk_ref,
                    at_ref, kh_ref, bh_ref, rt_ref, ke_ref, be_ref, vb_ref, bonus_ref, gate_ref, pc_ref, attn_ref,
                    hext_ref, *, col_chunk):
    i = pl.program_id(0)
    tm = x_ref.shape[0]
    n_chunks = tm // CHUNK
    C = CHUNK
    g = g_ref[...]
    hext_ref[PREV_ROWS:, :] = _rms(x_ref[...], g).astype(BF16)
    hp = jnp.where(i > 0, _rms(xp_ref[...], g), 0.0)
    hext_ref[:PREV_ROWS, :] = hp.astype(BF16)
    hx = hext_ref[...]

    def shifted(base, width):
        outs = []
        for c0 in range(base, base + width, col_chunk):
            z = _dot_nt(hx, w_ref[c0:c0 + col_chunk, :])
            zp = pltpu.roll(z, 1, axis=0)
            outs.append((z + mu_ref[:, c0:c0 + col_chunk] * (zp - z))[PREV_ROWS:, :])
        return jnp.concatenate(outs, axis=1)

    def attn_cols(c0):
        base = LORA_PAD + 3 * RWKV_WIDTH
        c1 = min(c0 + col_chunk, attn_ref.shape[1])
        attn_ref[:, c0:c1] = _dot_nt(hext_ref[PREV_ROWS:, :], w_ref[base + c0:base + c1, :]) + battn_ref[:, c0:c1]

    ones_bd = _same_head_mask().astype(BF16)
    t_i = lax.broadcasted_iota(jnp.int32, (tm, tm), 0)
    t_j = lax.broadcasted_iota(jnp.int32, (tm, tm), 1)
    tri = ((t_j <= t_i) & (t_j // C == t_i // C)).astype(BF16)
    attn_chunks = list(range(0, attn_ref.shape[1], col_chunk))

    lo = shifted(0, LORA_PAD)
    k = shifted(LORA_PAD, RWKV_WIDTH)
    xw = lo[:, 0:LANES]
    xa = lo[:, LANES:2 * LANES]
    xg = lo[:, 2 * LANES:LORA_PAD]
    u = w0_ref[...] + _dot(jnp.tanh(xw).astype(BF16), w2_ref[...])
    lw = -math.exp(-0.5) * _sigmoid_tanh(u)
    a = _sigmoid_tanh(a0_ref[...] + _dot(xa.astype(BF16), a2_ref[...]))
    gate_ref[...] = _dot(_sigmoid_tanh(xg).astype(BF16), g2_ref[...]).astype(gate_ref.dtype)
    lw_hi, lw_lo = _split_bf16(lw)
    cl = _dot(tri, lw_hi) + _dot(tri, lw_lo)
    last_rows = [cl[(c + 1) * C - 1:(c + 1) * C, :] for c in range(n_chunks)]
    cl_last = jnp.concatenate([jnp.broadcast_to(row, (C, RWKV_WIDTH)) for row in last_rows], axis=0)
    pc_ref[...] = jnp.concatenate([jnp.exp(row) for row in last_rows]
                                  + [jnp.zeros((SUBLANES - n_chunks, RWKV_WIDTH), F32)], axis=0)

    r = shifted(LORA_PAD + RWKV_WIDTH, RWKV_WIDTH)
    kkraw = k * kk_ref[...]
    k2 = k * (1.0 + (a - 1.0) * ka_ref[...])

    v = shifted(LORA_PAD + 2 * RWKV_WIDTH, RWKV_WIDTH)
    sums = _head_sums(jnp.concatenate([kkraw * kkraw, r * k2 * rk_ref[...]], axis=0), ones_bd)
    kk = kkraw / jnp.maximum(jnp.sqrt(sums[:tm]), 1e-12)
    b = kk * a

    attn_cols(attn_chunks[0])
    at_ref[...] = (-kk * jnp.exp(cl - lw)).astype(at_ref.dtype)
    e_neg = jnp.exp(-cl)
    kh_ref[...] = (k2 * e_neg).astype(kh_ref.dtype)
    bh_ref[...] = (b * e_neg).astype(bh_ref.dtype)

    for c0 in attn_chunks[1:2]:
        attn_cols(c0)
    rt_ref[...] = (r * jnp.exp(cl)).astype(rt_ref.dtype)
    e_end = jnp.exp(cl_last - cl)
    ke_ref[...] = (k2 * e_end).astype(ke_ref.dtype)
    be_ref[...] = (b * e_end).astype(be_ref.dtype)

    for c0 in attn_chunks[2:]:
        attn_cols(c0)
    bonus_ref[...] = (sums[tm:] * v).astype(bonus_ref.dtype)
    vb_ref[...] = v.astype(vb_ref.dtype)


def _in_proj(x2d, g, w_all, mu_shift, b_attn, w0, w2, a0, a2, g2, k_k, k_a, r_k, *, tm=SUB_BLOCK, col_chunk=512):
    T = x2d.shape[0]
    n_attn = b_attn.shape[1]
    assert w_all.shape == (3 * RWKV_WIDTH + LORA_PAD + n_attn, D_MODEL) and tm % CHUNK == 0
    grid = (T // tm,)
    prev_blocks = tm // PREV_ROWS
    vec = _resident((1, RWKV_WIDTH))
    tile = pl.BlockSpec((tm, RWKV_WIDTH), lambda i: (i, 0))
    tile_shape = jax.ShapeDtypeStruct((T, RWKV_WIDTH), BF16)
    return pl.pallas_call(
        functools.partial(_in_proj_kernel, col_chunk=col_chunk),
        grid=grid,
        in_specs=[
            pl.BlockSpec((tm, D_MODEL), lambda i: (i, 0)),
            pl.BlockSpec((PREV_ROWS, D_MODEL), lambda i: (jnp.maximum(i * prev_blocks - 1, 0), 0)),
            _resident((1, D_MODEL)),
            _resident(w_all.shape), _resident(mu_shift.shape), _resident(b_attn.shape),
            vec, _resident(w2.shape), vec, _resident(a2.shape), _resident(g2.shape), vec, vec, vec,
        ],
        out_specs=[tile] * 9 + [
            pl.BlockSpec((None, SUBLANES, RWKV_WIDTH), lambda i: (i, 0, 0)),
            pl.BlockSpec((tm, n_attn), lambda i: (i, 0)),
        ],
        out_shape=[tile_shape] * 9 + [
            jax.ShapeDtypeStruct((T // tm, SUBLANES, RWKV_WIDTH), F32),
            jax.ShapeDtypeStruct((T, n_attn), F32),
        ],
        scratch_shapes=[pltpu.VMEM((tm + PREV_ROWS, D_MODEL), BF16)],
        compiler_params=pltpu.CompilerParams(dimension_semantics=("arbitrary",),
                                             vmem_limit_bytes=VMEM_LIMIT),
        name="in_proj",
    )(x2d, x2d, g, w_all, mu_shift, b_attn, w0, w2, a0, a2, g2, k_k, k_a, r_k)


def _block_diag(x):
    blk = lax.broadcasted_iota(jnp.int32, x.shape, 1) // HEAD_DIM
    zero = jnp.zeros_like(x)
    return jnp.concatenate([jnp.where(blk == h, x, zero) for h in range(GROUP)], axis=0)


def _rwkv_kernel(at_ref, kh_ref, bh_ref, rt_ref, ke_ref, be_ref, vb_ref, bonus_ref, gate_ref, pc_ref,
                 lng_ref, lnb_ref, y_ref, s_ref, *, n_sub):
    C = CHUNK
    SB = SUB_BLOCK
    n_groups = RWKV_WIDTH // GROUP_W

    @pl.when(pl.program_id(0) == 0)
    def _():
        s_ref[...] = jnp.zeros_like(s_ref)

    row = lax.broadcasted_iota(jnp.int32, (C, GROUP_W), 0)
    col = lax.broadcasted_iota(jnp.int32, (C, GROUP_W), 1) % C
    strict = col < row
    incl = col <= row
    eye = (col == row).astype(F32)
    same_head = _same_head_mask()
    ones_bd = same_head.astype(BF16)
    streams = [(c, q) for c in range(SUB_CHUNKS) for q in range(n_groups)]
    state = [s_ref[q] for q in range(n_groups)]
    subs = [dict() for _ in range(n_sub)]

    def stream_pieces(h):
        H = subs[h]
        a_ak, a_rk, l_ab, a_rb, t_inv, pw, av, wu, q_t, y0, mw, nn, ys = ({} for _ in range(13))

        def blk(ref, s):
            c, q = s
            return ref[h * SB + c * C:h * SB + (c + 1) * C, q * GROUP_W:(q + 1) * GROUP_W]

        def scores():
            for s in streams:
                ar = jnp.concatenate([blk(at_ref, s), blk(rt_ref, s)], axis=0)
                ak = _dot_nt(ar, _block_diag(blk(kh_ref, s)))
                ab = _dot_nt(ar, _block_diag(blk(bh_ref, s)))
                a_ak[s] = jnp.where(strict, ak[:C], 0.0)
                a_rk[s] = jnp.where(incl, ak[C:], 0.0)
                l_ab[s] = jnp.where(strict, ab[:C], 0.0)
                a_rb[s] = jnp.where(incl, ab[C:], 0.0).astype(BF16)

        def inv_first():
            for s in streams:
                t_inv[s] = eye + l_ab[s]
                lb = l_ab[s].astype(BF16)
                pw[s] = _dot(lb, _block_diag(lb))

        def inv_level():
            for s in streams:
                tp = _dot(jnp.concatenate([t_inv[s], pw[s]], axis=0).astype(BF16),
                          _block_diag(pw[s].astype(BF16)))
                t_inv[s] = t_inv[s] + tp[:C]
                pw[s] = tp[C:]

        def inv_last():
            for s in streams:
                t_inv[s] = t_inv[s] + _dot(t_inv[s].astype(BF16), _block_diag(pw[s].astype(BF16)))

        def values():
            for s in streams:
                av[s] = _dot(jnp.concatenate([a_ak[s], a_rk[s]], axis=0).astype(BF16),
                             _block_diag(blk(vb_ref, s)))

        def solve():
            for s in streams:
                rhs = jnp.concatenate([_block_diag(blk(at_ref, s)), _block_diag(av[s][:C].astype(BF16))], axis=1)
                wu[s] = _dot(t_inv[s].astype(BF16), rhs).astype(BF16)

        def outputs():
            for s in streams:
                rhs = jnp.concatenate([_block_diag(wu[s][:, :GROUP_W]), _block_diag(wu[s][:, GROUP_W:])], axis=1)
                qy = _dot(a_rb[s], rhs)
                q_t[s] = (blk(rt_ref, s).astype(F32) + qy[:, :GROUP_W]).astype(BF16)
                y0[s] = av[s][C:] + qy[:, GROUP_W:]

        def transitions():
            for s in streams:
                mw[s] = jnp.where(same_head, _dot_tn(wu[s][:, :GROUP_W], blk(be_ref, s)), 0.0).astype(BF16)
                uv = jnp.concatenate([wu[s][:, GROUP_W:], blk(vb_ref, s)], axis=0)
                bk = jnp.concatenate([blk(be_ref, s), blk(ke_ref, s)], axis=0)
                nn[s] = jnp.where(same_head, _dot_tn(uv, bk), 0.0)

        def carry(c):
            def f():
                for q in range(n_groups):
                    s = (c, q)
                    s0b = state[q].astype(BF16)
                    ys[s] = y0[s] + _dot_nt(q_t[s], s0b)
                    p_c = pc_ref[h, c:c + 1, q * GROUP_W:(q + 1) * GROUP_W]
                    state[q] = state[q] * p_c + _dot(s0b, mw[s]) + nn[s]
                if c == SUB_CHUNKS - 1:
                    H["y"] = jnp.concatenate([jnp.concatenate([ys[(cc, q)] for q in range(n_groups)], axis=1)
                                              for cc in range(SUB_CHUNKS)], axis=0)
            return f

        return ([scores, inv_first] + [inv_level] * (int(math.log2(C)) - 2)
                + [inv_last, values, solve, outputs, transitions] + [carry(c) for c in range(SUB_CHUNKS)])

    def epilogue_pieces(h):
        H = subs[h]
        rows = slice(h * SB, (h + 1) * SB)

        def center():
            mu = _head_sums(H["y"], ones_bd) * (1.0 / HEAD_DIM)
            H["d"] = H["y"] - mu

        def scale():
            d = H["d"]
            var = _head_sums(d * d, ones_bd) * (1.0 / HEAD_DIM)
            yn = d * lax.rsqrt(var + LNX_EPS) * lng_ref[...] + lnb_ref[...]
            y_ref[rows, :] = ((yn + bonus_ref[rows, :].astype(F32)) * gate_ref[rows, :].astype(F32)).astype(y_ref.dtype)

        return [center, scale]

    def run_merged(main, *others):
        n = len(main)
        for i, piece in enumerate(main):
            piece()
            for other in others:
                for j, o in enumerate(other):
                    if j * n // len(other) == i:
                        o()

    for h in range(n_sub):
        run_merged(stream_pieces(h), *([epilogue_pieces(h - 1)] if h >= 1 else []))
    run_merged(epilogue_pieces(n_sub - 1))
    for q in range(n_groups):
        s_ref[q] = state[q]


def _rwkv(a_t, k_h, b_h, r_t, k_e, b_e, v_b, bonus, gate, p_c, ln_g, ln_b, *, tb=2 * SUB_BLOCK):
    T = a_t.shape[0]
    n_sub = tb // SUB_BLOCK
    assert p_c.shape == (T // SUB_BLOCK, SUBLANES, RWKV_WIDTH)
    vec = _resident((1, RWKV_WIDTH))
    tile = pl.BlockSpec((tb, RWKV_WIDTH), lambda i: (i, 0))
    return pl.pallas_call(
        functools.partial(_rwkv_kernel, n_sub=n_sub),
        grid=(T // tb,),
        in_specs=[tile] * 9 + [pl.BlockSpec((n_sub, SUBLANES, RWKV_WIDTH), lambda i: (i, 0, 0)), vec, vec],
        out_specs=tile,
        out_shape=jax.ShapeDtypeStruct((T, RWKV_WIDTH), BF16),
        scratch_shapes=[pltpu.VMEM((RWKV_WIDTH // GROUP_W, GROUP_W, GROUP_W), F32)],
        compiler_params=pltpu.CompilerParams(dimension_semantics=("arbitrary",),
                                             vmem_limit_bytes=VMEM_LIMIT),
        name="rwkv7",
    )(a_t, k_h, b_h, r_t, k_e, b_e, v_b, bonus, gate, p_c, ln_g, ln_b)


def _rope(x, cos_l, sin_a, sin_b):
    return x * cos_l + pltpu.roll(x, LANES - ROT_DIM // 2, axis=1) * sin_a + pltpu.roll(x, ROT_DIM // 2, axis=1) * sin_b


def _attn_kernel(sink_ref, q_ref, kv_ref, kvp_ref, cos_ref, sa_ref, sb_ref, cosp_ref, sap_ref, sbp_ref,
                 g_ref, o_ref):
    n = pl.program_id(0)
    B = WINDOW
    cos_l, sin_a, sin_b = cos_ref[...], sa_ref[...], sb_ref[...]
    kv = kv_ref[...]
    kvp = kvp_ref[...]
    k_cur = _rope(kv[:, :LANES], cos_l, sin_a, sin_b)
    k_prev = _rope(kvp[:, :LANES], cosp_ref[...], sap_ref[...], sbp_ref[...])
    k_win = jnp.concatenate([k_prev, k_cur], axis=0)
    v_win = jnp.concatenate([kvp[:, LANES:], kv[:, LANES:]], axis=0)
    lane = lax.broadcasted_iota(jnp.int32, (2 * B, LANES), 1)
    low = lane < HEAD_DIM
    k_sw = pltpu.roll(k_win, HEAD_DIM, axis=1)
    v_sw = pltpu.roll(v_win, HEAD_DIM, axis=1)
    k_dup = [jnp.where(low, k_win, k_sw).astype(BF16), jnp.where(low, k_sw, k_win).astype(BF16)]
    v_dup = [jnp.where(low, v_win, v_sw).astype(BF16), jnp.where(low, v_sw, v_win).astype(BF16)]

    qi = lax.broadcasted_iota(jnp.int32, (2 * B, 2 * B), 0) % B
    si = lax.broadcasted_iota(jnp.int32, (2 * B, 2 * B), 1)
    allowed = (si > qi) & (si <= qi + B) & ((si >= B) | (n > 0))
    top = lax.broadcasted_iota(jnp.int32, (2 * B, 1), 0) < B
    qlow = lax.broadcasted_iota(jnp.int32, (B, LANES), 1) < HEAD_DIM
    pairs_per_kv = (ATTN_HEADS // KV_HEADS) // 2
    pairs = range(ATTN_HEADS // 2)
    scores, es, denoms, outs = [], [], [], []
    for p in pairs:
        q = _rope(q_ref[:, p * LANES:(p + 1) * LANES], cos_l, sin_a, sin_b) * (HEAD_DIM ** -0.5)
        zero = jnp.zeros_like(q)
        q_st = jnp.concatenate([jnp.where(qlow, q, zero), jnp.where(qlow, zero, q)], axis=0).astype(BF16)
        scores.append(_dot_nt(q_st, k_dup[p // pairs_per_kv]))
    for p in pairs:
        s = jnp.where(allowed, scores[p], -1e30)
        sink = jnp.where(top, sink_ref[2 * p], sink_ref[2 * p + 1])
        m = jnp.maximum(jnp.max(s, axis=-1, keepdims=True), sink)
        e = jnp.exp(s - m)
        denoms.append(jnp.sum(e, axis=-1, keepdims=True) + jnp.exp(sink - m))
        es.append(e.astype(BF16))
    for p in pairs:
        o_st = _dot(es[p], v_dup[p // pairs_per_kv]) / denoms[p]
        outs.append(jnp.where(qlow, o_st[:B], o_st[B:]))
    o = jnp.concatenate(outs, axis=1)
    o_ref[...] = _rms(o, g_ref[...]).astype(o_ref.dtype)


def _attn(attn_in, sinks, cos_l, sin_a, sin_b, g):
    T = attn_in.shape[0]
    B = WINDOW
    grid = (T // B,)
    kv_blk = ATTN_WIDTH // (2 * KV_WIDTH)
    cur = lambda n: (n, 0)
    prev = lambda n: (jnp.maximum(n - 1, 0), 0)
    tab = lambda im: pl.BlockSpec((B, LANES), im)
    return pl.pallas_call(
        _attn_kernel,
        grid=grid,
        in_specs=[
            pl.BlockSpec(memory_space=pltpu.SMEM),
            pl.BlockSpec((B, ATTN_WIDTH), cur),
            pl.BlockSpec((B, 2 * KV_WIDTH), lambda n: (n, kv_blk)),
            pl.BlockSpec((B, 2 * KV_WIDTH), lambda n: (jnp.maximum(n - 1, 0), kv_blk)),
            tab(cur), tab(cur), tab(cur), tab(prev), tab(prev), tab(prev),
            _resident((1, ATTN_WIDTH)),
        ],
        out_specs=pl.BlockSpec((B, ATTN_WIDTH), cur),
        out_shape=jax.ShapeDtypeStruct((T, ATTN_WIDTH), BF16),
        compiler_params=pltpu.CompilerParams(dimension_semantics=("arbitrary",),
                                             vmem_limit_bytes=VMEM_LIMIT),
        name="swa_attn",
    )(sinks, attn_in, attn_in, attn_in, cos_l, sin_a, sin_b, cos_l, sin_a, sin_b, g)


def _out_proj_kernel(x_ref, yr_ref, ya_ref, w_ref, o_ref):
    o_ref[...] = (x_ref[...] + _dot(yr_ref[...], w_ref[:RWKV_WIDTH, :])
                  + _dot(ya_ref[...], w_ref[RWKV_WIDTH:, :]))


def _out_proj(x2d, y_rwkv, y_attn, w, *, tm=512):
    T = x2d.shape[0]
    row = lambda width: pl.BlockSpec((tm, width), lambda i: (i, 0))
    return pl.pallas_call(
        _out_proj_kernel,
        grid=(T // tm,),
        in_specs=[row(D_MODEL), row(RWKV_WIDTH), row(ATTN_WIDTH), _resident(w.shape)],
        out_specs=row(D_MODEL),
        out_shape=jax.ShapeDtypeStruct((T, D_MODEL), F32),
        compiler_params=pltpu.CompilerParams(dimension_semantics=("arbitrary",),
                                             vmem_limit_bytes=VMEM_LIMIT),
        name="out_proj",
    )(x2d, y_rwkv, y_attn, w)


def _ffn_kernel(x_ref, xp_ref, g_ref, wup_ref, wgate_ref, cw_ref, cb_ref, wdown_ref, gf_ref, o_ref, hext_ref):
    i = pl.program_id(0)
    f = pl.program_id(1)

    @pl.when(f == 0)
    def _():
        g = g_ref[...]
        x = x_ref[...]
        hext_ref[PREV_ROWS:, :] = _rms(x, g).astype(BF16)
        hp = jnp.where(i > 0, _rms(xp_ref[...], g), 0.0)
        hext_ref[:PREV_ROWS, :] = hp.astype(BF16)
        o_ref[...] = x

    hx = hext_ref[...]
    u = _dot(hx, wup_ref[...])
    cw = cw_ref[...]
    conv = cb_ref[...] + cw[2:3, :] * u + cw[1:2, :] * pltpu.roll(u, 1, axis=0) + cw[0:1, :] * pltpu.roll(u, 2, axis=0)
    conv = conv[PREV_ROWS:, :]
    gate = _dot(hext_ref[PREV_ROWS:, :], wgate_ref[...])
    act = conv * _sigmoid(conv) * gate
    o_ref[...] += _dot(act.astype(BF16), wdown_ref[...])

    @pl.when(f == pl.num_programs(1) - 1)
    def _():
        o_ref[...] = _rms(o_ref[...], gf_ref[...])


def _ffn(x1, g, w_up, w_gate, conv_w, conv_b, w_down, g_final, *, tm=1024, tf=512):
    T = x1.shape[0]
    prev_blocks = tm // PREV_ROWS
    return pl.pallas_call(
        _ffn_kernel,
        grid=(T // tm, D_FF // tf),
        in_specs=[
            pl.BlockSpec((tm, D_MODEL), lambda i, f: (i, 0)),
            pl.BlockSpec((PREV_ROWS, D_MODEL), lambda i, f: (jnp.maximum(i * prev_blocks - 1, 0), 0)),
            pl.BlockSpec((1, D_MODEL), lambda i, f: (0, 0)),
            pl.BlockSpec((D_MODEL, tf), lambda i, f: (0, f)),
            pl.BlockSpec((D_MODEL, tf), lambda i, f: (0, f)),
            pl.BlockSpec((3, tf), lambda i, f: (0, f)),
            pl.BlockSpec((1, tf), lambda i, f: (0, f)),
            pl.BlockSpec((tf, D_MODEL), lambda i, f: (f, 0)),
            pl.BlockSpec((1, D_MODEL), lambda i, f: (0, 0)),
        ],
        out_specs=pl.BlockSpec((tm, D_MODEL), lambda i, f: (i, 0)),
        out_shape=jax.ShapeDtypeStruct((T, D_MODEL), F32),
        scratch_shapes=[pltpu.VMEM((tm + PREV_ROWS, D_MODEL), BF16)],
        compiler_params=pltpu.CompilerParams(dimension_semantics=("arbitrary", "arbitrary"),
                                             vmem_limit_bytes=FFN_VMEM_LIMIT),
        name="ffn",
    )(x1, x1, g, w_up, w_gate, conv_w, conv_b, w_down, g_final)


def _pad_cols(a, width):
    return jnp.pad(a, ((0, 0), (0, width - a.shape[1])))


def _pad_rows(a, rows):
    return jnp.pad(a, ((0, rows - a.shape[0]), (0, 0)))


def _rope_lane_tables(T):
    half = ROT_DIM // 2
    inv_freq = ROPE_THETA ** (-jnp.arange(0, ROT_DIM, 2, dtype=F32) / ROT_DIM)
    ang = jnp.arange(T, dtype=F32)[:, None] * inv_freq[None, :]
    cos, sin = jnp.cos(ang), jnp.sin(ang)
    ones = jnp.ones((T, HEAD_DIM - ROT_DIM), F32)
    zeros_h = jnp.zeros((T, half), F32)
    zeros_r = jnp.zeros((T, HEAD_DIM - ROT_DIM), F32)
    cos_h = jnp.concatenate([cos, cos, ones], axis=1)
    sa_h = jnp.concatenate([-sin, zeros_h, zeros_r], axis=1)
    sb_h = jnp.concatenate([zeros_h, sin, zeros_r], axis=1)
    rep = LANES // HEAD_DIM
    return jnp.tile(cos_h, (1, rep)), jnp.tile(sa_h, (1, rep)), jnp.tile(sb_h, (1, rep))


_IN_PROJ_SEGMENTS = (
    (3 * RWKV_WIDTH, W_LORA, LANES),
    (3 * RWKV_WIDTH + W_LORA, A_LORA, LANES),
    (3 * RWKV_WIDTH + W_LORA + A_LORA, G_LORA, 2 * LANES),
    (RWKV_WIDTH, RWKV_WIDTH, RWKV_WIDTH),
    (0, RWKV_WIDTH, RWKV_WIDTH),
    (2 * RWKV_WIDTH, RWKV_WIDTH, RWKV_WIDTH),
    (3 * RWKV_WIDTH + W_LORA + A_LORA + G_LORA, ATTN_WIDTH + 2 * KV_WIDTH, ATTN_WIDTH + 2 * KV_WIDTH),
)


def _in_proj_cols(a):
    return jnp.concatenate([_pad_cols(a[:, src:src + width], slot) for src, width, slot in _IN_PROJ_SEGMENTS],
                           axis=1)


def _regroup_kernel(wt_ref, o_ref):
    dst = 0
    for src, width, slot in _IN_PROJ_SEGMENTS:
        o_ref[dst:dst + width, :] = wt_ref[src:src + width, :].astype(o_ref.dtype)
        if slot > width:
            o_ref[dst + width:dst + slot, :] = jnp.zeros((slot - width, o_ref.shape[1]), o_ref.dtype)
        dst += slot


def _regroup_w_in_t(w_t, *, tc=256):
    n_out, n_in = w_t.shape
    out_rows = sum(slot for _, _, slot in _IN_PROJ_SEGMENTS)
    return pl.pallas_call(
        _regroup_kernel,
        grid=(n_in // tc,),
        in_specs=[pl.BlockSpec((n_out, tc), lambda i: (0, i))],
        out_specs=pl.BlockSpec((out_rows, tc), lambda i: (0, i)),
        out_shape=jax.ShapeDtypeStruct((out_rows, n_in), BF16),
        compiler_params=pltpu.CompilerParams(dimension_semantics=("arbitrary",), vmem_limit_bytes=VMEM_LIMIT),
        name="regroup_w_in",
    )(w_t)


def kernel(x, ln_mix_g, w_in, b_attn_qkv, rwkv_shift_mu, rwkv_w0, rwkv_w2, rwkv_a0, rwkv_a2, rwkv_g2, rwkv_k_k, rwkv_k_a, rwkv_r_k, rwkv_lnx_g, rwkv_lnx_b, attn_sinks, attn_out_g, w_out, ln_ffn_g, ffn_w_up, ffn_w_gate, ffn_conv_w, ffn_conv_b, ffn_w_down, ln_final_g):
    B, T, _ = x.shape
    assert B == 1 and ln_mix_g.shape[0] == 1
    l = 0
    x2d = x.reshape(T, D_MODEL)
    row = lambda a: a.reshape(1, -1)

    w_all = _regroup_w_in_t(jnp.swapaxes(w_in[l], 0, 1))
    mu_shift = _in_proj_cols(row(rwkv_shift_mu[l]))
    (a_t, k_h, b_h, r_t, k_e, b_e, v_b, bonus, gate, p_c, attn_in) = _in_proj(
        x2d, row(ln_mix_g[l]), w_all, mu_shift, row(b_attn_qkv[l]),
        row(rwkv_w0[l]), _pad_rows(rwkv_w2[l], LANES).astype(BF16),
        row(rwkv_a0[l]), _pad_rows(rwkv_a2[l], LANES).astype(BF16),
        _pad_rows(rwkv_g2[l], 2 * LANES).astype(BF16), row(rwkv_k_k[l]), row(rwkv_k_a[l]), row(rwkv_r_k[l]))

    y_rwkv = _rwkv(a_t, k_h, b_h, r_t, k_e, b_e, v_b, bonus, gate, p_c,
                   row(rwkv_lnx_g[l]), row(rwkv_lnx_b[l]))

    cos_l, sin_a, sin_b = _rope_lane_tables(T)
    y_attn = _attn(attn_in, attn_sinks[l], cos_l, sin_a, sin_b, row(attn_out_g[l]))

    x1 = _out_proj(x2d, y_rwkv, y_attn, w_out[l].astype(BF16))

    out = _ffn(x1, row(ln_ffn_g[l]), ffn_w_up[l].astype(BF16), ffn_w_gate[l].astype(BF16),
               ffn_conv_w[l], row(ffn_conv_b[l]), ffn_w_down[l].astype(BF16), row(ln_final_g))
    return out.reshape(B, T, D_MODEL)
```

```python
import functools
import math

import jax
import jax.numpy as jnp
from jax import lax
from jax.experimental import pallas as pl
from jax.experimental.pallas import tpu as pltpu

F32 = jnp.float32
BF16 = jnp.bfloat16

D_MODEL = 2048
HEAD_DIM = 64
RWKV_WIDTH = 1024
ATTN_WIDTH = 1024
ATTN_HEADS = 16
KV_HEADS = 2
KV_WIDTH = KV_HEADS * HEAD_DIM
W_LORA = 64
A_LORA = 64
G_LORA = 160
WINDOW = 128
ROPE_THETA = 500000.0
ROT_DIM = 16
D_FF = 5632
NORM_EPS = 1e-5
LNX_EPS = 64e-5

LANES = 128
SUBLANES = 8
PREV_ROWS = 16
LORA_PAD = 512
GROUP = 4
GROUP_W = GROUP * HEAD_DIM
CHUNK = 64
SUB_CHUNKS = 4
SUB_BLOCK = SUB_CHUNKS * CHUNK
VMEM_LIMIT = 56 * 1024 * 1024
FFN_VMEM_LIMIT = 60 * 1024 * 1024

assert CHUNK == HEAD_DIM and SUB_CHUNKS <= SUBLANES


def _dot(a, b):
    return jnp.dot(a, b, preferred_element_type=F32)


def _dot_nt(a, b):
    return lax.dot_general(a, b, (((1,), (1,)), ((), ())), preferred_element_type=F32)


def _dot_tn(a, b):
    return lax.dot_general(a, b, (((0,), (0,)), ((), ())), preferred_element_type=F32)


def _sigmoid(x):
    return 1.0 / (1.0 + jnp.exp(-x))


def _sigmoid_tanh(x):
    return 0.5 + 0.5 * jnp.tanh(0.5 * x)


def _rms(xv, g):
    ms = jnp.mean(xv * xv, axis=-1, keepdims=True)
    return xv * lax.rsqrt(ms + NORM_EPS) * g


def _resident(shape):
    nd = len(shape)
    return pl.BlockSpec(shape, lambda *_: (0,) * nd, pipeline_mode=pl.Buffered(1))


def _split_bf16(x):
    hi = x.astype(BF16)
    lo = (x - hi.astype(F32)).astype(BF16)
    return hi, lo


def _same_head_mask():
    r_i = lax.broadcasted_iota(jnp.int32, (GROUP_W, GROUP_W), 0) // HEAD_DIM
    c_i = lax.broadcasted_iota(jnp.int32, (GROUP_W, GROUP_W), 1) // HEAD_DIM
    return r_i == c_i


def _head_sums(x, ones_bd):
    xb = x.astype(BF16)
    return jnp.concatenate([_dot(xb[:, q * GROUP_W:(q + 1) * GROUP_W], ones_bd)
                            for q in range(RWKV_WIDTH // GROUP_W)], axis=1)


def _in_proj_kernel(x_ref, xp_ref, g_ref, w_ref, mu_ref, battn_ref, w0_ref, w2_ref, a0_ref, a2_ref, g2_ref,
                    kk_ref, ka_ref, rk_ref,
                    at_ref, kh_ref, bh_ref, rt_ref, ke_ref, be_ref, vb_ref, bonus_ref, gate_ref, pc_ref, attn_ref,
                    hext_ref, *, col_chunk):
    i = pl.program_id(0)
    tm = x_ref.shape[0]
    n_chunks = tm // CHUNK
    C = CHUNK
    g = g_ref[...]
    hext_ref[PREV_ROWS:, :] = _rms(x_ref[...], g).astype(BF16)
    hp = jnp.where(i > 0, _rms(xp_ref[...], g), 0.0)
    hext_ref[:PREV_ROWS, :] = hp.astype(BF16)
    hx = hext_ref[...]

    def shifted(base, width):
        outs = []
        for c0 in range(base, base + width, col_chunk):
            z = _dot_nt(hx, w_ref[c0:c0 + col_chunk, :])
            zp = pltpu.roll(z, 1, axis=0)
            outs.append((z + mu_ref[:, c0:c0 + col_chunk] * (zp - z))[PREV_ROWS:, :])
        return jnp.concatenate(outs, axis=1)

    def attn_cols(c0):
        base = LORA_PAD + 3 * RWKV_WIDTH
        c1 = min(c0 + col_chunk, attn_ref.shape[1])
        attn_ref[:, c0:c1] = _dot_nt(hext_ref[PREV_ROWS:, :], w_ref[base + c0:base + c1, :]) + battn_ref[:, c0:c1]

    ones_bd = _same_head_mask().astype(BF16)
    t_i = lax.broadcasted_iota(jnp.int32, (tm, tm), 0)
    t_j = lax.broadcasted_iota(jnp.int32, (tm, tm), 1)
    tri = ((t_j <= t_i) & (t_j // C == t_i // C)).astype(BF16)
    attn_chunks = list(range(0, attn_ref.shape[1], col_chunk))

    lo = shifted(0, LORA_PAD)
    k = shifted(LORA_PAD, RWKV_WIDTH)
    xw = lo[:, 0:LANES]
    xa = lo[:, LANES:2 * LANES]
    xg = lo[:, 2 * LANES:LORA_PAD]
    u = w0_ref[...] + _dot(jnp.tanh(xw).astype(BF16), w2_ref[...])
    lw = -math.exp(-0.5) * _sigmoid_tanh(u)
    a = _sigmoid_tanh(a0_ref[...] + _dot(xa.astype(BF16), a2_ref[...]))
    gate_ref[...] = _dot(_sigmoid_tanh(xg).astype(BF16), g2_ref[...]).astype(gate_ref.dtype)
    lw_hi, lw_lo = _split_bf16(lw)
    cl = _dot(tri, lw_hi) + _dot(tri, lw_lo)
    last_rows = [cl[(c + 1) * C - 1:(c + 1) * C, :] for c in range(n_chunks)]
    cl_last = jnp.concatenate([jnp.broadcast_to(row, (C, RWKV_WIDTH)) for row in last_rows], axis=0)
    pc_ref[...] = jnp.concatenate([jnp.exp(row) for row in last_rows]
                                  + [jnp.zeros((SUBLANES - n_chunks, RWKV_WIDTH), F32)], axis=0)

    r = shifted(LORA_PAD + RWKV_WIDTH, RWKV_WIDTH)
    kkraw = k * kk_ref[...]
    k2 = k * (1.0 + (a - 1.0) * ka_ref[...])

    v = shifted(LORA_PAD + 2 * RWKV_WIDTH, RWKV_WIDTH)
    sums = _head_sums(jnp.concatenate([kkraw * kkraw, r * k2 * rk_ref[...]], axis=0), ones_bd)
    kk = kkraw / jnp.maximum(jnp.sqrt(sums[:tm]), 1e-12)
    b = kk * a

    attn_cols(attn_chunks[0])
    at_ref[...] = (-kk * jnp.exp(cl - lw)).astype(at_ref.dtype)
    e_neg = jnp.exp(-cl)
    kh_ref[...] = (k2 * e_neg).astype(kh_ref.dtype)
    bh_ref[...] = (b * e_neg).astype(bh_ref.dtype)

    for c0 in attn_chunks[1:2]:
        attn_cols(c0)
    rt_ref[...] = (r * jnp.exp(cl)).astype(rt_ref.dtype)
    e_end = jnp.exp(cl_last - cl)
    ke_ref[...] = (k2 * e_end).astype(ke_ref.dtype)
    be_ref[...] = (b * e_end).astype(be_ref.dtype)

    for c0 in attn_chunks[2:]:
        attn_cols(c0)
    bonus_ref[...] = (sums[tm:] * v).astype(bonus_ref.dtype)
    vb_ref[...] = v.astype(vb_ref.dtype)


def _in_proj(x2d, g, w_all, mu_shift, b_attn, w0, w2, a0, a2, g2, k_k, k_a, r_k, *, tm=SUB_BLOCK, col_chunk=512):
    T = x2d.shape[0]
    n_attn = b_attn.shape[1]
    assert w_all.shape == (3 * RWKV_WIDTH + LORA_PAD + n_attn, D_MODEL) and tm % CHUNK == 0
    grid = (T // tm,)
    prev_blocks = tm // PREV_ROWS
    vec = _resident((1, RWKV_WIDTH))
    tile = pl.BlockSpec((tm, RWKV_WIDTH), lambda i: (i, 0))
    tile_shape = jax.ShapeDtypeStruct((T, RWKV_WIDTH), BF16)
    return pl.pallas_call(
        functools.partial(_in_proj_kernel, col_chunk=col_chunk),
        grid=grid,
        in_specs=[
            pl.BlockSpec((tm, D_MODEL), lambda i: (i, 0)),
            pl.BlockSpec((PREV_ROWS, D_MODEL), lambda i: (jnp.maximum(i * prev_blocks - 1, 0), 0)),
            _resident((1, D_MODEL)),
            _resident(w_all.shape), _resident(mu_shift.shape), _resident(b_attn.shape),
            vec, _resident(w2.shape), vec, _resident(a2.shape), _resident(g2.shape), vec, vec, vec,
        ],
        out_specs=[tile] * 9 + [
            pl.BlockSpec((None, SUBLANES, RWKV_WIDTH), lambda i: (i, 0, 0)),
            pl.BlockSpec((tm, n_attn), lambda i: (i, 0)),
        ],
        out_shape=[tile_shape] * 9 + [
            jax.ShapeDtypeStruct((T // tm, SUBLANES, RWKV_WIDTH), F32),
            jax.ShapeDtypeStruct((T, n_attn), F32),
        ],
        scratch_shapes=[pltpu.VMEM((tm + PREV_ROWS, D_MODEL), BF16)],
        compiler_params=pltpu.CompilerParams(dimension_semantics=("arbitrary",),
                                             vmem_limit_bytes=VMEM_LIMIT),
        name="in_proj",
    )(x2d, x2d, g, w_all, mu_shift, b_attn, w0, w2, a0, a2, g2, k_k, k_a, r_k)


def _block_diag(x):
    blk = lax.broadcasted_iota(jnp.int32, x.shape, 1) // HEAD_DIM
    zero = jnp.zeros_like(x)
    return jnp.concatenate([jnp.where(blk == h, x, zero) for h in range(GROUP)], axis=0)


def _rwkv_kernel(at_ref, kh_ref, bh_ref, rt_ref, ke_ref, be_ref, vb_ref, bonus_ref, gate_ref, pc_ref,
                 lng_ref, lnb_ref, y_ref, s_ref, *, n_sub):
    C = CHUNK
    SB = SUB_BLOCK
    n_groups = RWKV_WIDTH // GROUP_W

    @pl.when(pl.program_id(0) == 0)
    def _():
        s_ref[...] = jnp.zeros_like(s_ref)

    row = lax.broadcasted_iota(jnp.int32, (C, GROUP_W), 0)
    col = lax.broadcasted_iota(jnp.int32, (C, GROUP_W), 1) % C
    strict = col < row
    incl = col <= row
    eye = (col == row).astype(F32)
    lane_head = lax.broadcasted_iota(jnp.int32, (HEAD_DIM, GROUP_W), 1) // HEAD_DIM
    ones_bd = _same_head_mask().astype(BF16)
    streams = [(c, q) for c in range(SUB_CHUNKS) for q in range(n_groups)]
    state = [s_ref[q] for q in range(n_groups)]
    subs = [dict() for _ in range(n_sub)]

    def blk(ref, h, s):
        c, q = s
        return ref[h * SB + c * C:h * SB + (c + 1) * C, q * GROUP_W:(q + 1) * GROUP_W]

    def local_pieces(h):
        H = subs[h]
        a_ak, a_rk, l_ab, pw = {}, {}, {}, {}
        H["a_rb"], H["t_inv"], H["av"] = {}, {}, {}
        a_rb, t_inv, av = H["a_rb"], H["t_inv"], H["av"]

        def scores():
            for s in streams:
                ar = jnp.concatenate([blk(at_ref, h, s), blk(rt_ref, h, s)], axis=0)
                ak = _dot_nt(ar, _block_diag(blk(kh_ref, h, s)))
                ab = _dot_nt(ar, _block_diag(blk(bh_ref, h, s)))
                a_ak[s] = jnp.where(strict, ak[:C], 0.0)
                a_rk[s] = jnp.where(incl, ak[C:], 0.0)
                l_ab[s] = jnp.where(strict, ab[:C], 0.0)
                a_rb[s] = jnp.where(incl, ab[C:], 0.0).astype(BF16)

        def inv_first():
            for s in streams:
                t_inv[s] = eye + l_ab[s]
                lb = l_ab[s].astype(BF16)
                pw[s] = _dot(lb, _block_diag(lb))

        def inv_level():
            for s in streams:
                tp = _dot(jnp.concatenate([t_inv[s], pw[s]], axis=0).astype(BF16),
                          _block_diag(pw[s].astype(BF16)))
                t_inv[s] = t_inv[s] + tp[:C]
                pw[s] = tp[C:]

        def inv_last():
            for s in streams:
                t_inv[s] = (t_inv[s] + _dot(t_inv[s].astype(BF16), _block_diag(pw[s].astype(BF16)))).astype(BF16)

        def values():
            for s in streams:
                av[s] = _dot(jnp.concatenate([a_ak[s], a_rk[s]], axis=0).astype(BF16),
                             _block_diag(blk(vb_ref, h, s)))

        return [scores, inv_first] + [inv_level] * (int(math.log2(C)) - 2) + [inv_last, values]

    def chain_pieces(h):
        H = subs[h]
        xs, u_b, ys = {}, {}, {}

        def read_state(c):
            def f():
                for q in range(n_groups):
                    s = (c, q)
                    ar = jnp.concatenate([blk(at_ref, h, s), blk(rt_ref, h, s)], axis=0)
                    xs[s] = _dot_nt(ar, _block_diag(state[q].astype(BF16)))
            return f

        def solve(c):
            def f():
                for q in range(n_groups):
                    s = (c, q)
                    x_u = (xs[s][:C] + H["av"][s][:C]).astype(BF16)
                    u_b[s] = _dot(H["t_inv"][s], _block_diag(x_u)).astype(BF16)
            return f

        def update(c):
            def f():
                for q in range(n_groups):
                    s = (c, q)
                    ys[s] = xs[s][C:] + H["av"][s][C:] + _dot(H["a_rb"][s], _block_diag(u_b[s]))
                    vu = jnp.concatenate([blk(vb_ref, h, s), u_b[s]], axis=0)
                    kb = jnp.concatenate([blk(ke_ref, h, s), blk(be_ref, h, s)], axis=0)
                    upd = _dot_tn(vu, kb)
                    upd_c = sum(jnp.where(lane_head == hd, upd[hd * HEAD_DIM:(hd + 1) * HEAD_DIM, :], 0.0)
                                for hd in range(GROUP))
                    p_c = pc_ref[h, c:c + 1, q * GROUP_W:(q + 1) * GROUP_W]
                    state[q] = state[q] * p_c + upd_c
                if c == SUB_CHUNKS - 1:
                    H["y"] = jnp.concatenate([jnp.concatenate([ys[(cc, q)] for q in range(n_groups)], axis=1)
                                              for cc in range(SUB_CHUNKS)], axis=0)
            return f

        return [p for c in range(SUB_CHUNKS) for p in (read_state(c), solve(c), update(c))]

    def epilogue_pieces(h):
        H = subs[h]
        rows = slice(h * SB, (h + 1) * SB)

        def center():
            mu = _head_sums(H["y"], ones_bd) * (1.0 / HEAD_DIM)
            H["d"] = H["y"] - mu

        def scale():
            d = H["d"]
            var = _head_sums(d * d, ones_bd) * (1.0 / HEAD_DIM)
            yn = d * lax.rsqrt(var + LNX_EPS) * lng_ref[...] + lnb_ref[...]
            y_ref[rows, :] = ((yn + bonus_ref[rows, :].astype(F32)) * gate_ref[rows, :].astype(F32)).astype(y_ref.dtype)

        return [center, scale]

    def run_merged(*lists):
        lists = [pieces for pieces in lists if pieces]
        if not lists:
            return
        main = max(lists, key=len)
        others = [pieces for pieces in lists if pieces is not main]
        n = len(main)
        for i, piece in enumerate(main):
            piece()
            for other in others:
                for j, o in enumerate(other):
                    if j * n // len(other) == i:
                        o()

    for h in range(n_sub + 2):
        run_merged(chain_pieces(h - 1) if 1 <= h <= n_sub else [],
                   local_pieces(h) if h < n_sub else [],
                   epilogue_pieces(h - 2) if h >= 2 else [])
    for q in range(n_groups):
        s_ref[q] = state[q]


def _rwkv(a_t, k_h, b_h, r_t, k_e, b_e, v_b, bonus, gate, p_c, ln_g, ln_b, *, tb=2 * SUB_BLOCK):
    T = a_t.shape[0]
    n_sub = tb // SUB_BLOCK
    assert p_c.shape == (T // SUB_BLOCK, SUBLANES, RWKV_WIDTH)
    vec = _resident((1, RWKV_WIDTH))
    tile = pl.BlockSpec((tb, RWKV_WIDTH), lambda i: (i, 0))
    return pl.pallas_call(
        functools.partial(_rwkv_kernel, n_sub=n_sub),
        grid=(T // tb,),
        in_specs=[tile] * 9 + [pl.BlockSpec((n_sub, SUBLANES, RWKV_WIDTH), lambda i: (i, 0, 0)), vec, vec],
        out_specs=tile,
        out_shape=jax.ShapeDtypeStruct((T, RWKV_WIDTH), BF16),
        scratch_shapes=[pltpu.VMEM((RWKV_WIDTH // GROUP_W, HEAD_DIM, GROUP_W), F32)],
        compiler_params=pltpu.CompilerParams(dimension_semantics=("arbitrary",),
                                             vmem_limit_bytes=VMEM_LIMIT),
        name="rwkv7",
    )(a_t, k_h, b_h, r_t, k_e, b_e, v_b, bonus, gate, p_c, ln_g, ln_b)


def _rope(x, cos_l, sin_a, sin_b):
    return x * cos_l + pltpu.roll(x, LANES - ROT_DIM // 2, axis=1) * sin_a + pltpu.roll(x, ROT_DIM // 2, axis=1) * sin_b


def _attn_kernel(sink_ref, q_ref, kv_ref, kvp_ref, cos_ref, sa_ref, sb_ref, cosp_ref, sap_ref, sbp_ref,
                 g_ref, o_ref):
    n = pl.program_id(0)
    B = WINDOW
    cos_l, sin_a, sin_b = cos_ref[...], sa_ref[...], sb_ref[...]
    kv = kv_ref[...]
    kvp = kvp_ref[...]
    k_cur = _rope(kv[:, :LANES], cos_l, sin_a, sin_b)
    k_prev = _rope(kvp[:, :LANES], cosp_ref[...], sap_ref[...], sbp_ref[...])
    k_win = jnp.concatenate([k_prev, k_cur], axis=0)
    v_win = jnp.concatenate([kvp[:, LANES:], kv[:, LANES:]], axis=0)
    lane = lax.broadcasted_iota(jnp.int32, (2 * B, LANES), 1)
    low = lane < HEAD_DIM
    k_sw = pltpu.roll(k_win, HEAD_DIM, axis=1)
    v_sw = pltpu.roll(v_win, HEAD_DIM, axis=1)
    k_dup = [jnp.where(low, k_win, k_sw).astype(BF16), jnp.where(low, k_sw, k_win).astype(BF16)]
    v_dup = [jnp.where(low, v_win, v_sw).astype(BF16), jnp.where(low, v_sw, v_win).astype(BF16)]

    qi = lax.broadcasted_iota(jnp.int32, (2 * B, 2 * B), 0) % B
    si = lax.broadcasted_iota(jnp.int32, (2 * B, 2 * B), 1)
    allowed = (si > qi) & (si <= qi + B) & ((si >= B) | (n > 0))
    top = lax.broadcasted_iota(jnp.int32, (2 * B, 1), 0) < B
    qlow = lax.broadcasted_iota(jnp.int32, (B, LANES), 1) < HEAD_DIM
    pairs_per_kv = (ATTN_HEADS // KV_HEADS) // 2
    pairs = range(ATTN_HEADS // 2)
    scores, es, denoms, outs = [], [], [], []
    for p in pairs:
        q = _rope(q_ref[:, p * LANES:(p + 1) * LANES], cos_l, sin_a, sin_b) * (HEAD_DIM ** -0.5)
        zero = jnp.zeros_like(q)
        q_st = jnp.concatenate([jnp.where(qlow, q, zero), jnp.where(qlow, zero, q)], axis=0).astype(BF16)
        scores.append(_dot_nt(q_st, k_dup[p // pairs_per_kv]))
    for p in pairs:
        s = jnp.where(allowed, scores[p], -1e30)
        sink = jnp.where(top, sink_ref[2 * p], sink_ref[2 * p + 1])
        m = jnp.maximum(jnp.max(s, axis=-1, keepdims=True), sink)
        e = jnp.exp(s - m)
        denoms.append(jnp.sum(e, axis=-1, keepdims=True) + jnp.exp(sink - m))
        es.append(e.astype(BF16))
    for p in pairs:
        o_st = _dot(es[p], v_dup[p // pairs_per_kv]) / denoms[p]
        outs.append(jnp.where(qlow, o_st[:B], o_st[B:]))
    o = jnp.concatenate(outs, axis=1)
    o_ref[...] = _rms(o, g_ref[...]).astype(o_ref.dtype)


def _attn(attn_in, sinks, cos_l, sin_a, sin_b, g):
    T = attn_in.shape[0]
    B = WINDOW
    grid = (T // B,)
    kv_blk = ATTN_WIDTH // (2 * KV_WIDTH)
    cur = lambda n: (n, 0)
    prev = lambda n: (jnp.maximum(n - 1, 0), 0)
    tab = lambda im: pl.BlockSpec((B, LANES), im)
    return pl.pallas_call(
        _attn_kernel,
        grid=grid,
        in_specs=[
            pl.BlockSpec(memory_space=pltpu.SMEM),
            pl.BlockSpec((B, ATTN_WIDTH), cur),
            pl.BlockSpec((B, 2 * KV_WIDTH), lambda n: (n, kv_blk)),
            pl.BlockSpec((B, 2 * KV_WIDTH), lambda n: (jnp.maximum(n - 1, 0), kv_blk)),
            tab(cur), tab(cur), tab(cur), tab(prev), tab(prev), tab(prev),
            _resident((1, ATTN_WIDTH)),
        ],
        out_specs=pl.BlockSpec((B, ATTN_WIDTH), cur),
        out_shape=jax.ShapeDtypeStruct((T, ATTN_WIDTH), BF16),
        compiler_params=pltpu.CompilerParams(dimension_semantics=("arbitrary",),
                                             vmem_limit_bytes=VMEM_LIMIT),
        name="swa_attn",
    )(sinks, attn_in, attn_in, attn_in, cos_l, sin_a, sin_b, cos_l, sin_a, sin_b, g)


def _out_proj_kernel(x_ref, yr_ref, ya_ref, w_ref, o_ref):
    o_ref[...] = (x_ref[...] + _dot(yr_ref[...], w_ref[:RWKV_WIDTH, :])
                  + _dot(ya_ref[...], w_ref[RWKV_WIDTH:, :]))


def _out_proj(x2d, y_rwkv, y_attn, w, *, tm=512):
    T = x2d.shape[0]
    row = lambda width: pl.BlockSpec((tm, width), lambda i: (i, 0))
    return pl.pallas_call(
        _out_proj_kernel,
        grid=(T // tm,),
        in_specs=[row(D_MODEL), row(RWKV_WIDTH), row(ATTN_WIDTH), _resident(w.shape)],
        out_specs=row(D_MODEL),
        out_shape=jax.ShapeDtypeStruct((T, D_MODEL), F32),
        compiler_params=pltpu.CompilerParams(dimension_semantics=("arbitrary",),
                                             vmem_limit_bytes=VMEM_LIMIT),
        name="out_proj",
    )(x2d, y_rwkv, y_attn, w)


def _ffn_kernel(x_ref, xp_ref, g_ref, wup_ref, wgate_ref, cw_ref, cb_ref, wdown_ref, gf_ref, o_ref, hext_ref):
    i = pl.program_id(0)
    f = pl.program_id(1)

    @pl.when(f == 0)
    def _():
        g = g_ref[...]
        x = x_ref[...]
        hext_ref[PREV_ROWS:, :] = _rms(x, g).astype(BF16)
        hp = jnp.where(i > 0, _rms(xp_ref[...], g), 0.0)
        hext_ref[:PREV_ROWS, :] = hp.astype(BF16)
        o_ref[...] = x

    hx = hext_ref[...]
    u = _dot(hx, wup_ref[...])
    cw = cw_ref[...]
    conv = cb_ref[...] + cw[2:3, :] * u + cw[1:2, :] * pltpu.roll(u, 1, axis=0) + cw[0:1, :] * pltpu.roll(u, 2, axis=0)
    conv = conv[PREV_ROWS:, :]
    gate = _dot(hext_ref[PREV_ROWS:, :], wgate_ref[...])
    act = conv * _sigmoid(conv) * gate
    o_ref[...] += _dot(act.astype(BF16), wdown_ref[...])

    @pl.when(f == pl.num_programs(1) - 1)
    def _():
        o_ref[...] = _rms(o_ref[...], gf_ref[...])


def _ffn(x1, g, w_up, w_gate, conv_w, conv_b, w_down, g_final, *, tm=1024, tf=512):
    T = x1.shape[0]
    prev_blocks = tm // PREV_ROWS
    return pl.pallas_call(
        _ffn_kernel,
        grid=(T // tm, D_FF // tf),
        in_specs=[
            pl.BlockSpec((tm, D_MODEL), lambda i, f: (i, 0)),
            pl.BlockSpec((PREV_ROWS, D_MODEL), lambda i, f: (jnp.maximum(i * prev_blocks - 1, 0), 0)),
            pl.BlockSpec((1, D_MODEL), lambda i, f: (0, 0)),
            pl.BlockSpec((D_MODEL, tf), lambda i, f: (0, f)),
            pl.BlockSpec((D_MODEL, tf), lambda i, f: (0, f)),
            pl.BlockSpec((3, tf), lambda i, f: (0, f)),
            pl.BlockSpec((1, tf), lambda i, f: (0, f)),
            pl.BlockSpec((tf, D_MODEL), lambda i, f: (f, 0)),
            pl.BlockSpec((1, D_MODEL), lambda i, f: (0, 0)),
        ],
        out_specs=pl.BlockSpec((tm, D_MODEL), lambda i, f: (i, 0)),
        out_shape=jax.ShapeDtypeStruct((T, D_MODEL), F32),
        scratch_shapes=[pltpu.VMEM((tm + PREV_ROWS, D_MODEL), BF16)],
        compiler_params=pltpu.CompilerParams(dimension_semantics=("arbitrary", "arbitrary"),
                                             vmem_limit_bytes=FFN_VMEM_LIMIT),
        name="ffn",
    )(x1, x1, g, w_up, w_gate, conv_w, conv_b, w_down, g_final)


def _pad_cols(a, width):
    return jnp.pad(a, ((0, 0), (0, width - a.shape[1])))


def _pad_rows(a, rows):
    return jnp.pad(a, ((0, rows - a.shape[0]), (0, 0)))


def _rope_lane_tables(T):
    half = ROT_DIM // 2
    inv_freq = ROPE_THETA ** (-jnp.arange(0, ROT_DIM, 2, dtype=F32) / ROT_DIM)
    ang = jnp.arange(T, dtype=F32)[:, None] * inv_freq[None, :]
    cos, sin = jnp.cos(ang), jnp.sin(ang)
    ones = jnp.ones((T, HEAD_DIM - ROT_DIM), F32)
    zeros_h = jnp.zeros((T, half), F32)
    zeros_r = jnp.zeros((T, HEAD_DIM - ROT_DIM), F32)
    cos_h = jnp.concatenate([cos, cos, ones], axis=1)
    sa_h = jnp.concatenate([-sin, zeros_h, zeros_r], axis=1)
    sb_h = jnp.concatenate([zeros_h, sin, zeros_r], axis=1)
    rep = LANES // HEAD_DIM
    return jnp.tile(cos_h, (1, rep)), jnp.tile(sa_h, (1, rep)), jnp.tile(sb_h, (1, rep))


_IN_PROJ_SEGMENTS = (
    (3 * RWKV_WIDTH, W_LORA, LANES),
    (3 * RWKV_WIDTH + W_LORA, A_LORA, LANES),
    (3 * RWKV_WIDTH + W_LORA + A_LORA, G_LORA, 2 * LANES),
    (RWKV_WIDTH, RWKV_WIDTH, RWKV_WIDTH),
    (0, RWKV_WIDTH, RWKV_WIDTH),
    (2 * RWKV_WIDTH, RWKV_WIDTH, RWKV_WIDTH),
    (3 * RWKV_WIDTH + W_LORA + A_LORA + G_LORA, ATTN_WIDTH + 2 * KV_WIDTH, ATTN_WIDTH + 2 * KV_WIDTH),
)


def _in_proj_cols(a):
    return jnp.concatenate([_pad_cols(a[:, src:src + width], slot) for src, width, slot in _IN_PROJ_SEGMENTS],
                           axis=1)


def _regroup_kernel(wt_ref, o_ref):
    dst = 0
    for src, width, slot in _IN_PROJ_SEGMENTS:
        o_ref[dst:dst + width, :] = wt_ref[src:src + width, :].astype(o_ref.dtype)
        if slot > width:
            o_ref[dst + width:dst + slot, :] = jnp.zeros((slot - width, o_ref.shape[1]), o_ref.dtype)
        dst += slot


def _regroup_w_in_t(w_t, *, tc=256):
    n_out, n_in = w_t.shape
    out_rows = sum(slot for _, _, slot in _IN_PROJ_SEGMENTS)
    return pl.pallas_call(
        _regroup_kernel,
        grid=(n_in // tc,),
        in_specs=[pl.BlockSpec((n_out, tc), lambda i: (0, i))],
        out_specs=pl.BlockSpec((out_rows, tc), lambda i: (0, i)),
        out_shape=jax.ShapeDtypeStruct((out_rows, n_in), BF16),
        compiler_params=pltpu.CompilerParams(dimension_semantics=("arbitrary",), vmem_limit_bytes=VMEM_LIMIT),
        name="regroup_w_in",
    )(w_t)


def kernel(x, ln_mix_g, w_in, b_attn_qkv, rwkv_shift_mu, rwkv_w0, rwkv_w2, rwkv_a0, rwkv_a2, rwkv_g2, rwkv_k_k, rwkv_k_a, rwkv_r_k, rwkv_lnx_g, rwkv_lnx_b, attn_sinks, attn_out_g, w_out, ln_ffn_g, ffn_w_up, ffn_w_gate, ffn_conv_w, ffn_conv_b, ffn_w_down, ln_final_g):
    B, T, _ = x.shape
    assert B == 1 and ln_mix_g.shape[0] == 1
    l = 0
    x2d = x.reshape(T, D_MODEL)
    row = lambda a: a.reshape(1, -1)

    w_all = _regroup_w_in_t(jnp.swapaxes(w_in[l], 0, 1))
    mu_shift = _in_proj_cols(row(rwkv_shift_mu[l]))
    (a_t, k_h, b_h, r_t, k_e, b_e, v_b, bonus, gate, p_c, attn_in) = _in_proj(
        x2d, row(ln_mix_g[l]), w_all, mu_shift, row(b_attn_qkv[l]),
        row(rwkv_w0[l]), _pad_rows(rwkv_w2[l], LANES).astype(BF16),
        row(rwkv_a0[l]), _pad_rows(rwkv_a2[l], LANES).astype(BF16),
        _pad_rows(rwkv_g2[l], 2 * LANES).astype(BF16), row(rwkv_k_k[l]), row(rwkv_k_a[l]), row(rwkv_r_k[l]))

    y_rwkv = _rwkv(a_t, k_h, b_h, r_t, k_e, b_e, v_b, bonus, gate, p_c,
                   row(rwkv_lnx_g[l]), row(rwkv_lnx_b[l]))

    cos_l, sin_a, sin_b = _rope_lane_tables(T)
    y_attn = _attn(attn_in, attn_sinks[l], cos_l, sin_a, sin_b, row(attn_out_g[l]))

    x1 = _out_proj(x2d, y_rwkv, y_attn, w_out[l].astype(BF16))

    out = _ffn(x1, row(ln_ffn_g[l]), ffn_w_up[l].astype(BF16), ffn_w_gate[l].astype(BF16),
               ffn_conv_w[l], row(ffn_conv_b[l]), ffn_w_down[l].astype(BF16), row(ln_final_g))
    return out.reshape(B, T, D_MODEL)
```

```python
import functools
import math

import jax
import jax.numpy as jnp
from jax import lax
from jax.experimental import pallas as pl
from jax.experimental.pallas import tpu as pltpu

F32 = jnp.float32
BF16 = jnp.bfloat16

D_MODEL = 2048
HEAD_DIM = 64
RWKV_WIDTH = 1024
ATTN_WIDTH = 1024
ATTN_HEADS = 16
KV_HEADS = 2
KV_WIDTH = KV_HEADS * HEAD_DIM
W_LORA = 64
A_LORA = 64
G_LORA = 160
WINDOW = 128
ROPE_THETA = 500000.0
ROT_DIM = 16
D_FF = 5632
NORM_EPS = 1e-5
LNX_EPS = 64e-5

LANES = 128
SUBLANES = 8
PREV_ROWS = 16
LORA_PAD = 512
GROUP = 4
GROUP_W = GROUP * HEAD_DIM
CHUNK = 64
SUB_CHUNKS = 4
SUB_BLOCK = SUB_CHUNKS * CHUNK
VMEM_LIMIT = 56 * 1024 * 1024
IN_PROJ_VMEM_LIMIT = 60 * 1024 * 1024
FFN_VMEM_LIMIT = 60 * 1024 * 1024

assert CHUNK == HEAD_DIM and SUB_CHUNKS <= SUBLANES


def _dot(a, b):
    return lax.dot_general(a, b, (((1,), (0,)), ((), ())), preferred_element_type=F32)


def _dot_nt(a, b):
    return lax.dot_general(a, b, (((1,), (1,)), ((), ())), preferred_element_type=F32)


def _dot_tn(a, b):
    return lax.dot_general(a, b, (((0,), (0,)), ((), ())), preferred_element_type=F32)


def _sigmoid(x):
    return 1.0 / (1.0 + jnp.exp(-x))


def _sigmoid_tanh(x):
    return 0.5 + 0.5 * jnp.tanh(0.5 * x)


def _rms(xv, g):
    ms = jnp.mean(xv * xv, axis=-1, keepdims=True)
    return xv * lax.rsqrt(ms + NORM_EPS) * g


def _resident(shape):
    nd = len(shape)
    return pl.BlockSpec(shape, lambda *_: (0,) * nd, pipeline_mode=pl.Buffered(1))


def _split_bf16(x):
    hi = x.astype(BF16)
    lo = (x - hi.astype(F32)).astype(BF16)
    return hi, lo


def _same_head_mask():
    r_i = lax.broadcasted_iota(jnp.int32, (GROUP_W, GROUP_W), 0) // HEAD_DIM
    c_i = lax.broadcasted_iota(jnp.int32, (GROUP_W, GROUP_W), 1) // HEAD_DIM
    return r_i == c_i


def _head_sums(x, ones_bd):
    xb = x.astype(BF16)
    return jnp.concatenate([_dot(xb[:, q * GROUP_W:(q + 1) * GROUP_W], ones_bd)
                            for q in range(RWKV_WIDTH // GROUP_W)], axis=1)


def _split_refs(refs, *counts):
    out, pos = [], 0
    for n in counts:
        out.append(tuple(refs[pos:pos + n]))
        pos += n
    assert pos == len(refs)
    return out


def _side_cast_specs(arrays, n_steps):
    assert all(a.shape[0] % (n_steps * PREV_ROWS) == 0 for a in arrays)
    specs = [pl.BlockSpec((a.shape[0] // n_steps, a.shape[1]), lambda i: (i, 0)) for a in arrays]
    shapes = [jax.ShapeDtypeStruct(a.shape, BF16) for a in arrays]
    return specs, shapes


def _side_casts(src_refs, dst_refs):
    for src, dst in zip(src_refs, dst_refs):
        dst[...] = src[...].astype(dst.dtype)


def _in_proj_kernel(x_ref, xp_ref, g_ref, w_ref, mu_ref, battn_ref, w0_ref, w2_ref, a0_ref, a2_ref, g2_ref,
                    kk_ref, ka_ref, rk_ref, *rest, col_chunk, n_cast):
    (cast_in, (at_ref, kh_ref, bh_ref, rt_ref, ke_ref, be_ref, vb_ref, bonus_ref, gate_ref, pc_ref, attn_ref),
     cast_out, (hext_ref,)) = _split_refs(rest, n_cast, 11, n_cast, 1)
    i = pl.program_id(0)
    tm = x_ref.shape[0]
    n_chunks = tm // CHUNK
    C = CHUNK
    _side_casts(cast_in, cast_out)
    g = g_ref[...]
    hext_ref[PREV_ROWS:, :] = _rms(x_ref[...], g).astype(BF16)
    hp = jnp.where(i > 0, _rms(xp_ref[...], g), 0.0)
    hext_ref[:PREV_ROWS, :] = hp.astype(BF16)
    hx = hext_ref[...]

    def shifted(base, width):
        outs = []
        for c0 in range(base, base + width, col_chunk):
            z = _dot_nt(hx, w_ref[c0:c0 + col_chunk, :])
            zp = pltpu.roll(z, 1, axis=0)
            outs.append((z + mu_ref[:, c0:c0 + col_chunk] * (zp - z))[PREV_ROWS:, :])
        return jnp.concatenate(outs, axis=1)

    def attn_cols(c0):
        base = LORA_PAD + 3 * RWKV_WIDTH
        c1 = min(c0 + col_chunk, attn_ref.shape[1])
        attn_ref[:, c0:c1] = _dot_nt(hext_ref[PREV_ROWS:, :], w_ref[base + c0:base + c1, :]) + battn_ref[:, c0:c1]

    ones_bd = _same_head_mask().astype(BF16)
    t_i = lax.broadcasted_iota(jnp.int32, (tm, tm), 0)
    t_j = lax.broadcasted_iota(jnp.int32, (tm, tm), 1)
    tri = ((t_j <= t_i) & (t_j // C == t_i // C)).astype(BF16)
    attn_chunks = list(range(0, attn_ref.shape[1], col_chunk))

    lo = shifted(0, LORA_PAD)
    k = shifted(LORA_PAD, RWKV_WIDTH)
    xw = lo[:, 0:LANES]
    xa = lo[:, LANES:2 * LANES]
    xg = lo[:, 2 * LANES:LORA_PAD]
    u = w0_ref[...] + _dot(jnp.tanh(xw).astype(BF16), w2_ref[...])
    lw = -math.exp(-0.5) * _sigmoid_tanh(u)
    a = _sigmoid_tanh(a0_ref[...] + _dot(xa.astype(BF16), a2_ref[...]))
    gate_ref[...] = _dot(_sigmoid_tanh(xg).astype(BF16), g2_ref[...]).astype(gate_ref.dtype)
    lw_hi, lw_lo = _split_bf16(lw)
    cl = _dot(tri, lw_hi) + _dot(tri, lw_lo)
    last_rows = [cl[(c + 1) * C - 1:(c + 1) * C, :] for c in range(n_chunks)]
    cl_last = jnp.concatenate([jnp.broadcast_to(row, (C, RWKV_WIDTH)) for row in last_rows], axis=0)
    pc_ref[...] = jnp.concatenate([jnp.exp(row) for row in last_rows]
                                  + [jnp.zeros((SUBLANES - n_chunks, RWKV_WIDTH), F32)], axis=0)

    r = shifted(LORA_PAD + RWKV_WIDTH, RWKV_WIDTH)
    kkraw = k * kk_ref[...]
    k2 = k * (1.0 + (a - 1.0) * ka_ref[...])

    v = shifted(LORA_PAD + 2 * RWKV_WIDTH, RWKV_WIDTH)
    sums = _head_sums(jnp.concatenate([kkraw * kkraw, r * k2 * rk_ref[...]], axis=0), ones_bd)
    kk = kkraw / jnp.maximum(jnp.sqrt(sums[:tm]), 1e-12)
    b = kk * a

    attn_cols(attn_chunks[0])
    at_ref[...] = (-kk * jnp.exp(cl - lw)).astype(at_ref.dtype)
    e_neg = jnp.exp(-cl)
    kh_ref[...] = (k2 * e_neg).astype(kh_ref.dtype)
    bh_ref[...] = (b * e_neg).astype(bh_ref.dtype)

    for c0 in attn_chunks[1:2]:
        attn_cols(c0)
    rt_ref[...] = (r * jnp.exp(cl)).astype(rt_ref.dtype)
    e_end = jnp.exp(cl_last - cl)
    ke_ref[...] = (k2 * e_end).astype(ke_ref.dtype)
    be_ref[...] = (b * e_end).astype(be_ref.dtype)

    for c0 in attn_chunks[2:]:
        attn_cols(c0)
    bonus_ref[...] = (sums[tm:] * v).astype(bonus_ref.dtype)
    vb_ref[...] = v.astype(vb_ref.dtype)


def _in_proj(x2d, g, w_all, mu_shift, b_attn, w0, w2, a0, a2, g2, k_k, k_a, r_k, side_casts=(), *,
             tm=SUB_BLOCK, col_chunk=512):
    T = x2d.shape[0]
    n_attn = b_attn.shape[1]
    assert w_all.shape == (3 * RWKV_WIDTH + LORA_PAD + n_attn, D_MODEL) and tm % CHUNK == 0
    grid = (T // tm,)
    cast_specs, cast_shapes = _side_cast_specs(side_casts, T // tm)
    prev_blocks = tm // PREV_ROWS
    vec = _resident((1, RWKV_WIDTH))
    tile = pl.BlockSpec((tm, RWKV_WIDTH), lambda i: (i, 0))
    tile_shape = jax.ShapeDtypeStruct((T, RWKV_WIDTH), BF16)
    return pl.pallas_call(
        functools.partial(_in_proj_kernel, col_chunk=col_chunk, n_cast=len(side_casts)),
        grid=grid,
        in_specs=[
            pl.BlockSpec((tm, D_MODEL), lambda i: (i, 0)),
            pl.BlockSpec((PREV_ROWS, D_MODEL), lambda i: (jnp.maximum(i * prev_blocks - 1, 0), 0)),
            _resident((1, D_MODEL)),
            _resident(w_all.shape), _resident(mu_shift.shape), _resident(b_attn.shape),
            vec, _resident(w2.shape), vec, _resident(a2.shape), _resident(g2.shape), vec, vec, vec,
        ] + cast_specs,
        out_specs=[tile] * 9 + [
            pl.BlockSpec((None, SUBLANES, RWKV_WIDTH), lambda i: (i, 0, 0)),
            pl.BlockSpec((tm, n_attn), lambda i: (i, 0)),
        ] + cast_specs,
        out_shape=[tile_shape] * 9 + [
            jax.ShapeDtypeStruct((T // tm, SUBLANES, RWKV_WIDTH), F32),
            jax.ShapeDtypeStruct((T, n_attn), F32),
        ] + cast_shapes,
        scratch_shapes=[pltpu.VMEM((tm + PREV_ROWS, D_MODEL), BF16)],
        compiler_params=pltpu.CompilerParams(dimension_semantics=("arbitrary",),
                                             vmem_limit_bytes=IN_PROJ_VMEM_LIMIT),
        name="in_proj",
    )(x2d, x2d, g, w_all, mu_shift, b_attn, w0, w2, a0, a2, g2, k_k, k_a, r_k, *side_casts)


def _block_diag(x):
    blk = lax.broadcasted_iota(jnp.int32, x.shape, 1) // HEAD_DIM
    zero = jnp.zeros_like(x)
    return jnp.concatenate([jnp.where(blk == h, x, zero) for h in range(GROUP)], axis=0)


def _rwkv_kernel(at_ref, kh_ref, bh_ref, rt_ref, ke_ref, be_ref, vb_ref, bonus_ref, gate_ref, pc_ref,
                 lng_ref, lnb_ref, *rest, n_sub, n_cast):
    cast_in, (y_ref,), cast_out, (s_ref,) = _split_refs(rest, n_cast, 1, n_cast, 1)
    C = CHUNK
    SB = SUB_BLOCK
    n_groups = RWKV_WIDTH // GROUP_W

    @pl.when(pl.program_id(0) == 0)
    def _():
        s_ref[...] = jnp.zeros_like(s_ref)

    _side_casts(cast_in, cast_out)
    row = lax.broadcasted_iota(jnp.int32, (C, GROUP_W), 0)
    col = lax.broadcasted_iota(jnp.int32, (C, GROUP_W), 1) % C
    strict = col < row
    incl = col <= row
    eye = (col == row).astype(F32)
    lane_head = lax.broadcasted_iota(jnp.int32, (HEAD_DIM, GROUP_W), 1) // HEAD_DIM
    ones_bd = _same_head_mask().astype(BF16)
    streams = [(c, q) for c in range(SUB_CHUNKS) for q in range(n_groups)]
    state = [s_ref[q] for q in range(n_groups)]
    subs = [dict() for _ in range(n_sub)]

    def blk(ref, h, s):
        c, q = s
        return ref[h * SB + c * C:h * SB + (c + 1) * C, q * GROUP_W:(q + 1) * GROUP_W]

    def local_pieces(h):
        H = subs[h]
        a_ak, a_rk, l_ab, pw = {}, {}, {}, {}
        H["a_rb"], H["t_inv"], H["av"] = {}, {}, {}
        a_rb, t_inv, av = H["a_rb"], H["t_inv"], H["av"]

        def scores():
            for s in streams:
                ar = jnp.concatenate([blk(at_ref, h, s), blk(rt_ref, h, s)], axis=0)
                ak = _dot_nt(ar, _block_diag(blk(kh_ref, h, s)))
                ab = _dot_nt(ar, _block_diag(blk(bh_ref, h, s)))
                a_ak[s] = jnp.where(strict, ak[:C], 0.0)
                a_rk[s] = jnp.where(incl, ak[C:], 0.0)
                l_ab[s] = jnp.where(strict, ab[:C], 0.0)
                a_rb[s] = jnp.where(incl, ab[C:], 0.0).astype(BF16)

        def inv_first():
            for s in streams:
                t_inv[s] = eye + l_ab[s]
                lb = l_ab[s].astype(BF16)
                pw[s] = _dot(lb, _block_diag(lb))

        def inv_level():
            for s in streams:
                tp = _dot(jnp.concatenate([t_inv[s], pw[s]], axis=0).astype(BF16),
                          _block_diag(pw[s].astype(BF16)))
                t_inv[s] = t_inv[s] + tp[:C]
                pw[s] = tp[C:]

        def inv_last():
            for s in streams:
                t_inv[s] = (t_inv[s] + _dot(t_inv[s].astype(BF16), _block_diag(pw[s].astype(BF16)))).astype(BF16)

        def values():
            for s in streams:
                av[s] = _dot(jnp.concatenate([a_ak[s], a_rk[s]], axis=0).astype(BF16),
                             _block_diag(blk(vb_ref, h, s)))

        return [scores, inv_first] + [inv_level] * (int(math.log2(C)) - 2) + [inv_last, values]

    def chain_pieces(h):
        H = subs[h]
        xs, u_b, ys = {}, {}, {}

        def read_state(c):
            def f():
                for q in range(n_groups):
                    s = (c, q)
                    ar = jnp.concatenate([blk(at_ref, h, s), blk(rt_ref, h, s)], axis=0)
                    xs[s] = _dot_nt(ar, _block_diag(state[q].astype(BF16)))
            return f

        def solve(c):
            def f():
                for q in range(n_groups):
                    s = (c, q)
                    x_u = (xs[s][:C] + H["av"][s][:C]).astype(BF16)
                    u_b[s] = _dot(H["t_inv"][s], _block_diag(x_u)).astype(BF16)
            return f

        def update(c):
            def f():
                for q in range(n_groups):
                    s = (c, q)
                    ys[s] = xs[s][C:] + H["av"][s][C:] + _dot(H["a_rb"][s], _block_diag(u_b[s]))
                    vu = jnp.concatenate([blk(vb_ref, h, s), u_b[s]], axis=0)
                    kb = jnp.concatenate([blk(ke_ref, h, s), blk(be_ref, h, s)], axis=0)
                    upd = _dot_tn(vu, kb)
                    upd_c = sum(jnp.where(lane_head == hd, upd[hd * HEAD_DIM:(hd + 1) * HEAD_DIM, :], 0.0)
                                for hd in range(GROUP))
                    p_c = pc_ref[h, c:c + 1, q * GROUP_W:(q + 1) * GROUP_W]
                    state[q] = state[q] * p_c + upd_c
                if c == SUB_CHUNKS - 1:
                    H["y"] = jnp.concatenate([jnp.concatenate([ys[(cc, q)] for q in range(n_groups)], axis=1)
                                              for cc in range(SUB_CHUNKS)], axis=0)
            return f

        return [p for c in range(SUB_CHUNKS) for p in (read_state(c), solve(c), update(c))]

    def epilogue_pieces(h):
        H = subs[h]
        rows = slice(h * SB, (h + 1) * SB)

        def center():
            mu = _head_sums(H["y"], ones_bd) * (1.0 / HEAD_DIM)
            H["d"] = H["y"] - mu

        def scale():
            d = H["d"]
            var = _head_sums(d * d, ones_bd) * (1.0 / HEAD_DIM)
            yn = d * lax.rsqrt(var + LNX_EPS) * lng_ref[...] + lnb_ref[...]
            y_ref[rows, :] = ((yn + bonus_ref[rows, :].astype(F32)) * gate_ref[rows, :].astype(F32)).astype(y_ref.dtype)

        return [center, scale]

    def run_merged(*lists):
        lists = [pieces for pieces in lists if pieces]
        if not lists:
            return
        main = max(lists, key=len)
        others = [pieces for pieces in lists if pieces is not main]
        n = len(main)
        for i, piece in enumerate(main):
            piece()
            for other in others:
                for j, o in enumerate(other):
                    if j * n // len(other) == i:
                        o()

    for h in range(n_sub + 2):
        run_merged(chain_pieces(h - 1) if 1 <= h <= n_sub else [],
                   local_pieces(h) if h < n_sub else [],
                   epilogue_pieces(h - 2) if h >= 2 else [])
    for q in range(n_groups):
        s_ref[q] = state[q]


def _rwkv(a_t, k_h, b_h, r_t, k_e, b_e, v_b, bonus, gate, p_c, ln_g, ln_b, side_casts=(), *, tb=2 * SUB_BLOCK):
    T = a_t.shape[0]
    n_sub = tb // SUB_BLOCK
    assert p_c.shape == (T // SUB_BLOCK, SUBLANES, RWKV_WIDTH)
    vec = _resident((1, RWKV_WIDTH))
    tile = pl.BlockSpec((tb, RWKV_WIDTH), lambda i: (i, 0))
    cast_specs, cast_shapes = _side_cast_specs(side_casts, T // tb)
    return pl.pallas_call(
        functools.partial(_rwkv_kernel, n_sub=n_sub, n_cast=len(side_casts)),
        grid=(T // tb,),
        in_specs=[tile] * 9 + [pl.BlockSpec((n_sub, SUBLANES, RWKV_WIDTH), lambda i: (i, 0, 0)), vec, vec]
        + cast_specs,
        out_specs=[tile] + cast_specs,
        out_shape=[jax.ShapeDtypeStruct((T, RWKV_WIDTH), BF16)] + cast_shapes,
        scratch_shapes=[pltpu.VMEM((RWKV_WIDTH // GROUP_W, HEAD_DIM, GROUP_W), F32)],
        compiler_params=pltpu.CompilerParams(dimension_semantics=("arbitrary",),
                                             vmem_limit_bytes=VMEM_LIMIT),
        name="rwkv7",
    )(a_t, k_h, b_h, r_t, k_e, b_e, v_b, bonus, gate, p_c, ln_g, ln_b, *side_casts)


def _rope(x, cos_l, sin_a, sin_b):
    return x * cos_l + pltpu.roll(x, LANES - ROT_DIM // 2, axis=1) * sin_a + pltpu.roll(x, ROT_DIM // 2, axis=1) * sin_b


def _attn_kernel(sink_ref, q_ref, kv_ref, kvp_ref, cos_ref, sa_ref, sb_ref, cosp_ref, sap_ref, sbp_ref,
                 g_ref, o_ref):
    n = pl.program_id(0)
    B = WINDOW
    cos_l, sin_a, sin_b = cos_ref[...], sa_ref[...], sb_ref[...]
    kv = kv_ref[...]
    kvp = kvp_ref[...]
    k_cur = _rope(kv[:, :LANES], cos_l, sin_a, sin_b)
    k_prev = _rope(kvp[:, :LANES], cosp_ref[...], sap_ref[...], sbp_ref[...])
    k_win = jnp.concatenate([k_prev, k_cur], axis=0)
    v_win = jnp.concatenate([kvp[:, LANES:], kv[:, LANES:]], axis=0)
    lane = lax.broadcasted_iota(jnp.int32, (2 * B, LANES), 1)
    low = lane < HEAD_DIM
    k_sw = pltpu.roll(k_win, HEAD_DIM, axis=1)
    v_sw = pltpu.roll(v_win, HEAD_DIM, axis=1)
    k_dup = [jnp.where(low, k_win, k_sw).astype(BF16), jnp.where(low, k_sw, k_win).astype(BF16)]
    v_dup = [jnp.where(low, v_win, v_sw).astype(BF16), jnp.where(low, v_sw, v_win).astype(BF16)]

    qi = lax.broadcasted_iota(jnp.int32, (2 * B, 2 * B), 0) % B
    si = lax.broadcasted_iota(jnp.int32, (2 * B, 2 * B), 1)
    allowed = (si > qi) & (si <= qi + B) & ((si >= B) | (n > 0))
    top = lax.broadcasted_iota(jnp.int32, (2 * B, 1), 0) < B
    qlow = lax.broadcasted_iota(jnp.int32, (B, LANES), 1) < HEAD_DIM
    pairs_per_kv = (ATTN_HEADS // KV_HEADS) // 2
    pairs = range(ATTN_HEADS // 2)
    scores, es, denoms, outs = [], [], [], []
    for p in pairs:
        q = _rope(q_ref[:, p * LANES:(p + 1) * LANES], cos_l, sin_a, sin_b) * (HEAD_DIM ** -0.5)
        zero = jnp.zeros_like(q)
        q_st = jnp.concatenate([jnp.where(qlow, q, zero), jnp.where(qlow, zero, q)], axis=0).astype(BF16)
        scores.append(_dot_nt(q_st, k_dup[p // pairs_per_kv]))
    for p in pairs:
        s = jnp.where(allowed, scores[p], -1e30)
        sink = jnp.where(top, sink_ref[2 * p], sink_ref[2 * p + 1])
        m = jnp.maximum(jnp.max(s, axis=-1, keepdims=True), sink)
        e = jnp.exp(s - m)
        denoms.append(jnp.sum(e, axis=-1, keepdims=True) + jnp.exp(sink - m))
        es.append(e.astype(BF16))
    for p in pairs:
        o_st = _dot(es[p], v_dup[p // pairs_per_kv]) / denoms[p]
        outs.append(jnp.where(qlow, o_st[:B], o_st[B:]))
    o = jnp.concatenate(outs, axis=1)
    o_ref[...] = _rms(o, g_ref[...]).astype(o_ref.dtype)


def _attn(attn_in, sinks, cos_l, sin_a, sin_b, g):
    T = attn_in.shape[0]
    B = WINDOW
    grid = (T // B,)
    kv_blk = ATTN_WIDTH // (2 * KV_WIDTH)
    cur = lambda n: (n, 0)
    prev = lambda n: (jnp.maximum(n - 1, 0), 0)
    tab = lambda im: pl.BlockSpec((B, LANES), im)
    return pl.pallas_call(
        _attn_kernel,
        grid=grid,
        in_specs=[
            pl.BlockSpec(memory_space=pltpu.SMEM),
            pl.BlockSpec((B, ATTN_WIDTH), cur),
            pl.BlockSpec((B, 2 * KV_WIDTH), lambda n: (n, kv_blk)),
            pl.BlockSpec((B, 2 * KV_WIDTH), lambda n: (jnp.maximum(n - 1, 0), kv_blk)),
            tab(cur), tab(cur), tab(cur), tab(prev), tab(prev), tab(prev),
            _resident((1, ATTN_WIDTH)),
        ],
        out_specs=pl.BlockSpec((B, ATTN_WIDTH), cur),
        out_shape=jax.ShapeDtypeStruct((T, ATTN_WIDTH), BF16),
        compiler_params=pltpu.CompilerParams(dimension_semantics=("arbitrary",),
                                             vmem_limit_bytes=VMEM_LIMIT),
        name="swa_attn",
    )(sinks, attn_in, attn_in, attn_in, cos_l, sin_a, sin_b, cos_l, sin_a, sin_b, g)


def _out_proj_kernel(x_ref, yr_ref, ya_ref, w_ref, o_ref):
    o_ref[...] = (x_ref[...] + _dot(yr_ref[...], w_ref[:RWKV_WIDTH, :])
                  + _dot(ya_ref[...], w_ref[RWKV_WIDTH:, :]))


def _out_proj(x2d, y_rwkv, y_attn, w, *, tm=512):
    T = x2d.shape[0]
    row = lambda width: pl.BlockSpec((tm, width), lambda i: (i, 0))
    return pl.pallas_call(
        _out_proj_kernel,
        grid=(T // tm,),
        in_specs=[row(D_MODEL), row(RWKV_WIDTH), row(ATTN_WIDTH), _resident(w.shape)],
        out_specs=row(D_MODEL),
        out_shape=jax.ShapeDtypeStruct((T, D_MODEL), F32),
        compiler_params=pltpu.CompilerParams(dimension_semantics=("arbitrary",),
                                             vmem_limit_bytes=VMEM_LIMIT),
        name="out_proj",
    )(x2d, y_rwkv, y_attn, w)


def _ffn_kernel(x_ref, xp_ref, g_ref, wup_ref, wgate_ref, cw_ref, cb_ref, wdown_ref, gf_ref, o_ref, hext_ref):
    i = pl.program_id(0)
    f = pl.program_id(1)

    @pl.when(f == 0)
    def _():
        g = g_ref[...]
        x = x_ref[...]
        hext_ref[PREV_ROWS:, :] = _rms(x, g).astype(BF16)
        hp = jnp.where(i > 0, _rms(xp_ref[...], g), 0.0)
        hext_ref[:PREV_ROWS, :] = hp.astype(BF16)
        o_ref[...] = x

    hx = hext_ref[...]
    u = _dot(hx, wup_ref[...])
    cw = cw_ref[...]
    conv = cb_ref[...] + cw[2:3, :] * u + cw[1:2, :] * pltpu.roll(u, 1, axis=0) + cw[0:1, :] * pltpu.roll(u, 2, axis=0)
    conv = conv[PREV_ROWS:, :]
    gate = _dot(hext_ref[PREV_ROWS:, :], wgate_ref[...])
    act = conv * _sigmoid(conv) * gate
    o_ref[...] += _dot(act.astype(BF16), wdown_ref[...])

    @pl.when(f == pl.num_programs(1) - 1)
    def _():
        o_ref[...] = _rms(o_ref[...], gf_ref[...])


def _ffn(x1, g, w_up, w_gate, conv_w, conv_b, w_down, g_final, *, tm=1024, tf=512):
    T = x1.shape[0]
    prev_blocks = tm // PREV_ROWS
    return pl.pallas_call(
        _ffn_kernel,
        grid=(T // tm, D_FF // tf),
        in_specs=[
            pl.BlockSpec((tm, D_MODEL), lambda i, f: (i, 0)),
            pl.BlockSpec((PREV_ROWS, D_MODEL), lambda i, f: (jnp.maximum(i * prev_blocks - 1, 0), 0)),
            pl.BlockSpec((1, D_MODEL), lambda i, f: (0, 0)),
            pl.BlockSpec((D_MODEL, tf), lambda i, f: (0, f)),
            pl.BlockSpec((D_MODEL, tf), lambda i, f: (0, f)),
            pl.BlockSpec((3, tf), lambda i, f: (0, f)),
            pl.BlockSpec((1, tf), lambda i, f: (0, f)),
            pl.BlockSpec((tf, D_MODEL), lambda i, f: (f, 0)),
            pl.BlockSpec((1, D_MODEL), lambda i, f: (0, 0)),
        ],
        out_specs=pl.BlockSpec((tm, D_MODEL), lambda i, f: (i, 0)),
        out_shape=jax.ShapeDtypeStruct((T, D_MODEL), F32),
        scratch_shapes=[pltpu.VMEM((tm + PREV_ROWS, D_MODEL), BF16)],
        compiler_params=pltpu.CompilerParams(dimension_semantics=("arbitrary", "arbitrary"),
                                             vmem_limit_bytes=FFN_VMEM_LIMIT),
        name="ffn",
    )(x1, x1, g, w_up, w_gate, conv_w, conv_b, w_down, g_final)


def _pad_cols(a, width):
    return jnp.pad(a, ((0, 0), (0, width - a.shape[1])))


def _pad_rows(a, rows):
    return jnp.pad(a, ((0, rows - a.shape[0]), (0, 0)))


def _rope_lane_tables(T):
    half = ROT_DIM // 2
    inv_freq = ROPE_THETA ** (-jnp.arange(0, ROT_DIM, 2, dtype=F32) / ROT_DIM)
    ang = jnp.arange(T, dtype=F32)[:, None] * inv_freq[None, :]
    cos, sin = jnp.cos(ang), jnp.sin(ang)
    ones = jnp.ones((T, HEAD_DIM - ROT_DIM), F32)
    zeros_h = jnp.zeros((T, half), F32)
    zeros_r = jnp.zeros((T, HEAD_DIM - ROT_DIM), F32)
    cos_h = jnp.concatenate([cos, cos, ones], axis=1)
    sa_h = jnp.concatenate([-sin, zeros_h, zeros_r], axis=1)
    sb_h = jnp.concatenate([zeros_h, sin, zeros_r], axis=1)
    rep = LANES // HEAD_DIM
    return jnp.tile(cos_h, (1, rep)), jnp.tile(sa_h, (1, rep)), jnp.tile(sb_h, (1, rep))


_IN_PROJ_SEGMENTS = (
    (3 * RWKV_WIDTH, W_LORA, LANES),
    (3 * RWKV_WIDTH + W_LORA, A_LORA, LANES),
    (3 * RWKV_WIDTH + W_LORA + A_LORA, G_LORA, 2 * LANES),
    (RWKV_WIDTH, RWKV_WIDTH, RWKV_WIDTH),
    (0, RWKV_WIDTH, RWKV_WIDTH),
    (2 * RWKV_WIDTH, RWKV_WIDTH, RWKV_WIDTH),
    (3 * RWKV_WIDTH + W_LORA + A_LORA + G_LORA, ATTN_WIDTH + 2 * KV_WIDTH, ATTN_WIDTH + 2 * KV_WIDTH),
)


def _in_proj_cols(a):
    return jnp.concatenate([_pad_cols(a[:, src:src + width], slot) for src, width, slot in _IN_PROJ_SEGMENTS],
                           axis=1)


def _regroup_kernel(wt_ref, o_ref):
    dst = 0
    for src, width, slot in _IN_PROJ_SEGMENTS:
        o_ref[dst:dst + width, :] = wt_ref[src:src + width, :].astype(o_ref.dtype)
        if slot > width:
            o_ref[dst + width:dst + slot, :] = jnp.zeros((slot - width, o_ref.shape[1]), o_ref.dtype)
        dst += slot


def _regroup_w_in_t(w_t, *, tc=256):
    n_out, n_in = w_t.shape
    out_rows = sum(slot for _, _, slot in _IN_PROJ_SEGMENTS)
    return pl.pallas_call(
        _regroup_kernel,
        grid=(n_in // tc,),
        in_specs=[pl.BlockSpec((n_out, tc), lambda i: (0, i))],
        out_specs=pl.BlockSpec((out_rows, tc), lambda i: (0, i)),
        out_shape=jax.ShapeDtypeStruct((out_rows, n_in), BF16),
        compiler_params=pltpu.CompilerParams(dimension_semantics=("arbitrary",), vmem_limit_bytes=VMEM_LIMIT),
        name="regroup_w_in",
    )(w_t)


def kernel(x, ln_mix_g, w_in, b_attn_qkv, rwkv_shift_mu, rwkv_w0, rwkv_w2, rwkv_a0, rwkv_a2, rwkv_g2, rwkv_k_k, rwkv_k_a, rwkv_r_k, rwkv_lnx_g, rwkv_lnx_b, attn_sinks, attn_out_g, w_out, ln_ffn_g, ffn_w_up, ffn_w_gate, ffn_conv_w, ffn_conv_b, ffn_w_down, ln_final_g):
    B, T, _ = x.shape
    assert B == 1 and ln_mix_g.shape[0] == 1
    l = 0
    x2d = x.reshape(T, D_MODEL)
    row = lambda a: a.reshape(1, -1)

    w_all = _regroup_w_in_t(jnp.swapaxes(w_in[l], 0, 1))
    mu_shift = _in_proj_cols(row(rwkv_shift_mu[l]))
    (a_t, k_h, b_h, r_t, k_e, b_e, v_b, bonus, gate, p_c, attn_in, w_up, w_gate) = _in_proj(
        x2d, row(ln_mix_g[l]), w_all, mu_shift, row(b_attn_qkv[l]),
        row(rwkv_w0[l]), _pad_rows(rwkv_w2[l], LANES).astype(BF16),
        row(rwkv_a0[l]), _pad_rows(rwkv_a2[l], LANES).astype(BF16),
        _pad_rows(rwkv_g2[l], 2 * LANES).astype(BF16), row(rwkv_k_k[l]), row(rwkv_k_a[l]), row(rwkv_r_k[l]),
        side_casts=(ffn_w_up[l], ffn_w_gate[l]))

    y_rwkv, w_down, w_o = _rwkv(a_t, k_h, b_h, r_t, k_e, b_e, v_b, bonus, gate, p_c,
                                row(rwkv_lnx_g[l]), row(rwkv_lnx_b[l]), side_casts=(ffn_w_down[l], w_out[l]))

    cos_l, sin_a, sin_b = _rope_lane_tables(T)
    y_attn = _attn(attn_in, attn_sinks[l], cos_l, sin_a, sin_b, row(attn_out_g[l]))

    x1 = _out_proj(x2d, y_rwkv, y_attn, w_o)

    out = _ffn(x1, row(ln_ffn_g[l]), w_up, w_gate, ffn_conv_w[l], row(ffn_conv_b[l]), w_down, row(ln_final_g))
    return out.reshape(B, T, D_MODEL)
```

```python
import functools
import math

import jax
import jax.numpy as jnp
from jax import lax
from jax.experimental import pallas as pl
from jax.experimental.pallas import tpu as pltpu

F32 = jnp.float32
BF16 = jnp.bfloat16

D_MODEL = 2048
HEAD_DIM = 64
RWKV_WIDTH = 1024
ATTN_WIDTH = 1024
ATTN_HEADS = 16
KV_HEADS = 2
KV_WIDTH = KV_HEADS * HEAD_DIM
W_LORA = 64
A_LORA = 64
G_LORA = 160
WINDOW = 128
ROPE_THETA = 500000.0
ROT_DIM = 16
D_FF = 5632
NORM_EPS = 1e-5
LNX_EPS = 64e-5

LANES = 128
SUBLANES = 8
PREV_ROWS = 16
LORA_PAD = 512
GROUP = 4
GROUP_W = GROUP * HEAD_DIM
CHUNK = 64
SUB_CHUNKS = 4
SUB_BLOCK = SUB_CHUNKS * CHUNK
VMEM_LIMIT = 56 * 1024 * 1024
IN_PROJ_VMEM_LIMIT = 62 * 1024 * 1024
FFN_VMEM_LIMIT = 60 * 1024 * 1024

assert CHUNK == HEAD_DIM and SUB_CHUNKS <= SUBLANES


def _dot(a, b):
    return lax.dot_general(a, b, (((1,), (0,)), ((), ())), preferred_element_type=F32)


def _dot_nt(a, b):
    return lax.dot_general(a, b, (((1,), (1,)), ((), ())), preferred_element_type=F32)


def _dot_tn(a, b):
    return lax.dot_general(a, b, (((0,), (0,)), ((), ())), preferred_element_type=F32)


def _sigmoid(x):
    return 1.0 / (1.0 + jnp.exp(-x))


def _sigmoid_tanh(x):
    return 0.5 + 0.5 * jnp.tanh(0.5 * x)


def _rms(xv, g):
    ms = jnp.mean(xv * xv, axis=-1, keepdims=True)
    return xv * lax.rsqrt(ms + NORM_EPS) * g


def _resident(shape):
    nd = len(shape)
    return pl.BlockSpec(shape, lambda *_: (0,) * nd, pipeline_mode=pl.Buffered(1))


def _split_bf16(x):
    hi = x.astype(BF16)
    lo = (x - hi.astype(F32)).astype(BF16)
    return hi, lo


def _same_head_mask():
    r_i = lax.broadcasted_iota(jnp.int32, (GROUP_W, GROUP_W), 0) // HEAD_DIM
    c_i = lax.broadcasted_iota(jnp.int32, (GROUP_W, GROUP_W), 1) // HEAD_DIM
    return r_i == c_i


def _head_sums(x, ones_bd):
    xb = x.astype(BF16)
    return jnp.concatenate([_dot(xb[:, q * GROUP_W:(q + 1) * GROUP_W], ones_bd)
                            for q in range(RWKV_WIDTH // GROUP_W)], axis=1)


def _split_refs(refs, *counts):
    out, pos = [], 0
    for n in counts:
        out.append(tuple(refs[pos:pos + n]))
        pos += n
    assert pos == len(refs)
    return out


def _side_cast_specs(arrays, n_steps):
    assert all(a.shape[0] % (n_steps * PREV_ROWS) == 0 for a in arrays)
    specs = [pl.BlockSpec((a.shape[0] // n_steps, a.shape[1]), lambda i: (i, 0)) for a in arrays]
    shapes = [jax.ShapeDtypeStruct(a.shape, BF16) for a in arrays]
    return specs, shapes


def _side_casts(src_refs, dst_refs):
    for src, dst in zip(src_refs, dst_refs):
        dst[...] = src[...].astype(dst.dtype)


def _in_proj_kernel(x_ref, xp_ref, g_ref, w_ref, mu_ref, battn_ref, w0_ref, w2_ref, a0_ref, a2_ref, g2_ref,
                    kk_ref, ka_ref, rk_ref, *rest, col_chunk, n_cast):
    (cast_in, (at_ref, kh_ref, bh_ref, rt_ref, ke_ref, be_ref, vb_ref, bonus_ref, gate_ref, pc_ref, attn_ref),
     cast_out, (hext_ref,)) = _split_refs(rest, n_cast, 11, n_cast, 1)
    i = pl.program_id(0)
    tm = x_ref.shape[0]
    n_chunks = tm // CHUNK
    C = CHUNK
    _side_casts(cast_in, cast_out)
    g = g_ref[...]
    hext_ref[PREV_ROWS:, :] = _rms(x_ref[...], g).astype(BF16)
    hp = jnp.where(i > 0, _rms(xp_ref[...], g), 0.0)
    hext_ref[:PREV_ROWS, :] = hp.astype(BF16)
    hx = hext_ref[...]

    def shifted(base, width):
        outs = []
        for c0 in range(base, base + width, col_chunk):
            z = _dot_nt(hx, w_ref[c0:c0 + col_chunk, :])
            zp = pltpu.roll(z, 1, axis=0)
            outs.append((z + mu_ref[:, c0:c0 + col_chunk] * (zp - z))[PREV_ROWS:, :])
        return jnp.concatenate(outs, axis=1)

    def attn_cols(c0):
        base = LORA_PAD + 3 * RWKV_WIDTH
        c1 = min(c0 + col_chunk, attn_ref.shape[1])
        attn_ref[:, c0:c1] = _dot_nt(hext_ref[PREV_ROWS:, :], w_ref[base + c0:base + c1, :]) + battn_ref[:, c0:c1]

    ones_bd = _same_head_mask().astype(BF16)
    t_i = lax.broadcasted_iota(jnp.int32, (tm, tm), 0)
    t_j = lax.broadcasted_iota(jnp.int32, (tm, tm), 1)
    tri = ((t_j <= t_i) & (t_j // C == t_i // C)).astype(BF16)
    attn_chunks = list(range(0, attn_ref.shape[1], col_chunk))

    lo = shifted(0, LORA_PAD)
    k = shifted(LORA_PAD, RWKV_WIDTH)
    xw = lo[:, 0:LANES]
    xa = lo[:, LANES:2 * LANES]
    xg = lo[:, 2 * LANES:LORA_PAD]
    u = w0_ref[...] + _dot(jnp.tanh(xw).astype(BF16), w2_ref[...])
    lw = -math.exp(-0.5) * _sigmoid_tanh(u)
    a = _sigmoid_tanh(a0_ref[...] + _dot(xa.astype(BF16), a2_ref[...]))
    gate_ref[...] = _dot(_sigmoid_tanh(xg).astype(BF16), g2_ref[...]).astype(gate_ref.dtype)
    lw_hi, lw_lo = _split_bf16(lw)
    cl = _dot(tri, lw_hi) + _dot(tri, lw_lo)
    last_rows = [cl[(c + 1) * C - 1:(c + 1) * C, :] for c in range(n_chunks)]
    cl_last = jnp.concatenate([jnp.broadcast_to(row, (C, RWKV_WIDTH)) for row in last_rows], axis=0)
    pc_ref[...] = jnp.concatenate([jnp.exp(row) for row in last_rows]
                                  + [jnp.zeros((SUBLANES - n_chunks, RWKV_WIDTH), F32)], axis=0)

    r = shifted(LORA_PAD + RWKV_WIDTH, RWKV_WIDTH)
    kkraw = k * kk_ref[...]
    k2 = k * (1.0 + (a - 1.0) * ka_ref[...])

    v = shifted(LORA_PAD + 2 * RWKV_WIDTH, RWKV_WIDTH)
    sums = _head_sums(jnp.concatenate([kkraw * kkraw, r * k2 * rk_ref[...]], axis=0), ones_bd)
    kk = kkraw / jnp.maximum(jnp.sqrt(sums[:tm]), 1e-12)
    b = kk * a

    attn_cols(attn_chunks[0])
    at_ref[...] = (-kk * jnp.exp(cl - lw)).astype(at_ref.dtype)
    e_neg = jnp.exp(-cl)
    kh_ref[...] = (k2 * e_neg).astype(kh_ref.dtype)
    bh_ref[...] = (b * e_neg).astype(bh_ref.dtype)

    for c0 in attn_chunks[1:2]:
        attn_cols(c0)
    rt_ref[...] = (r * jnp.exp(cl)).astype(rt_ref.dtype)
    e_end = jnp.exp(cl_last - cl)
    ke_ref[...] = (k2 * e_end).astype(ke_ref.dtype)
    be_ref[...] = (b * e_end).astype(be_ref.dtype)

    for c0 in attn_chunks[2:]:
        attn_cols(c0)
    bonus_ref[...] = (sums[tm:] * v).astype(bonus_ref.dtype)
    vb_ref[...] = v.astype(vb_ref.dtype)


def _in_proj(x2d, g, w_all, mu_shift, b_attn, w0, w2, a0, a2, g2, k_k, k_a, r_k, side_casts=(), *,
             tm=SUB_BLOCK, col_chunk=512):
    T = x2d.shape[0]
    n_attn = b_attn.shape[1]
    assert w_all.shape == (3 * RWKV_WIDTH + LORA_PAD + n_attn, D_MODEL) and tm % CHUNK == 0
    grid = (T // tm,)
    cast_specs, cast_shapes = _side_cast_specs(side_casts, T // tm)
    prev_blocks = tm // PREV_ROWS
    vec = _resident((1, RWKV_WIDTH))
    tile = pl.BlockSpec((tm, RWKV_WIDTH), lambda i: (i, 0))
    tile_shape = jax.ShapeDtypeStruct((T, RWKV_WIDTH), BF16)
    return pl.pallas_call(
        functools.partial(_in_proj_kernel, col_chunk=col_chunk, n_cast=len(side_casts)),
        grid=grid,
        in_specs=[
            pl.BlockSpec((tm, D_MODEL), lambda i: (i, 0)),
            pl.BlockSpec((PREV_ROWS, D_MODEL), lambda i: (jnp.maximum(i * prev_blocks - 1, 0), 0)),
            _resident((1, D_MODEL)),
            _resident(w_all.shape), _resident(mu_shift.shape), _resident(b_attn.shape),
            vec, _resident(w2.shape), vec, _resident(a2.shape), _resident(g2.shape), vec, vec, vec,
        ] + cast_specs,
        out_specs=[tile] * 9 + [
            pl.BlockSpec((None, SUBLANES, RWKV_WIDTH), lambda i: (i, 0, 0)),
            pl.BlockSpec((tm, n_attn), lambda i: (i, 0)),
        ] + cast_specs,
        out_shape=[tile_shape] * 9 + [
            jax.ShapeDtypeStruct((T // tm, SUBLANES, RWKV_WIDTH), F32),
            jax.ShapeDtypeStruct((T, n_attn), F32),
        ] + cast_shapes,
        scratch_shapes=[pltpu.VMEM((tm + PREV_ROWS, D_MODEL), BF16)],
        compiler_params=pltpu.CompilerParams(dimension_semantics=("arbitrary",),
                                             vmem_limit_bytes=IN_PROJ_VMEM_LIMIT),
        name="in_proj",
    )(x2d, x2d, g, w_all, mu_shift, b_attn, w0, w2, a0, a2, g2, k_k, k_a, r_k, *side_casts)


def _block_diag(x):
    blk = lax.broadcasted_iota(jnp.int32, x.shape, 1) // HEAD_DIM
    zero = jnp.zeros_like(x)
    return jnp.concatenate([jnp.where(blk == h, x, zero) for h in range(GROUP)], axis=0)


def _rwkv_kernel(at_ref, kh_ref, bh_ref, rt_ref, ke_ref, be_ref, vb_ref, bonus_ref, gate_ref, pc_ref,
                 lng_ref, lnb_ref, *rest, n_sub, n_cast):
    cast_in, (y_ref,), cast_out, (s_ref,) = _split_refs(rest, n_cast, 1, n_cast, 1)
    C = CHUNK
    SB = SUB_BLOCK
    n_groups = RWKV_WIDTH // GROUP_W

    @pl.when(pl.program_id(0) == 0)
    def _():
        s_ref[...] = jnp.zeros_like(s_ref)

    _side_casts(cast_in, cast_out)
    row = lax.broadcasted_iota(jnp.int32, (C, GROUP_W), 0)
    col = lax.broadcasted_iota(jnp.int32, (C, GROUP_W), 1) % C
    strict = col < row
    incl = col <= row
    eye = (col == row).astype(F32)
    lane_head = lax.broadcasted_iota(jnp.int32, (HEAD_DIM, GROUP_W), 1) // HEAD_DIM
    ones_bd = _same_head_mask().astype(BF16)
    streams = [(c, q) for c in range(SUB_CHUNKS) for q in range(n_groups)]
    state = [s_ref[q] for q in range(n_groups)]
    subs = [dict() for _ in range(n_sub)]

    def blk(ref, h, s):
        c, q = s
        return ref[h * SB + c * C:h * SB + (c + 1) * C, q * GROUP_W:(q + 1) * GROUP_W]

    def local_pieces(h):
        H = subs[h]
        a_ak, a_rk, l_ab, pw = {}, {}, {}, {}
        H["a_rb"], H["t_inv"], H["av"] = {}, {}, {}
        a_rb, t_inv, av = H["a_rb"], H["t_inv"], H["av"]

        def scores():
            for s in streams:
                ar = jnp.concatenate([blk(at_ref, h, s), blk(rt_ref, h, s)], axis=0)
                ak = _dot_nt(ar, _block_diag(blk(kh_ref, h, s)))
                ab = _dot_nt(ar, _block_diag(blk(bh_ref, h, s)))
                a_ak[s] = jnp.where(strict, ak[:C], 0.0)
                a_rk[s] = jnp.where(incl, ak[C:], 0.0)
                l_ab[s] = jnp.where(strict, ab[:C], 0.0)
                a_rb[s] = jnp.where(incl, ab[C:], 0.0).astype(BF16)

        def inv_first():
            for s in streams:
                t_inv[s] = eye + l_ab[s]
                lb = l_ab[s].astype(BF16)
                pw[s] = _dot(lb, _block_diag(lb))

        def inv_level():
            for s in streams:
                tp = _dot(jnp.concatenate([t_inv[s], pw[s]], axis=0).astype(BF16),
                          _block_diag(pw[s].astype(BF16)))
                t_inv[s] = t_inv[s] + tp[:C]
                pw[s] = tp[C:]

        def inv_last():
            for s in streams:
                t_inv[s] = (t_inv[s] + _dot(t_inv[s].astype(BF16), _block_diag(pw[s].astype(BF16)))).astype(BF16)

        def values():
            for s in streams:
                av[s] = _dot(jnp.concatenate([a_ak[s], a_rk[s]], axis=0).astype(BF16),
                             _block_diag(blk(vb_ref, h, s)))

        return [scores, inv_first] + [inv_level] * (int(math.log2(C)) - 2) + [inv_last, values]

    def chain_pieces(h):
        H = subs[h]
        xs, u_b, ys = {}, {}, {}

        def read_state(c):
            def f():
                for q in range(n_groups):
                    s = (c, q)
                    ar = jnp.concatenate([blk(at_ref, h, s), blk(rt_ref, h, s)], axis=0)
                    xs[s] = _dot_nt(ar, _block_diag(state[q].astype(BF16)))
            return f

        def solve(c):
            def f():
                for q in range(n_groups):
                    s = (c, q)
                    x_u = (xs[s][:C] + H["av"][s][:C]).astype(BF16)
                    u_b[s] = _dot(H["t_inv"][s], _block_diag(x_u)).astype(BF16)
            return f

        def update(c):
            def f():
                for q in range(n_groups):
                    s = (c, q)
                    ys[s] = xs[s][C:] + H["av"][s][C:] + _dot(H["a_rb"][s], _block_diag(u_b[s]))
                    vu = jnp.concatenate([blk(vb_ref, h, s), u_b[s]], axis=0)
                    kb = jnp.concatenate([blk(ke_ref, h, s), blk(be_ref, h, s)], axis=0)
                    upd = _dot_tn(vu, kb)
                    upd_c = sum(jnp.where(lane_head == hd, upd[hd * HEAD_DIM:(hd + 1) * HEAD_DIM, :], 0.0)
                                for hd in range(GROUP))
                    p_c = pc_ref[h, c:c + 1, q * GROUP_W:(q + 1) * GROUP_W]
                    state[q] = state[q] * p_c + upd_c
                if c == SUB_CHUNKS - 1:
                    H["y"] = jnp.concatenate([jnp.concatenate([ys[(cc, q)] for q in range(n_groups)], axis=1)
                                              for cc in range(SUB_CHUNKS)], axis=0)
            return f

        return [p for c in range(SUB_CHUNKS) for p in (read_state(c), solve(c), update(c))]

    def epilogue_pieces(h):
        H = subs[h]
        rows = slice(h * SB, (h + 1) * SB)

        def center():
            mu = _head_sums(H["y"], ones_bd) * (1.0 / HEAD_DIM)
            H["d"] = H["y"] - mu

        def scale():
            d = H["d"]
            var = _head_sums(d * d, ones_bd) * (1.0 / HEAD_DIM)
            yn = d * lax.rsqrt(var + LNX_EPS) * lng_ref[...] + lnb_ref[...]
            y_ref[rows, :] = ((yn + bonus_ref[rows, :].astype(F32)) * gate_ref[rows, :].astype(F32)).astype(y_ref.dtype)

        return [center, scale]

    def run_merged(*lists):
        lists = [pieces for pieces in lists if pieces]
        if not lists:
            return
        main = max(lists, key=len)
        others = [pieces for pieces in lists if pieces is not main]
        n = len(main)
        for i, piece in enumerate(main):
            piece()
            for other in others:
                for j, o in enumerate(other):
                    if j * n // len(other) == i:
                        o()

    for h in range(n_sub + 2):
        run_merged(chain_pieces(h - 1) if 1 <= h <= n_sub else [],
                   local_pieces(h) if h < n_sub else [],
                   epilogue_pieces(h - 2) if h >= 2 else [])
    for q in range(n_groups):
        s_ref[q] = state[q]


def _rwkv(a_t, k_h, b_h, r_t, k_e, b_e, v_b, bonus, gate, p_c, ln_g, ln_b, side_casts=(), *, tb=4 * SUB_BLOCK):
    T = a_t.shape[0]
    n_sub = tb // SUB_BLOCK
    assert p_c.shape == (T // SUB_BLOCK, SUBLANES, RWKV_WIDTH)
    vec = _resident((1, RWKV_WIDTH))
    tile = pl.BlockSpec((tb, RWKV_WIDTH), lambda i: (i, 0))
    cast_specs, cast_shapes = _side_cast_specs(side_casts, T // tb)
    return pl.pallas_call(
        functools.partial(_rwkv_kernel, n_sub=n_sub, n_cast=len(side_casts)),
        grid=(T // tb,),
        in_specs=[tile] * 9 + [pl.BlockSpec((n_sub, SUBLANES, RWKV_WIDTH), lambda i: (i, 0, 0)), vec, vec]
        + cast_specs,
        out_specs=[tile] + cast_specs,
        out_shape=[jax.ShapeDtypeStruct((T, RWKV_WIDTH), BF16)] + cast_shapes,
        scratch_shapes=[pltpu.VMEM((RWKV_WIDTH // GROUP_W, HEAD_DIM, GROUP_W), F32)],
        compiler_params=pltpu.CompilerParams(dimension_semantics=("arbitrary",),
                                             vmem_limit_bytes=VMEM_LIMIT),
        name="rwkv7",
    )(a_t, k_h, b_h, r_t, k_e, b_e, v_b, bonus, gate, p_c, ln_g, ln_b, *side_casts)


def _rope(x, cos_l, sin_a, sin_b):
    return x * cos_l + pltpu.roll(x, LANES - ROT_DIM // 2, axis=1) * sin_a + pltpu.roll(x, ROT_DIM // 2, axis=1) * sin_b


def _attn_kernel(sink_ref, q_ref, kv_ref, kvp_ref, cos_ref, sa_ref, sb_ref, cosp_ref, sap_ref, sbp_ref,
                 g_ref, o_ref):
    n = pl.program_id(0)
    B = WINDOW
    cos_l, sin_a, sin_b = cos_ref[...], sa_ref[...], sb_ref[...]
    kv = kv_ref[...]
    kvp = kvp_ref[...]
    k_cur = _rope(kv[:, :LANES], cos_l, sin_a, sin_b)
    k_prev = _rope(kvp[:, :LANES], cosp_ref[...], sap_ref[...], sbp_ref[...])
    k_win = jnp.concatenate([k_prev, k_cur], axis=0)
    v_win = jnp.concatenate([kvp[:, LANES:], kv[:, LANES:]], axis=0)
    lane = lax.broadcasted_iota(jnp.int32, (2 * B, LANES), 1)
    low = lane < HEAD_DIM
    k_sw = pltpu.roll(k_win, HEAD_DIM, axis=1)
    v_sw = pltpu.roll(v_win, HEAD_DIM, axis=1)
    k_dup = [jnp.where(low, k_win, k_sw).astype(BF16), jnp.where(low, k_sw, k_win).astype(BF16)]
    v_dup = [jnp.where(low, v_win, v_sw).astype(BF16), jnp.where(low, v_sw, v_win).astype(BF16)]

    qi = lax.broadcasted_iota(jnp.int32, (2 * B, 2 * B), 0) % B
    si = lax.broadcasted_iota(jnp.int32, (2 * B, 2 * B), 1)
    allowed = (si > qi) & (si <= qi + B) & ((si >= B) | (n > 0))
    top = lax.broadcasted_iota(jnp.int32, (2 * B, 1), 0) < B
    qlow = lax.broadcasted_iota(jnp.int32, (B, LANES), 1) < HEAD_DIM
    pairs_per_kv = (ATTN_HEADS // KV_HEADS) // 2
    pairs = range(ATTN_HEADS // 2)
    scores, es, denoms, outs = [], [], [], []
    for p in pairs:
        q = _rope(q_ref[:, p * LANES:(p + 1) * LANES], cos_l, sin_a, sin_b) * (HEAD_DIM ** -0.5)
        zero = jnp.zeros_like(q)
        q_st = jnp.concatenate([jnp.where(qlow, q, zero), jnp.where(qlow, zero, q)], axis=0).astype(BF16)
        scores.append(_dot_nt(q_st, k_dup[p // pairs_per_kv]))
    for p in pairs:
        s = jnp.where(allowed, scores[p], -1e30)
        sink = jnp.where(top, sink_ref[2 * p], sink_ref[2 * p + 1])
        m = jnp.maximum(jnp.max(s, axis=-1, keepdims=True), sink)
        e = jnp.exp(s - m)
        denoms.append(jnp.sum(e, axis=-1, keepdims=True) + jnp.exp(sink - m))
        es.append(e.astype(BF16))
    for p in pairs:
        o_st = _dot(es[p], v_dup[p // pairs_per_kv]) / denoms[p]
        outs.append(jnp.where(qlow, o_st[:B], o_st[B:]))
    o = jnp.concatenate(outs, axis=1)
    o_ref[...] = _rms(o, g_ref[...]).astype(o_ref.dtype)


def _attn(attn_in, sinks, cos_l, sin_a, sin_b, g):
    T = attn_in.shape[0]
    B = WINDOW
    grid = (T // B,)
    kv_blk = ATTN_WIDTH // (2 * KV_WIDTH)
    cur = lambda n: (n, 0)
    prev = lambda n: (jnp.maximum(n - 1, 0), 0)
    tab = lambda im: pl.BlockSpec((B, LANES), im)
    return pl.pallas_call(
        _attn_kernel,
        grid=grid,
        in_specs=[
            pl.BlockSpec(memory_space=pltpu.SMEM),
            pl.BlockSpec((B, ATTN_WIDTH), cur),
            pl.BlockSpec((B, 2 * KV_WIDTH), lambda n: (n, kv_blk)),
            pl.BlockSpec((B, 2 * KV_WIDTH), lambda n: (jnp.maximum(n - 1, 0), kv_blk)),
            tab(cur), tab(cur), tab(cur), tab(prev), tab(prev), tab(prev),
            _resident((1, ATTN_WIDTH)),
        ],
        out_specs=pl.BlockSpec((B, ATTN_WIDTH), cur),
        out_shape=jax.ShapeDtypeStruct((T, ATTN_WIDTH), BF16),
        compiler_params=pltpu.CompilerParams(dimension_semantics=("arbitrary",),
                                             vmem_limit_bytes=VMEM_LIMIT),
        name="swa_attn",
    )(sinks, attn_in, attn_in, attn_in, cos_l, sin_a, sin_b, cos_l, sin_a, sin_b, g)


def _out_proj_kernel(x_ref, yr_ref, ya_ref, w_ref, o_ref):
    o_ref[...] = (x_ref[...] + _dot(yr_ref[...], w_ref[:RWKV_WIDTH, :])
                  + _dot(ya_ref[...], w_ref[RWKV_WIDTH:, :]))


def _out_proj(x2d, y_rwkv, y_attn, w, *, tm=512):
    T = x2d.shape[0]
    row = lambda width: pl.BlockSpec((tm, width), lambda i: (i, 0))
    return pl.pallas_call(
        _out_proj_kernel,
        grid=(T // tm,),
        in_specs=[row(D_MODEL), row(RWKV_WIDTH), row(ATTN_WIDTH), _resident(w.shape)],
        out_specs=row(D_MODEL),
        out_shape=jax.ShapeDtypeStruct((T, D_MODEL), F32),
        compiler_params=pltpu.CompilerParams(dimension_semantics=("arbitrary",),
                                             vmem_limit_bytes=VMEM_LIMIT),
        name="out_proj",
    )(x2d, y_rwkv, y_attn, w)


def _ffn_kernel(x_ref, xp_ref, g_ref, wup_ref, wgate_ref, cw_ref, cb_ref, wdown_ref, gf_ref, o_ref, hext_ref):
    i = pl.program_id(0)
    f = pl.program_id(1)

    @pl.when(f == 0)
    def _():
        g = g_ref[...]
        x = x_ref[...]
        hext_ref[PREV_ROWS:, :] = _rms(x, g).astype(BF16)
        hp = jnp.where(i > 0, _rms(xp_ref[...], g), 0.0)
        hext_ref[:PREV_ROWS, :] = hp.astype(BF16)
        o_ref[...] = x

    hx = hext_ref[...]
    u = _dot(hx, wup_ref[...])
    cw = cw_ref[...]
    conv = cb_ref[...] + cw[2:3, :] * u + cw[1:2, :] * pltpu.roll(u, 1, axis=0) + cw[0:1, :] * pltpu.roll(u, 2, axis=0)
    conv = conv[PREV_ROWS:, :]
    gate = _dot(hext_ref[PREV_ROWS:, :], wgate_ref[...])
    act = conv * _sigmoid(conv) * gate
    o_ref[...] += _dot(act.astype(BF16), wdown_ref[...])

    @pl.when(f == pl.num_programs(1) - 1)
    def _():
        o_ref[...] = _rms(o_ref[...], gf_ref[...])


def _ffn(x1, g, w_up, w_gate, conv_w, conv_b, w_down, g_final, *, tm=1024, tf=512):
    T = x1.shape[0]
    prev_blocks = tm // PREV_ROWS
    return pl.pallas_call(
        _ffn_kernel,
        grid=(T // tm, D_FF // tf),
        in_specs=[
            pl.BlockSpec((tm, D_MODEL), lambda i, f: (i, 0)),
            pl.BlockSpec((PREV_ROWS, D_MODEL), lambda i, f: (jnp.maximum(i * prev_blocks - 1, 0), 0)),
            pl.BlockSpec((1, D_MODEL), lambda i, f: (0, 0)),
            pl.BlockSpec((D_MODEL, tf), lambda i, f: (0, f)),
            pl.BlockSpec((D_MODEL, tf), lambda i, f: (0, f)),
            pl.BlockSpec((3, tf), lambda i, f: (0, f)),
            pl.BlockSpec((1, tf), lambda i, f: (0, f)),
            pl.BlockSpec((tf, D_MODEL), lambda i, f: (f, 0)),
            pl.BlockSpec((1, D_MODEL), lambda i, f: (0, 0)),
        ],
        out_specs=pl.BlockSpec((tm, D_MODEL), lambda i, f: (i, 0)),
        out_shape=jax.ShapeDtypeStruct((T, D_MODEL), F32),
        scratch_shapes=[pltpu.VMEM((tm + PREV_ROWS, D_MODEL), BF16)],
        compiler_params=pltpu.CompilerParams(dimension_semantics=("arbitrary", "arbitrary"),
                                             vmem_limit_bytes=FFN_VMEM_LIMIT),
        name="ffn",
    )(x1, x1, g, w_up, w_gate, conv_w, conv_b, w_down, g_final)


def _pad_cols(a, width):
    return jnp.pad(a, ((0, 0), (0, width - a.shape[1])))


def _pad_rows(a, rows):
    return jnp.pad(a, ((0, rows - a.shape[0]), (0, 0)))


def _rope_lane_tables(T):
    half = ROT_DIM // 2
    inv_freq = ROPE_THETA ** (-jnp.arange(0, ROT_DIM, 2, dtype=F32) / ROT_DIM)
    ang = jnp.arange(T, dtype=F32)[:, None] * inv_freq[None, :]
    cos, sin = jnp.cos(ang), jnp.sin(ang)
    ones = jnp.ones((T, HEAD_DIM - ROT_DIM), F32)
    zeros_h = jnp.zeros((T, half), F32)
    zeros_r = jnp.zeros((T, HEAD_DIM - ROT_DIM), F32)
    cos_h = jnp.concatenate([cos, cos, ones], axis=1)
    sa_h = jnp.concatenate([-sin, zeros_h, zeros_r], axis=1)
    sb_h = jnp.concatenate([zeros_h, sin, zeros_r], axis=1)
    rep = LANES // HEAD_DIM
    return jnp.tile(cos_h, (1, rep)), jnp.tile(sa_h, (1, rep)), jnp.tile(sb_h, (1, rep))


_IN_PROJ_SEGMENTS = (
    (3 * RWKV_WIDTH, W_LORA, LANES),
    (3 * RWKV_WIDTH + W_LORA, A_LORA, LANES),
    (3 * RWKV_WIDTH + W_LORA + A_LORA, G_LORA, 2 * LANES),
    (RWKV_WIDTH, RWKV_WIDTH, RWKV_WIDTH),
    (0, RWKV_WIDTH, RWKV_WIDTH),
    (2 * RWKV_WIDTH, RWKV_WIDTH, RWKV_WIDTH),
    (3 * RWKV_WIDTH + W_LORA + A_LORA + G_LORA, ATTN_WIDTH + 2 * KV_WIDTH, ATTN_WIDTH + 2 * KV_WIDTH),
)


def _in_proj_cols(a):
    return jnp.concatenate([_pad_cols(a[:, src:src + width], slot) for src, width, slot in _IN_PROJ_SEGMENTS],
                           axis=1)


def _regroup_kernel(wt_ref, o_ref):
    dst = 0
    for src, width, slot in _IN_PROJ_SEGMENTS:
        o_ref[dst:dst + width, :] = wt_ref[src:src + width, :].astype(o_ref.dtype)
        if slot > width:
            o_ref[dst + width:dst + slot, :] = jnp.zeros((slot - width, o_ref.shape[1]), o_ref.dtype)
        dst += slot


def _regroup_w_in_t(w_t, *, tc=256):
    n_out, n_in = w_t.shape
    out_rows = sum(slot for _, _, slot in _IN_PROJ_SEGMENTS)
    return pl.pallas_call(
        _regroup_kernel,
        grid=(n_in // tc,),
        in_specs=[pl.BlockSpec((n_out, tc), lambda i: (0, i))],
        out_specs=pl.BlockSpec((out_rows, tc), lambda i: (0, i)),
        out_shape=jax.ShapeDtypeStruct((out_rows, n_in), BF16),
        compiler_params=pltpu.CompilerParams(dimension_semantics=("arbitrary",), vmem_limit_bytes=VMEM_LIMIT),
        name="regroup_w_in",
    )(w_t)


def kernel(x, ln_mix_g, w_in, b_attn_qkv, rwkv_shift_mu, rwkv_w0, rwkv_w2, rwkv_a0, rwkv_a2, rwkv_g2, rwkv_k_k, rwkv_k_a, rwkv_r_k, rwkv_lnx_g, rwkv_lnx_b, attn_sinks, attn_out_g, w_out, ln_ffn_g, ffn_w_up, ffn_w_gate, ffn_conv_w, ffn_conv_b, ffn_w_down, ln_final_g):
    B, T, _ = x.shape
    assert B == 1 and ln_mix_g.shape[0] == 1
    l = 0
    x2d = x.reshape(T, D_MODEL)
    row = lambda a: a.reshape(1, -1)

    w_all = _regroup_w_in_t(jnp.swapaxes(w_in[l], 0, 1))
    mu_shift = _in_proj_cols(row(rwkv_shift_mu[l]))
    (a_t, k_h, b_h, r_t, k_e, b_e, v_b, bonus, gate, p_c, attn_in, w_up, w_gate, w_down, w_o) = _in_proj(
        x2d, row(ln_mix_g[l]), w_all, mu_shift, row(b_attn_qkv[l]),
        row(rwkv_w0[l]), _pad_rows(rwkv_w2[l], LANES).astype(BF16),
        row(rwkv_a0[l]), _pad_rows(rwkv_a2[l], LANES).astype(BF16),
        _pad_rows(rwkv_g2[l], 2 * LANES).astype(BF16), row(rwkv_k_k[l]), row(rwkv_k_a[l]), row(rwkv_r_k[l]),
        side_casts=(ffn_w_up[l], ffn_w_gate[l], ffn_w_down[l], w_out[l]))

    (y_rwkv,) = _rwkv(a_t, k_h, b_h, r_t, k_e, b_e, v_b, bonus, gate, p_c,
                      row(rwkv_lnx_g[l]), row(rwkv_lnx_b[l]))

    cos_l, sin_a, sin_b = _rope_lane_tables(T)
    y_attn = _attn(attn_in, attn_sinks[l], cos_l, sin_a, sin_b, row(attn_out_g[l]))

    x1 = _out_proj(x2d, y_rwkv, y_attn, w_o)

    out = _ffn(x1, row(ln_ffn_g[l]), w_up, w_gate, ffn_conv_w[l], row(ffn_conv_b[l]), w_down, row(ln_final_g))
    return out.reshape(B, T, D_MODEL)
```

```python
import functools
import math

import jax
import jax.numpy as jnp
from jax import lax
from jax.experimental import pallas as pl
from jax.experimental.pallas import tpu as pltpu

F32 = jnp.float32
BF16 = jnp.bfloat16

D_MODEL = 2048
HEAD_DIM = 64
RWKV_WIDTH = 1024
ATTN_WIDTH = 1024
ATTN_HEADS = 16
KV_HEADS = 2
KV_WIDTH = KV_HEADS * HEAD_DIM
W_LORA = 64
A_LORA = 64
G_LORA = 160
WINDOW = 128
ROPE_THETA = 500000.0
ROT_DIM = 16
D_FF = 5632
NORM_EPS = 1e-5
LNX_EPS = 64e-5

LANES = 128
SUBLANES = 8
PREV_ROWS = 16
LORA_PAD = 512
GROUP = 4
GROUP_W = GROUP * HEAD_DIM
CHUNK = 64
SUB_CHUNKS = 4
SUB_BLOCK = SUB_CHUNKS * CHUNK
VMEM_LIMIT = 56 * 1024 * 1024
IN_PROJ_VMEM_LIMIT = 60 * 1024 * 1024
FFN_VMEM_LIMIT = 60 * 1024 * 1024

assert CHUNK == HEAD_DIM and SUB_CHUNKS <= SUBLANES


def _dot(a, b):
    return lax.dot_general(a, b, (((1,), (0,)), ((), ())), preferred_element_type=F32)


def _dot_nt(a, b):
    return lax.dot_general(a, b, (((1,), (1,)), ((), ())), preferred_element_type=F32)


def _dot_tn(a, b):
    return lax.dot_general(a, b, (((0,), (0,)), ((), ())), preferred_element_type=F32)


def _sigmoid(x):
    return 1.0 / (1.0 + jnp.exp(-x))


def _sigmoid_tanh(x):
    return 0.5 + 0.5 * jnp.tanh(0.5 * x)


def _rms(xv, g):
    ms = jnp.mean(xv * xv, axis=-1, keepdims=True)
    return xv * lax.rsqrt(ms + NORM_EPS) * g


def _resident(shape):
    nd = len(shape)
    return pl.BlockSpec(shape, lambda *_: (0,) * nd, pipeline_mode=pl.Buffered(1))


def _split_bf16(x):
    hi = x.astype(BF16)
    lo = (x - hi.astype(F32)).astype(BF16)
    return hi, lo


def _same_head_mask():
    r_i = lax.broadcasted_iota(jnp.int32, (GROUP_W, GROUP_W), 0) // HEAD_DIM
    c_i = lax.broadcasted_iota(jnp.int32, (GROUP_W, GROUP_W), 1) // HEAD_DIM
    return r_i == c_i


def _head_sums(x, ones_bd):
    xb = x.astype(BF16)
    return jnp.concatenate([_dot(xb[:, q * GROUP_W:(q + 1) * GROUP_W], ones_bd)
                            for q in range(RWKV_WIDTH // GROUP_W)], axis=1)


def _rope(x, cos_l, sin_a, sin_b):
    return x * cos_l + pltpu.roll(x, LANES - ROT_DIM // 2, axis=1) * sin_a + pltpu.roll(x, ROT_DIM // 2, axis=1) * sin_b


def _split_refs(refs, *counts):
    out, pos = [], 0
    for n in counts:
        out.append(tuple(refs[pos:pos + n]))
        pos += n
    assert pos == len(refs)
    return out


def _side_cast_specs(arrays, n_steps):
    assert all(a.shape[0] % (n_steps * PREV_ROWS) == 0 for a in arrays)
    specs = [pl.BlockSpec((a.shape[0] // n_steps, a.shape[1]), lambda i: (i, 0)) for a in arrays]
    shapes = [jax.ShapeDtypeStruct(a.shape, BF16) for a in arrays]
    return specs, shapes


def _side_casts(src_refs, dst_refs):
    for src, dst in zip(src_refs, dst_refs):
        dst[...] = src[...].astype(dst.dtype)


def _in_proj_kernel(x_ref, xp_ref, g_ref, w_ref, mu_ref, battn_ref, w0_ref, w2_ref, a0_ref, a2_ref, g2_ref,
                    kk_ref, ka_ref, rk_ref, cos_ref, sa_ref, sb_ref, *rest, col_chunk, n_cast):
    (cast_in, (at_ref, kh_ref, bh_ref, rt_ref, ke_ref, be_ref, vb_ref, bonus_ref, gate_ref, pc_ref, attn_ref),
     cast_out, (hext_ref,)) = _split_refs(rest, n_cast, 11, n_cast, 1)
    i = pl.program_id(0)
    tm = x_ref.shape[0]
    n_chunks = tm // CHUNK
    C = CHUNK
    _side_casts(cast_in, cast_out)
    g = g_ref[...]
    hext_ref[PREV_ROWS:, :] = _rms(x_ref[...], g).astype(BF16)
    hp = jnp.where(i > 0, _rms(xp_ref[...], g), 0.0)
    hext_ref[:PREV_ROWS, :] = hp.astype(BF16)
    hx = hext_ref[...]

    def shifted(base, width):
        outs = []
        for c0 in range(base, base + width, col_chunk):
            z = _dot_nt(hx, w_ref[c0:c0 + col_chunk, :])
            zp = pltpu.roll(z, 1, axis=0)
            outs.append((z + mu_ref[:, c0:c0 + col_chunk] * (zp - z))[PREV_ROWS:, :])
        return jnp.concatenate(outs, axis=1)

    def attn_cols(c0):
        base = LORA_PAD + 3 * RWKV_WIDTH
        c1 = min(c0 + col_chunk, attn_ref.shape[1])
        z = _dot_nt(hext_ref[PREV_ROWS:, :], w_ref[base + c0:base + c1, :]) + battn_ref[:, c0:c1]
        slabs = []
        for col in range(c0, c1, LANES):
            zs = z[:, col - c0:col - c0 + LANES]
            if col < ATTN_WIDTH + KV_WIDTH:
                zs = _rope(zs, cos_ref[...], sa_ref[...], sb_ref[...])
            if col < ATTN_WIDTH:
                zs = zs * (HEAD_DIM ** -0.5)
            slabs.append(zs)
        attn_ref[:, c0:c1] = jnp.concatenate(slabs, axis=1)

    ones_bd = _same_head_mask().astype(BF16)
    t_i = lax.broadcasted_iota(jnp.int32, (tm, tm), 0)
    t_j = lax.broadcasted_iota(jnp.int32, (tm, tm), 1)
    tri = ((t_j <= t_i) & (t_j // C == t_i // C)).astype(BF16)
    attn_chunks = list(range(0, attn_ref.shape[1], col_chunk))

    lo = shifted(0, LORA_PAD)
    k = shifted(LORA_PAD, RWKV_WIDTH)
    xw = lo[:, 0:LANES]
    xa = lo[:, LANES:2 * LANES]
    xg = lo[:, 2 * LANES:LORA_PAD]
    u = w0_ref[...] + _dot(jnp.tanh(xw).astype(BF16), w2_ref[...])
    lw = -math.exp(-0.5) * _sigmoid_tanh(u)
    a = _sigmoid_tanh(a0_ref[...] + _dot(xa.astype(BF16), a2_ref[...]))
    gate_ref[...] = _dot(_sigmoid_tanh(xg).astype(BF16), g2_ref[...]).astype(gate_ref.dtype)
    lw_hi, lw_lo = _split_bf16(lw)
    cl = _dot(tri, lw_hi) + _dot(tri, lw_lo)
    last_rows = [cl[(c + 1) * C - 1:(c + 1) * C, :] for c in range(n_chunks)]
    cl_last = jnp.concatenate([jnp.broadcast_to(row, (C, RWKV_WIDTH)) for row in last_rows], axis=0)
    pc_ref[...] = jnp.concatenate([jnp.exp(row) for row in last_rows]
                                  + [jnp.zeros((SUBLANES - n_chunks, RWKV_WIDTH), F32)], axis=0)

    r = shifted(LORA_PAD + RWKV_WIDTH, RWKV_WIDTH)
    kkraw = k * kk_ref[...]
    k2 = k * (1.0 + (a - 1.0) * ka_ref[...])

    v = shifted(LORA_PAD + 2 * RWKV_WIDTH, RWKV_WIDTH)
    sums = _head_sums(jnp.concatenate([kkraw * kkraw, r * k2 * rk_ref[...]], axis=0), ones_bd)
    kk = kkraw / jnp.maximum(jnp.sqrt(sums[:tm]), 1e-12)
    b = kk * a

    attn_cols(attn_chunks[0])
    at_ref[...] = (-kk * jnp.exp(cl - lw)).astype(at_ref.dtype)
    e_neg = jnp.exp(-cl)
    kh_ref[...] = (k2 * e_neg).astype(kh_ref.dtype)
    bh_ref[...] = (b * e_neg).astype(bh_ref.dtype)

    for c0 in attn_chunks[1:2]:
        attn_cols(c0)
    rt_ref[...] = (r * jnp.exp(cl)).astype(rt_ref.dtype)
    e_end = jnp.exp(cl_last - cl)
    ke_ref[...] = (k2 * e_end).astype(ke_ref.dtype)
    be_ref[...] = (b * e_end).astype(be_ref.dtype)

    for c0 in attn_chunks[2:]:
        attn_cols(c0)
    bonus_ref[...] = (sums[tm:] * v).astype(bonus_ref.dtype)
    vb_ref[...] = v.astype(vb_ref.dtype)


def _in_proj(x2d, g, w_all, mu_shift, b_attn, w0, w2, a0, a2, g2, k_k, k_a, r_k, rope_tables, side_casts=(), *,
             tm=SUB_BLOCK, col_chunk=512):
    T = x2d.shape[0]
    n_attn = b_attn.shape[1]
    assert w_all.shape == (3 * RWKV_WIDTH + LORA_PAD + n_attn, D_MODEL) and tm % CHUNK == 0
    grid = (T // tm,)
    cast_specs, cast_shapes = _side_cast_specs(side_casts, T // tm)
    prev_blocks = tm // PREV_ROWS
    vec = _resident((1, RWKV_WIDTH))
    tile = pl.BlockSpec((tm, RWKV_WIDTH), lambda i: (i, 0))
    tile_shape = jax.ShapeDtypeStruct((T, RWKV_WIDTH), BF16)
    return pl.pallas_call(
        functools.partial(_in_proj_kernel, col_chunk=col_chunk, n_cast=len(side_casts)),
        grid=grid,
        in_specs=[
            pl.BlockSpec((tm, D_MODEL), lambda i: (i, 0)),
            pl.BlockSpec((PREV_ROWS, D_MODEL), lambda i: (jnp.maximum(i * prev_blocks - 1, 0), 0)),
            _resident((1, D_MODEL)),
            _resident(w_all.shape), _resident(mu_shift.shape), _resident(b_attn.shape),
            vec, _resident(w2.shape), vec, _resident(a2.shape), _resident(g2.shape), vec, vec, vec,
        ] + [pl.BlockSpec((tm, LANES), lambda i: (i, 0))] * 3 + cast_specs,
        out_specs=[tile] * 9 + [
            pl.BlockSpec((None, SUBLANES, RWKV_WIDTH), lambda i: (i, 0, 0)),
            pl.BlockSpec((tm, n_attn), lambda i: (i, 0)),
        ] + cast_specs,
        out_shape=[tile_shape] * 9 + [
            jax.ShapeDtypeStruct((T // tm, SUBLANES, RWKV_WIDTH), F32),
            jax.ShapeDtypeStruct((T, n_attn), F32),
        ] + cast_shapes,
        scratch_shapes=[pltpu.VMEM((tm + PREV_ROWS, D_MODEL), BF16)],
        compiler_params=pltpu.CompilerParams(dimension_semantics=("arbitrary",),
                                             vmem_limit_bytes=IN_PROJ_VMEM_LIMIT),
        name="in_proj",
    )(x2d, x2d, g, w_all, mu_shift, b_attn, w0, w2, a0, a2, g2, k_k, k_a, r_k, *rope_tables, *side_casts)


def _block_diag(x):
    blk = lax.broadcasted_iota(jnp.int32, x.shape, 1) // HEAD_DIM
    zero = jnp.zeros_like(x)
    return jnp.concatenate([jnp.where(blk == h, x, zero) for h in range(GROUP)], axis=0)


def _rwkv_kernel(at_ref, kh_ref, bh_ref, rt_ref, ke_ref, be_ref, vb_ref, bonus_ref, gate_ref, pc_ref,
                 lng_ref, lnb_ref, *rest, n_sub, n_cast):
    cast_in, (y_ref,), cast_out, (s_ref,) = _split_refs(rest, n_cast, 1, n_cast, 1)
    C = CHUNK
    SB = SUB_BLOCK
    n_groups = RWKV_WIDTH // GROUP_W

    @pl.when(pl.program_id(0) == 0)
    def _():
        s_ref[...] = jnp.zeros_like(s_ref)

    _side_casts(cast_in, cast_out)
    row = lax.broadcasted_iota(jnp.int32, (C, GROUP_W), 0)
    col = lax.broadcasted_iota(jnp.int32, (C, GROUP_W), 1) % C
    strict = col < row
    incl = col <= row
    eye = (col == row).astype(F32)
    lane_head = lax.broadcasted_iota(jnp.int32, (HEAD_DIM, GROUP_W), 1) // HEAD_DIM
    ones_bd = _same_head_mask().astype(BF16)
    streams = [(c, q) for c in range(SUB_CHUNKS) for q in range(n_groups)]
    state = [s_ref[q] for q in range(n_groups)]
    subs = [dict() for _ in range(n_sub)]

    def blk(ref, h, s):
        c, q = s
        return ref[h * SB + c * C:h * SB + (c + 1) * C, q * GROUP_W:(q + 1) * GROUP_W]

    def local_pieces(h):
        H = subs[h]
        a_ak, a_rk, l_ab, pw = {}, {}, {}, {}
        H["a_rb"], H["t_inv"], H["av"] = {}, {}, {}
        a_rb, t_inv, av = H["a_rb"], H["t_inv"], H["av"]

        def scores():
            for s in streams:
                ar = jnp.concatenate([blk(at_ref, h, s), blk(rt_ref, h, s)], axis=0)
                ak = _dot_nt(ar, _block_diag(blk(kh_ref, h, s)))
                ab = _dot_nt(ar, _block_diag(blk(bh_ref, h, s)))
                a_ak[s] = jnp.where(strict, ak[:C], 0.0)
                a_rk[s] = jnp.where(incl, ak[C:], 0.0)
                l_ab[s] = jnp.where(strict, ab[:C], 0.0)
                a_rb[s] = jnp.where(incl, ab[C:], 0.0).astype(BF16)

        def inv_first():
            for s in streams:
                t_inv[s] = eye + l_ab[s]
                lb = l_ab[s].astype(BF16)
                pw[s] = _dot(lb, _block_diag(lb))

        def inv_level():
            for s in streams:
                tp = _dot(jnp.concatenate([t_inv[s], pw[s]], axis=0).astype(BF16),
                          _block_diag(pw[s].astype(BF16)))
                t_inv[s] = t_inv[s] + tp[:C]
                pw[s] = tp[C:]

        def inv_last():
            for s in streams:
                t_inv[s] = (t_inv[s] + _dot(t_inv[s].astype(BF16), _block_diag(pw[s].astype(BF16)))).astype(BF16)

        def values():
            for s in streams:
                av[s] = _dot(jnp.concatenate([a_ak[s], a_rk[s]], axis=0).astype(BF16),
                             _block_diag(blk(vb_ref, h, s)))

        return [scores, inv_first] + [inv_level] * (int(math.log2(C)) - 2) + [inv_last, values]

    def chain_pieces(h):
        H = subs[h]
        xs, u_b, ys = {}, {}, {}

        def read_state(c):
            def f():
                for q in range(n_groups):
                    s = (c, q)
                    ar = jnp.concatenate([blk(at_ref, h, s), blk(rt_ref, h, s)], axis=0)
                    xs[s] = _dot_nt(ar, _block_diag(state[q].astype(BF16)))
            return f

        def solve(c):
            def f():
                for q in range(n_groups):
                    s = (c, q)
                    x_u = (xs[s][:C] + H["av"][s][:C]).astype(BF16)
                    u_b[s] = _dot(H["t_inv"][s], _block_diag(x_u)).astype(BF16)
            return f

        def update(c):
            def f():
                for q in range(n_groups):
                    s = (c, q)
                    ys[s] = xs[s][C:] + H["av"][s][C:] + _dot(H["a_rb"][s], _block_diag(u_b[s]))
                    vu = jnp.concatenate([blk(vb_ref, h, s), u_b[s]], axis=0)
                    kb = jnp.concatenate([blk(ke_ref, h, s), blk(be_ref, h, s)], axis=0)
                    upd = _dot_tn(vu, kb)
                    upd_c = sum(jnp.where(lane_head == hd, upd[hd * HEAD_DIM:(hd + 1) * HEAD_DIM, :], 0.0)
                                for hd in range(GROUP))
                    p_c = pc_ref[h, c:c + 1, q * GROUP_W:(q + 1) * GROUP_W]
                    state[q] = state[q] * p_c + upd_c
                if c == SUB_CHUNKS - 1:
                    H["y"] = jnp.concatenate([jnp.concatenate([ys[(cc, q)] for q in range(n_groups)], axis=1)
                                              for cc in range(SUB_CHUNKS)], axis=0)
            return f

        return [p for c in range(SUB_CHUNKS) for p in (read_state(c), solve(c), update(c))]

    def epilogue_pieces(h):
        H = subs[h]
        rows = slice(h * SB, (h + 1) * SB)

        def center():
            mu = _head_sums(H["y"], ones_bd) * (1.0 / HEAD_DIM)
            H["d"] = H["y"] - mu

        def scale():
            d = H["d"]
            var = _head_sums(d * d, ones_bd) * (1.0 / HEAD_DIM)
            yn = d * lax.rsqrt(var + LNX_EPS) * lng_ref[...] + lnb_ref[...]
            y_ref[rows, :] = ((yn + bonus_ref[rows, :].astype(F32)) * gate_ref[rows, :].astype(F32)).astype(y_ref.dtype)

        return [center, scale]

    def run_merged(*lists):
        lists = [pieces for pieces in lists if pieces]
        if not lists:
            return
        main = max(lists, key=len)
        others = [pieces for pieces in lists if pieces is not main]
        n = len(main)
        for i, piece in enumerate(main):
            piece()
            for other in others:
                for j, o in enumerate(other):
                    if j * n // len(other) == i:
                        o()

    for h in range(n_sub + 2):
        run_merged(chain_pieces(h - 1) if 1 <= h <= n_sub else [],
                   local_pieces(h) if h < n_sub else [],
                   epilogue_pieces(h - 2) if h >= 2 else [])
    for q in range(n_groups):
        s_ref[q] = state[q]


def _rwkv(a_t, k_h, b_h, r_t, k_e, b_e, v_b, bonus, gate, p_c, ln_g, ln_b, side_casts=(), *, tb=2 * SUB_BLOCK):
    T = a_t.shape[0]
    n_sub = tb // SUB_BLOCK
    assert p_c.shape == (T // SUB_BLOCK, SUBLANES, RWKV_WIDTH)
    vec = _resident((1, RWKV_WIDTH))
    tile = pl.BlockSpec((tb, RWKV_WIDTH), lambda i: (i, 0))
    cast_specs, cast_shapes = _side_cast_specs(side_casts, T // tb)
    return pl.pallas_call(
        functools.partial(_rwkv_kernel, n_sub=n_sub, n_cast=len(side_casts)),
        grid=(T // tb,),
        in_specs=[tile] * 9 + [pl.BlockSpec((n_sub, SUBLANES, RWKV_WIDTH), lambda i: (i, 0, 0)), vec, vec]
        + cast_specs,
        out_specs=[tile] + cast_specs,
        out_shape=[jax.ShapeDtypeStruct((T, RWKV_WIDTH), BF16)] + cast_shapes,
        scratch_shapes=[pltpu.VMEM((RWKV_WIDTH // GROUP_W, HEAD_DIM, GROUP_W), F32)],
        compiler_params=pltpu.CompilerParams(dimension_semantics=("arbitrary",),
                                             vmem_limit_bytes=VMEM_LIMIT),
        name="rwkv7",
    )(a_t, k_h, b_h, r_t, k_e, b_e, v_b, bonus, gate, p_c, ln_g, ln_b, *side_casts)


def _attn_kernel(sink_ref, q_ref, kv_ref, kvp_ref, g_ref, o_ref):
    n = pl.program_id(0)
    B = WINDOW
    kv = kv_ref[...]
    kvp = kvp_ref[...]
    k_win = jnp.concatenate([kvp[:, :LANES], kv[:, :LANES]], axis=0)
    v_win = jnp.concatenate([kvp[:, LANES:], kv[:, LANES:]], axis=0)
    key0 = lax.broadcasted_iota(jnp.int32, (2 * B, LANES), 0) == 0
    v_win = jnp.where(key0, 0.0, v_win)
    lane = lax.broadcasted_iota(jnp.int32, (2 * B, LANES), 1)
    low = lane < HEAD_DIM
    k_sw = pltpu.roll(k_win, HEAD_DIM, axis=1)
    v_sw = pltpu.roll(v_win, HEAD_DIM, axis=1)
    k_dup = [jnp.where(low, k_win, k_sw).astype(BF16), jnp.where(low, k_sw, k_win).astype(BF16)]
    v_top = [jnp.where(low, v_win, 1.0).astype(BF16), jnp.where(low, v_sw, 1.0).astype(BF16)]
    v_bot = [jnp.where(low, 1.0, v_sw).astype(BF16), jnp.where(low, 1.0, v_win).astype(BF16)]

    qi = lax.broadcasted_iota(jnp.int32, (2 * B, B), 0) % B
    si = lax.broadcasted_iota(jnp.int32, (2 * B, B), 1)
    allowed_prev = (si > qi) & (n > 0)
    allowed_cur = si <= qi
    sink_col = si == 0
    top = lax.broadcasted_iota(jnp.int32, (2 * B, 1), 0) < B
    qlow = lax.broadcasted_iota(jnp.int32, (B, LANES), 1) < HEAD_DIM
    pairs_per_kv = (ATTN_HEADS // KV_HEADS) // 2
    pairs = range(ATTN_HEADS // 2)
    scores, es, outs = [], [], []
    for p in pairs:
        q = q_ref[:, p * LANES:(p + 1) * LANES]
        zero = jnp.zeros_like(q)
        q_st = jnp.concatenate([jnp.where(qlow, q, zero), jnp.where(qlow, zero, q)], axis=0).astype(BF16)
        scores.append(_dot_nt(q_st, k_dup[p // pairs_per_kv]))
    for p in pairs:
        sink = jnp.where(top, sink_ref[2 * p], sink_ref[2 * p + 1])
        s = jnp.concatenate([jnp.where(allowed_prev, scores[p][:, :B], jnp.where(sink_col, sink, -1e30)),
                             jnp.where(allowed_cur, scores[p][:, B:], -1e30)], axis=1)
        m = jnp.max(s, axis=-1, keepdims=True)
        es.append(jnp.exp((s - m).astype(BF16)))
    for p in pairs:
        kvh = p // pairs_per_kv
        o_top = _dot(es[p][:B], v_top[kvh])
        o_bot = _dot(es[p][B:], v_bot[kvh])
        sums = pltpu.roll(jnp.where(qlow, o_bot, o_top), HEAD_DIM, axis=1)
        outs.append(jnp.where(qlow, o_top, o_bot) / sums)
    o = jnp.concatenate(outs, axis=1)
    o_ref[...] = _rms(o, g_ref[...]).astype(o_ref.dtype)


def _attn(attn_in, sinks, g):
    T = attn_in.shape[0]
    B = WINDOW
    kv_blk = ATTN_WIDTH // (2 * KV_WIDTH)
    return pl.pallas_call(
        _attn_kernel,
        grid=(T // B,),
        in_specs=[
            pl.BlockSpec(memory_space=pltpu.SMEM),
            pl.BlockSpec((B, ATTN_WIDTH), lambda n: (n, 0)),
            pl.BlockSpec((B, 2 * KV_WIDTH), lambda n: (n, kv_blk)),
            pl.BlockSpec((B, 2 * KV_WIDTH), lambda n: (jnp.maximum(n - 1, 0), kv_blk)),
            _resident((1, ATTN_WIDTH)),
        ],
        out_specs=pl.BlockSpec((B, ATTN_WIDTH), lambda n: (n, 0)),
        out_shape=jax.ShapeDtypeStruct((T, ATTN_WIDTH), BF16),
        compiler_params=pltpu.CompilerParams(dimension_semantics=("arbitrary",),
                                             vmem_limit_bytes=VMEM_LIMIT),
        name="swa_attn",
    )(sinks, attn_in, attn_in, attn_in, g)


def _out_proj_kernel(x_ref, yr_ref, ya_ref, w_ref, o_ref):
    o_ref[...] = (x_ref[...] + _dot(yr_ref[...], w_ref[:RWKV_WIDTH, :])
                  + _dot(ya_ref[...], w_ref[RWKV_WIDTH:, :]))


def _out_proj(x2d, y_rwkv, y_attn, w, *, tm=512):
    T = x2d.shape[0]
    row = lambda width: pl.BlockSpec((tm, width), lambda i: (i, 0))
    return pl.pallas_call(
        _out_proj_kernel,
        grid=(T // tm,),
        in_specs=[row(D_MODEL), row(RWKV_WIDTH), row(ATTN_WIDTH), _resident(w.shape)],
        out_specs=row(D_MODEL),
        out_shape=jax.ShapeDtypeStruct((T, D_MODEL), F32),
        compiler_params=pltpu.CompilerParams(dimension_semantics=("arbitrary",),
                                             vmem_limit_bytes=VMEM_LIMIT),
        name="out_proj",
    )(x2d, y_rwkv, y_attn, w)


def _ffn_kernel(x_ref, xp_ref, g_ref, wup_ref, wgate_ref, cw_ref, cb_ref, wdown_ref, gf_ref, o_ref, hext_ref):
    i = pl.program_id(0)
    f = pl.program_id(1)

    @pl.when(f == 0)
    def _():
        g = g_ref[...]
        x = x_ref[...]
        hext_ref[PREV_ROWS:, :] = _rms(x, g).astype(BF16)
        hp = jnp.where(i > 0, _rms(xp_ref[...], g), 0.0)
        hext_ref[:PREV_ROWS, :] = hp.astype(BF16)
        o_ref[...] = x

    hx = hext_ref[...]
    u = _dot(hx, wup_ref[...])
    cw = cw_ref[...]
    conv = cb_ref[...] + cw[2:3, :] * u + cw[1:2, :] * pltpu.roll(u, 1, axis=0) + cw[0:1, :] * pltpu.roll(u, 2, axis=0)
    conv = conv[PREV_ROWS:, :]
    gate = _dot(hext_ref[PREV_ROWS:, :], wgate_ref[...])
    act = conv * _sigmoid(conv) * gate
    o_ref[...] += _dot(act.astype(BF16), wdown_ref[...])

    @pl.when(f == pl.num_programs(1) - 1)
    def _():
        o_ref[...] = _rms(o_ref[...], gf_ref[...])


def _ffn(x1, g, w_up, w_gate, conv_w, conv_b, w_down, g_final, *, tm=1024, tf=512):
    T = x1.shape[0]
    prev_blocks = tm // PREV_ROWS
    return pl.pallas_call(
        _ffn_kernel,
        grid=(T // tm, D_FF // tf),
        in_specs=[
            pl.BlockSpec((tm, D_MODEL), lambda i, f: (i, 0)),
            pl.BlockSpec((PREV_ROWS, D_MODEL), lambda i, f: (jnp.maximum(i * prev_blocks - 1, 0), 0)),
            pl.BlockSpec((1, D_MODEL), lambda i, f: (0, 0)),
            pl.BlockSpec((D_MODEL, tf), lambda i, f: (0, f)),
            pl.BlockSpec((D_MODEL, tf), lambda i, f: (0, f)),
            pl.BlockSpec((3, tf), lambda i, f: (0, f)),
            pl.BlockSpec((1, tf), lambda i, f: (0, f)),
            pl.BlockSpec((tf, D_MODEL), lambda i, f: (f, 0)),
            pl.BlockSpec((1, D_MODEL), lambda i, f: (0, 0)),
        ],
        out_specs=pl.BlockSpec((tm, D_MODEL), lambda i, f: (i, 0)),
        out_shape=jax.ShapeDtypeStruct((T, D_MODEL), F32),
        scratch_shapes=[pltpu.VMEM((tm + PREV_ROWS, D_MODEL), BF16)],
        compiler_params=pltpu.CompilerParams(dimension_semantics=("arbitrary", "arbitrary"),
                                             vmem_limit_bytes=FFN_VMEM_LIMIT),
        name="ffn",
    )(x1, x1, g, w_up, w_gate, conv_w, conv_b, w_down, g_final)


def _pad_cols(a, width):
    return jnp.pad(a, ((0, 0), (0, width - a.shape[1])))


def _pad_rows(a, rows):
    return jnp.pad(a, ((0, rows - a.shape[0]), (0, 0)))


def _rope_lane_tables(T):
    half = ROT_DIM // 2
    inv_freq = ROPE_THETA ** (-jnp.arange(0, ROT_DIM, 2, dtype=F32) / ROT_DIM)
    ang = jnp.arange(T, dtype=F32)[:, None] * inv_freq[None, :]
    cos, sin = jnp.cos(ang), jnp.sin(ang)
    ones = jnp.ones((T, HEAD_DIM - ROT_DIM), F32)
    zeros_h = jnp.zeros((T, half), F32)
    zeros_r = jnp.zeros((T, HEAD_DIM - ROT_DIM), F32)
    cos_h = jnp.concatenate([cos, cos, ones], axis=1)
    sa_h = jnp.concatenate([-sin, zeros_h, zeros_r], axis=1)
    sb_h = jnp.concatenate([zeros_h, sin, zeros_r], axis=1)
    rep = LANES // HEAD_DIM
    return jnp.tile(cos_h, (1, rep)), jnp.tile(sa_h, (1, rep)), jnp.tile(sb_h, (1, rep))


_IN_PROJ_SEGMENTS = (
    (3 * RWKV_WIDTH, W_LORA, LANES),
    (3 * RWKV_WIDTH + W_LORA, A_LORA, LANES),
    (3 * RWKV_WIDTH + W_LORA + A_LORA, G_LORA, 2 * LANES),
    (RWKV_WIDTH, RWKV_WIDTH, RWKV_WIDTH),
    (0, RWKV_WIDTH, RWKV_WIDTH),
    (2 * RWKV_WIDTH, RWKV_WIDTH, RWKV_WIDTH),
    (3 * RWKV_WIDTH + W_LORA + A_LORA + G_LORA, ATTN_WIDTH + 2 * KV_WIDTH, ATTN_WIDTH + 2 * KV_WIDTH),
)


def _in_proj_cols(a):
    return jnp.concatenate([_pad_cols(a[:, src:src + width], slot) for src, width, slot in _IN_PROJ_SEGMENTS],
                           axis=1)


def _regroup_kernel(wt_ref, o_ref):
    dst = 0
    for src, width, slot in _IN_PROJ_SEGMENTS:
        o_ref[dst:dst + width, :] = wt_ref[src:src + width, :].astype(o_ref.dtype)
        if slot > width:
            o_ref[dst + width:dst + slot, :] = jnp.zeros((slot - width, o_ref.shape[1]), o_ref.dtype)
        dst += slot


def _regroup_w_in_t(w_t, *, tc=256):
    n_out, n_in = w_t.shape
    out_rows = sum(slot for _, _, slot in _IN_PROJ_SEGMENTS)
    return pl.pallas_call(
        _regroup_kernel,
        grid=(n_in // tc,),
        in_specs=[pl.BlockSpec((n_out, tc), lambda i: (0, i))],
        out_specs=pl.BlockSpec((out_rows, tc), lambda i: (0, i)),
        out_shape=jax.ShapeDtypeStruct((out_rows, n_in), BF16),
        compiler_params=pltpu.CompilerParams(dimension_semantics=("arbitrary",), vmem_limit_bytes=VMEM_LIMIT),
        name="regroup_w_in",
    )(w_t)


def kernel(x, ln_mix_g, w_in, b_attn_qkv, rwkv_shift_mu, rwkv_w0, rwkv_w2, rwkv_a0, rwkv_a2, rwkv_g2, rwkv_k_k, rwkv_k_a, rwkv_r_k, rwkv_lnx_g, rwkv_lnx_b, attn_sinks, attn_out_g, w_out, ln_ffn_g, ffn_w_up, ffn_w_gate, ffn_conv_w, ffn_conv_b, ffn_w_down, ln_final_g):
    B, T, _ = x.shape
    assert B == 1 and ln_mix_g.shape[0] == 1
    l = 0
    x2d = x.reshape(T, D_MODEL)
    row = lambda a: a.reshape(1, -1)

    w_all = _regroup_w_in_t(jnp.swapaxes(w_in[l], 0, 1))
    mu_shift = _in_proj_cols(row(rwkv_shift_mu[l]))
    (a_t, k_h, b_h, r_t, k_e, b_e, v_b, bonus, gate, p_c, attn_in, w_up, w_gate) = _in_proj(
        x2d, row(ln_mix_g[l]), w_all, mu_shift, row(b_attn_qkv[l]),
        row(rwkv_w0[l]), _pad_rows(rwkv_w2[l], LANES).astype(BF16),
        row(rwkv_a0[l]), _pad_rows(rwkv_a2[l], LANES).astype(BF16),
        _pad_rows(rwkv_g2[l], 2 * LANES).astype(BF16), row(rwkv_k_k[l]), row(rwkv_k_a[l]), row(rwkv_r_k[l]),
        _rope_lane_tables(T), side_casts=(ffn_w_up[l], ffn_w_gate[l]))

    y_rwkv, w_down, w_o = _rwkv(a_t, k_h, b_h, r_t, k_e, b_e, v_b, bonus, gate, p_c,
                                row(rwkv_lnx_g[l]), row(rwkv_lnx_b[l]), side_casts=(ffn_w_down[l], w_out[l]))

    y_attn = _attn(attn_in, attn_sinks[l], row(attn_out_g[l]))

    x1 = _out_proj(x2d, y_rwkv, y_attn, w_o)

    out = _ffn(x1, row(ln_ffn_g[l]), w_up, w_gate, ffn_conv_w[l], row(ffn_conv_b[l]), w_down, row(ln_final_g))
    return out.reshape(B, T, D_MODEL)
```

```python
import functools
import math

import jax
import jax.numpy as jnp
from jax import lax
from jax.experimental import pallas as pl
from jax.experimental.pallas import tpu as pltpu

F32 = jnp.float32
BF16 = jnp.bfloat16

D_MODEL = 2048
HEAD_DIM = 64
RWKV_WIDTH = 1024
ATTN_WIDTH = 1024
ATTN_HEADS = 16
KV_HEADS = 2
KV_WIDTH = KV_HEADS * HEAD_DIM
W_LORA = 64
A_LORA = 64
G_LORA = 160
WINDOW = 128
ROPE_THETA = 500000.0
ROT_DIM = 16
D_FF = 5632
NORM_EPS = 1e-5
LNX_EPS = 64e-5

LANES = 128
SUBLANES = 8
PREV_ROWS = 16
LORA_PAD = 512
GROUP = 4
GROUP_W = GROUP * HEAD_DIM
CHUNK = 64
SUB_CHUNKS = 4
SUB_BLOCK = SUB_CHUNKS * CHUNK
VMEM_LIMIT = 56 * 1024 * 1024
IN_PROJ_VMEM_LIMIT = 60 * 1024 * 1024
FFN_VMEM_LIMIT = 60 * 1024 * 1024

assert CHUNK == HEAD_DIM and SUB_CHUNKS <= SUBLANES


def _dot(a, b):
    return lax.dot_general(a, b, (((1,), (0,)), ((), ())), preferred_element_type=F32)


def _dot_nt(a, b):
    return lax.dot_general(a, b, (((1,), (1,)), ((), ())), preferred_element_type=F32)


def _dot_tn(a, b):
    return lax.dot_general(a, b, (((0,), (0,)), ((), ())), preferred_element_type=F32)


def _sigmoid(x):
    return 1.0 / (1.0 + jnp.exp(-x))


def _sigmoid_tanh(x):
    return 0.5 + 0.5 * jnp.tanh(0.5 * x)


def _rms(xv, g):
    ms = jnp.mean(xv * xv, axis=-1, keepdims=True)
    return xv * lax.rsqrt(ms + NORM_EPS) * g


def _resident(shape):
    nd = len(shape)
    return pl.BlockSpec(shape, lambda *_: (0,) * nd, pipeline_mode=pl.Buffered(1))


def _split_bf16(x):
    hi = x.astype(BF16)
    lo = (x - hi.astype(F32)).astype(BF16)
    return hi, lo


def _same_head_mask():
    r_i = lax.broadcasted_iota(jnp.int32, (GROUP_W, GROUP_W), 0) // HEAD_DIM
    c_i = lax.broadcasted_iota(jnp.int32, (GROUP_W, GROUP_W), 1) // HEAD_DIM
    return r_i == c_i


def _head_sums(x, ones_bd):
    xb = x.astype(BF16)
    return jnp.concatenate([_dot(xb[:, q * GROUP_W:(q + 1) * GROUP_W], ones_bd)
                            for q in range(RWKV_WIDTH // GROUP_W)], axis=1)


def _rope(x, cos_l, sin_a, sin_b):
    return x * cos_l + pltpu.roll(x, LANES - ROT_DIM // 2, axis=1) * sin_a + pltpu.roll(x, ROT_DIM // 2, axis=1) * sin_b


def _split_refs(refs, *counts):
    out, pos = [], 0
    for n in counts:
        out.append(tuple(refs[pos:pos + n]))
        pos += n
    assert pos == len(refs)
    return out


def _side_cast_specs(arrays, n_steps):
    assert all(a.shape[0] % (n_steps * PREV_ROWS) == 0 for a in arrays)
    specs = [pl.BlockSpec((a.shape[0] // n_steps, a.shape[1]), lambda i: (i, 0)) for a in arrays]
    shapes = [jax.ShapeDtypeStruct(a.shape, BF16) for a in arrays]
    return specs, shapes


def _side_casts(src_refs, dst_refs):
    for src, dst in zip(src_refs, dst_refs):
        dst[...] = src[...].astype(dst.dtype)


def _in_proj_kernel(x_ref, g_ref, w_ref, mu_ref, battn_ref, w0_ref, w2_ref, a0_ref, a2_ref, g2_ref,
                    kk_ref, ka_ref, rk_ref, cos_ref, sa_ref, sb_ref, *rest, col_chunk, n_cast):
    (cast_in, (at_ref, kh_ref, bh_ref, rt_ref, ke_ref, be_ref, vb_ref, bonus_ref, gate_ref, pc_ref, attn_ref),
     cast_out, (h_ref, carry_ref)) = _split_refs(rest, n_cast, 11, n_cast, 2)
    i = pl.program_id(0)
    tm = x_ref.shape[0]
    n_chunks = tm // CHUNK
    C = CHUNK
    _side_casts(cast_in, cast_out)
    h_ref[...] = _rms(x_ref[...], g_ref[...]).astype(BF16)
    hx = h_ref[...]
    first_rows = lax.broadcasted_iota(jnp.int32, (SUBLANES, col_chunk), 0) == 0

    def shifted(base, width):
        outs = []
        for c0 in range(base, base + width, col_chunk):
            z = _dot_nt(hx, w_ref[c0:c0 + col_chunk, :])
            prev_last = jnp.where(i > 0, carry_ref[0:1, c0:c0 + col_chunk], 0.0)
            zp = pltpu.roll(z, 1, axis=0)
            zp = jnp.concatenate([jnp.where(first_rows, prev_last, zp[:SUBLANES]), zp[SUBLANES:]], axis=0)
            carry_ref[0:1, c0:c0 + col_chunk] = z[tm - 1:tm, :]
            outs.append(z + mu_ref[:, c0:c0 + col_chunk] * (zp - z))
        return jnp.concatenate(outs, axis=1)

    def attn_cols(c0):
        base = LORA_PAD + 3 * RWKV_WIDTH
        c1 = min(c0 + col_chunk, attn_ref.shape[1])
        z = _dot_nt(hx, w_ref[base + c0:base + c1, :]) + battn_ref[:, c0:c1]
        slabs = []
        for col in range(c0, c1, LANES):
            zs = z[:, col - c0:col - c0 + LANES]
            if col < ATTN_WIDTH + KV_WIDTH:
                zs = _rope(zs, cos_ref[...], sa_ref[...], sb_ref[...])
            if col < ATTN_WIDTH:
                zs = zs * (HEAD_DIM ** -0.5)
            slabs.append(zs)
        attn_ref[:, c0:c1] = jnp.concatenate(slabs, axis=1)

    ones_bd = _same_head_mask().astype(BF16)
    t_i = lax.broadcasted_iota(jnp.int32, (tm, tm), 0)
    t_j = lax.broadcasted_iota(jnp.int32, (tm, tm), 1)
    tri = ((t_j <= t_i) & (t_j // C == t_i // C)).astype(BF16)
    attn_chunks = list(range(0, attn_ref.shape[1], col_chunk))

    lo = shifted(0, LORA_PAD)
    k = shifted(LORA_PAD, RWKV_WIDTH)
    xw = lo[:, 0:LANES]
    xa = lo[:, LANES:2 * LANES]
    xg = lo[:, 2 * LANES:LORA_PAD]
    u = w0_ref[...] + _dot(jnp.tanh(xw).astype(BF16), w2_ref[...])
    lw = -math.exp(-0.5) * _sigmoid_tanh(u)
    a = _sigmoid_tanh(a0_ref[...] + _dot(xa.astype(BF16), a2_ref[...]))
    gate_ref[...] = _dot(_sigmoid_tanh(xg).astype(BF16), g2_ref[...]).astype(gate_ref.dtype)
    lw_hi, lw_lo = _split_bf16(lw)
    cl = _dot(tri, lw_hi) + _dot(tri, lw_lo)
    last_rows = [cl[(c + 1) * C - 1:(c + 1) * C, :] for c in range(n_chunks)]
    cl_last = jnp.concatenate([jnp.broadcast_to(row, (C, RWKV_WIDTH)) for row in last_rows], axis=0)
    pc_ref[...] = jnp.concatenate([jnp.exp(row) for row in last_rows]
                                  + [jnp.zeros((SUBLANES - n_chunks, RWKV_WIDTH), F32)], axis=0)

    r = shifted(LORA_PAD + RWKV_WIDTH, RWKV_WIDTH)
    kkraw = k * kk_ref[...]
    k2 = k * (1.0 + (a - 1.0) * ka_ref[...])

    v = shifted(LORA_PAD + 2 * RWKV_WIDTH, RWKV_WIDTH)
    sums = _head_sums(jnp.concatenate([kkraw * kkraw, r * k2 * rk_ref[...]], axis=0), ones_bd)
    kk = kkraw / jnp.maximum(jnp.sqrt(sums[:tm]), 1e-12)
    b = kk * a

    attn_cols(attn_chunks[0])
    at_ref[...] = (-kk * jnp.exp(cl - lw)).astype(at_ref.dtype)
    e_neg = jnp.exp(-cl)
    kh_ref[...] = (k2 * e_neg).astype(kh_ref.dtype)
    bh_ref[...] = (b * e_neg).astype(bh_ref.dtype)

    for c0 in attn_chunks[1:2]:
        attn_cols(c0)
    rt_ref[...] = (r * jnp.exp(cl)).astype(rt_ref.dtype)
    e_end = jnp.exp(cl_last - cl)
    ke_ref[...] = (k2 * e_end).astype(ke_ref.dtype)
    be_ref[...] = (b * e_end).astype(be_ref.dtype)

    for c0 in attn_chunks[2:]:
        attn_cols(c0)
    bonus_ref[...] = (sums[tm:] * v).astype(bonus_ref.dtype)
    vb_ref[...] = v.astype(vb_ref.dtype)


def _in_proj(x2d, g, w_all, mu_shift, b_attn, w0, w2, a0, a2, g2, k_k, k_a, r_k, rope_tables, side_casts=(), *,
             tm=SUB_BLOCK, col_chunk=512):
    T = x2d.shape[0]
    n_attn = b_attn.shape[1]
    assert w_all.shape == (3 * RWKV_WIDTH + LORA_PAD + n_attn, D_MODEL) and tm % CHUNK == 0
    n_tiles = T // tm
    cast_specs, cast_shapes = _side_cast_specs(side_casts, n_tiles)
    out = lambda i: (i, 0)
    vec = _resident((1, RWKV_WIDTH))
    tile = pl.BlockSpec((tm, RWKV_WIDTH), out)
    tile_shape = jax.ShapeDtypeStruct((T, RWKV_WIDTH), BF16)
    n_shift = LORA_PAD + 3 * RWKV_WIDTH
    return pl.pallas_call(
        functools.partial(_in_proj_kernel, col_chunk=col_chunk, n_cast=len(side_casts)),
        grid=(n_tiles,),
        in_specs=[
            pl.BlockSpec((tm, D_MODEL), out),
            _resident((1, D_MODEL)),
            _resident(w_all.shape), _resident(mu_shift.shape), _resident(b_attn.shape),
            vec, _resident(w2.shape), vec, _resident(a2.shape), _resident(g2.shape), vec, vec, vec,
        ] + [pl.BlockSpec((tm, LANES), out)] * 3 + cast_specs,
        out_specs=[tile] * 9 + [
            pl.BlockSpec((None, SUBLANES, RWKV_WIDTH), lambda i: (i, 0, 0)),
            pl.BlockSpec((tm, n_attn), out),
        ] + cast_specs,
        out_shape=[tile_shape] * 9 + [
            jax.ShapeDtypeStruct((T // tm, SUBLANES, RWKV_WIDTH), F32),
            jax.ShapeDtypeStruct((T, n_attn), F32),
        ] + cast_shapes,
        scratch_shapes=[pltpu.VMEM((tm, D_MODEL), BF16), pltpu.VMEM((SUBLANES, n_shift), F32)],
        compiler_params=pltpu.CompilerParams(dimension_semantics=("arbitrary",),
                                             vmem_limit_bytes=IN_PROJ_VMEM_LIMIT),
        name="in_proj",
    )(x2d, g, w_all, mu_shift, b_attn, w0, w2, a0, a2, g2, k_k, k_a, r_k, *rope_tables, *side_casts)


def _block_diag(x):
    blk = lax.broadcasted_iota(jnp.int32, x.shape, 1) // HEAD_DIM
    zero = jnp.zeros_like(x)
    return jnp.concatenate([jnp.where(blk == h, x, zero) for h in range(GROUP)], axis=0)


def _rwkv_kernel(at_ref, kh_ref, bh_ref, rt_ref, ke_ref, be_ref, vb_ref, bonus_ref, gate_ref, pc_ref,
                 lng_ref, lnb_ref, *rest, n_sub, n_cast):
    cast_in, (y_ref,), cast_out, (s_ref,) = _split_refs(rest, n_cast, 1, n_cast, 1)
    C = CHUNK
    SB = SUB_BLOCK
    n_groups = RWKV_WIDTH // GROUP_W

    @pl.when(pl.program_id(0) == 0)
    def _():
        s_ref[...] = jnp.zeros_like(s_ref)

    _side_casts(cast_in, cast_out)
    row = lax.broadcasted_iota(jnp.int32, (C, GROUP_W), 0)
    col = lax.broadcasted_iota(jnp.int32, (C, GROUP_W), 1) % C
    strict = col < row
    incl = col <= row
    eye = (col == row).astype(F32)
    lane_head = lax.broadcasted_iota(jnp.int32, (HEAD_DIM, GROUP_W), 1) // HEAD_DIM
    ones_bd = _same_head_mask().astype(BF16)
    streams = [(c, q) for c in range(SUB_CHUNKS) for q in range(n_groups)]
    state = [s_ref[q] for q in range(n_groups)]
    subs = [dict() for _ in range(n_sub)]

    def blk(ref, h, s):
        c, q = s
        return ref[h * SB + c * C:h * SB + (c + 1) * C, q * GROUP_W:(q + 1) * GROUP_W]

    def local_pieces(h):
        H = subs[h]
        a_ak, a_rk, l_ab, pw = {}, {}, {}, {}
        H["a_rb"], H["t_inv"], H["av"] = {}, {}, {}
        a_rb, t_inv, av = H["a_rb"], H["t_inv"], H["av"]

        def scores():
            for s in streams:
                ar = jnp.concatenate([blk(at_ref, h, s), blk(rt_ref, h, s)], axis=0)
                ak = _dot_nt(ar, _block_diag(blk(kh_ref, h, s)))
                ab = _dot_nt(ar, _block_diag(blk(bh_ref, h, s)))
                a_ak[s] = jnp.where(strict, ak[:C], 0.0)
                a_rk[s] = jnp.where(incl, ak[C:], 0.0)
                l_ab[s] = jnp.where(strict, ab[:C], 0.0)
                a_rb[s] = jnp.where(incl, ab[C:], 0.0).astype(BF16)

        def inv_first():
            for s in streams:
                t_inv[s] = eye + l_ab[s]
                lb = l_ab[s].astype(BF16)
                pw[s] = _dot(lb, _block_diag(lb))

        def inv_level():
            for s in streams:
                tp = _dot(jnp.concatenate([t_inv[s], pw[s]], axis=0).astype(BF16),
                          _block_diag(pw[s].astype(BF16)))
                t_inv[s] = t_inv[s] + tp[:C]
                pw[s] = tp[C:]

        def inv_last():
            for s in streams:
                t_inv[s] = (t_inv[s] + _dot(t_inv[s].astype(BF16), _block_diag(pw[s].astype(BF16)))).astype(BF16)

        def values():
            for s in streams:
                av[s] = _dot(jnp.concatenate([a_ak[s], a_rk[s]], axis=0).astype(BF16),
                             _block_diag(blk(vb_ref, h, s)))

        return [scores, inv_first] + [inv_level] * (int(math.log2(C)) - 2) + [inv_last, values]

    def chain_pieces(h):
        H = subs[h]
        xs, u_b, ys = {}, {}, {}

        def read_state(c):
            def f():
                for q in range(n_groups):
                    s = (c, q)
                    ar = jnp.concatenate([blk(at_ref, h, s), blk(rt_ref, h, s)], axis=0)
                    xs[s] = _dot_nt(ar, _block_diag(state[q].astype(BF16)))
            return f

        def solve(c):
            def f():
                for q in range(n_groups):
                    s = (c, q)
                    x_u = (xs[s][:C] + H["av"][s][:C]).astype(BF16)
                    u_b[s] = _dot(H["t_inv"][s], _block_diag(x_u)).astype(BF16)
            return f

        def update(c):
            def f():
                for q in range(n_groups):
                    s = (c, q)
                    ys[s] = xs[s][C:] + H["av"][s][C:] + _dot(H["a_rb"][s], _block_diag(u_b[s]))
                    vu = jnp.concatenate([blk(vb_ref, h, s), u_b[s]], axis=0)
                    kb = jnp.concatenate([blk(ke_ref, h, s), blk(be_ref, h, s)], axis=0)
                    upd = _dot_tn(vu, kb)
                    upd_c = sum(jnp.where(lane_head == hd, upd[hd * HEAD_DIM:(hd + 1) * HEAD_DIM, :], 0.0)
                                for hd in range(GROUP))
                    p_c = pc_ref[h, c:c + 1, q * GROUP_W:(q + 1) * GROUP_W]
                    state[q] = state[q] * p_c + upd_c
                if c == SUB_CHUNKS - 1:
                    H["y"] = jnp.concatenate([jnp.concatenate([ys[(cc, q)] for q in range(n_groups)], axis=1)
                                              for cc in range(SUB_CHUNKS)], axis=0)
            return f

        return [p for c in range(SUB_CHUNKS) for p in (read_state(c), solve(c), update(c))]

    def epilogue_pieces(h):
        H = subs[h]
        rows = slice(h * SB, (h + 1) * SB)

        def center():
            mu = _head_sums(H["y"], ones_bd) * (1.0 / HEAD_DIM)
            H["d"] = H["y"] - mu

        def scale():
            d = H["d"]
            var = _head_sums(d * d, ones_bd) * (1.0 / HEAD_DIM)
            yn = d * lax.rsqrt(var + LNX_EPS) * lng_ref[...] + lnb_ref[...]
            y_ref[rows, :] = ((yn + bonus_ref[rows, :].astype(F32)) * gate_ref[rows, :].astype(F32)).astype(y_ref.dtype)

        return [center, scale]

    def run_merged(*lists):
        lists = [pieces for pieces in lists if pieces]
        if not lists:
            return
        main = max(lists, key=len)
        others = [pieces for pieces in lists if pieces is not main]
        n = len(main)
        for i, piece in enumerate(main):
            piece()
            for other in others:
                for j, o in enumerate(other):
                    if j * n // len(other) == i:
                        o()

    for h in range(n_sub + 2):
        run_merged(chain_pieces(h - 1) if 1 <= h <= n_sub else [],
                   local_pieces(h) if h < n_sub else [],
                   epilogue_pieces(h - 2) if h >= 2 else [])
    for q in range(n_groups):
        s_ref[q] = state[q]


def _rwkv(a_t, k_h, b_h, r_t, k_e, b_e, v_b, bonus, gate, p_c, ln_g, ln_b, side_casts=(), *, tb=2 * SUB_BLOCK):
    T = a_t.shape[0]
    n_sub = tb // SUB_BLOCK
    assert p_c.shape == (T // SUB_BLOCK, SUBLANES, RWKV_WIDTH)
    vec = _resident((1, RWKV_WIDTH))
    tile = pl.BlockSpec((tb, RWKV_WIDTH), lambda i: (i, 0))
    cast_specs, cast_shapes = _side_cast_specs(side_casts, T // tb)
    return pl.pallas_call(
        functools.partial(_rwkv_kernel, n_sub=n_sub, n_cast=len(side_casts)),
        grid=(T // tb,),
        in_specs=[tile] * 9 + [pl.BlockSpec((n_sub, SUBLANES, RWKV_WIDTH), lambda i: (i, 0, 0)), vec, vec]
        + cast_specs,
        out_specs=[tile] + cast_specs,
        out_shape=[jax.ShapeDtypeStruct((T, RWKV_WIDTH), BF16)] + cast_shapes,
        scratch_shapes=[pltpu.VMEM((RWKV_WIDTH // GROUP_W, HEAD_DIM, GROUP_W), F32)],
        compiler_params=pltpu.CompilerParams(dimension_semantics=("arbitrary",),
                                             vmem_limit_bytes=VMEM_LIMIT),
        name="rwkv7",
    )(a_t, k_h, b_h, r_t, k_e, b_e, v_b, bonus, gate, p_c, ln_g, ln_b, *side_casts)


def _attn_kernel(sink_ref, q_ref, kv_ref, kvp_ref, g_ref, o_ref, *, n_blocks):
    n = pl.program_id(0)
    B = WINDOW
    key0 = lax.broadcasted_iota(jnp.int32, (2 * B, LANES), 0) == 0
    low = lax.broadcasted_iota(jnp.int32, (2 * B, LANES), 1) < HEAD_DIM
    qi = lax.broadcasted_iota(jnp.int32, (2 * B, B), 0) % B
    si = lax.broadcasted_iota(jnp.int32, (2 * B, B), 1)
    allowed_cur = si <= qi
    sink_col = si == 0
    top = lax.broadcasted_iota(jnp.int32, (2 * B, 1), 0) < B
    qlow = lax.broadcasted_iota(jnp.int32, (B, LANES), 1) < HEAD_DIM
    pairs_per_kv = (ATTN_HEADS // KV_HEADS) // 2

    k_dup, v_top, v_bot, allowed_prev = [], [], [], []
    for b in range(n_blocks):
        kv = kv_ref[b * B:(b + 1) * B, :]
        kvp = kvp_ref[...] if b == 0 else kv_ref[(b - 1) * B:b * B, :]
        k_win = jnp.concatenate([kvp[:, :LANES], kv[:, :LANES]], axis=0)
        v_win = jnp.where(key0, 0.0, jnp.concatenate([kvp[:, LANES:], kv[:, LANES:]], axis=0))
        k_sw = pltpu.roll(k_win, HEAD_DIM, axis=1)
        v_sw = pltpu.roll(v_win, HEAD_DIM, axis=1)
        k_dup.append([jnp.where(low, k_win, k_sw).astype(BF16), jnp.where(low, k_sw, k_win).astype(BF16)])
        v_top.append([jnp.where(low, v_win, 1.0).astype(BF16), jnp.where(low, v_sw, 1.0).astype(BF16)])
        v_bot.append([jnp.where(low, 1.0, v_sw).astype(BF16), jnp.where(low, 1.0, v_win).astype(BF16)])
        allowed_prev.append((si > qi) & (n > 0) if b == 0 else si > qi)

    work = [(b, p) for b in range(n_blocks) for p in range(ATTN_HEADS // 2)]
    scores, es, outs = {}, {}, {}
    for b, p in work:
        q = q_ref[b * B:(b + 1) * B, p * LANES:(p + 1) * LANES]
        zero = jnp.zeros_like(q)
        q_st = jnp.concatenate([jnp.where(qlow, q, zero), jnp.where(qlow, zero, q)], axis=0).astype(BF16)
        scores[b, p] = _dot_nt(q_st, k_dup[b][p // pairs_per_kv])
    for b, p in work:
        sink = jnp.where(top, sink_ref[2 * p], sink_ref[2 * p + 1])
        sc = scores[b, p]
        s = jnp.concatenate([jnp.where(allowed_prev[b], sc[:, :B], jnp.where(sink_col, sink, -1e30)),
                             jnp.where(allowed_cur, sc[:, B:], -1e30)], axis=1)
        m = jnp.max(s, axis=-1, keepdims=True)
        es[b, p] = jnp.exp((s - m).astype(BF16))
    for b, p in work:
        kvh = p // pairs_per_kv
        o_top = _dot(es[b, p][:B], v_top[b][kvh])
        o_bot = _dot(es[b, p][B:], v_bot[b][kvh])
        sums = pltpu.roll(jnp.where(qlow, o_bot, o_top), HEAD_DIM, axis=1)
        outs[b, p] = jnp.where(qlow, o_top, o_bot) / sums
    g = g_ref[...]
    for b in range(n_blocks):
        o = jnp.concatenate([outs[b, p] for p in range(ATTN_HEADS // 2)], axis=1)
        o_ref[b * B:(b + 1) * B, :] = _rms(o, g).astype(o_ref.dtype)


def _attn(attn_in, sinks, g, *, n_blocks=4):
    T = attn_in.shape[0]
    B = WINDOW
    tq = n_blocks * B
    kv_blk = ATTN_WIDTH // (2 * KV_WIDTH)
    return pl.pallas_call(
        functools.partial(_attn_kernel, n_blocks=n_blocks),
        grid=(T // tq,),
        in_specs=[
            pl.BlockSpec(memory_space=pltpu.SMEM),
            pl.BlockSpec((tq, ATTN_WIDTH), lambda n: (n, 0)),
            pl.BlockSpec((tq, 2 * KV_WIDTH), lambda n: (n, kv_blk)),
            pl.BlockSpec((B, 2 * KV_WIDTH), lambda n: (jnp.maximum(n * n_blocks - 1, 0), kv_blk)),
            _resident((1, ATTN_WIDTH)),
        ],
        out_specs=pl.BlockSpec((tq, ATTN_WIDTH), lambda n: (n, 0)),
        out_shape=jax.ShapeDtypeStruct((T, ATTN_WIDTH), BF16),
        compiler_params=pltpu.CompilerParams(dimension_semantics=("arbitrary",),
                                             vmem_limit_bytes=VMEM_LIMIT),
        name="swa_attn",
    )(sinks, attn_in, attn_in, attn_in, g)


def _out_proj_kernel(x_ref, yr_ref, ya_ref, w_ref, o_ref):
    o_ref[...] = (x_ref[...] + _dot(yr_ref[...], w_ref[:RWKV_WIDTH, :])
                  + _dot(ya_ref[...], w_ref[RWKV_WIDTH:, :]))


def _out_proj(x2d, y_rwkv, y_attn, w, *, tm=512):
    T = x2d.shape[0]
    row = lambda width: pl.BlockSpec((tm, width), lambda i: (i, 0))
    return pl.pallas_call(
        _out_proj_kernel,
        grid=(T // tm,),
        in_specs=[row(D_MODEL), row(RWKV_WIDTH), row(ATTN_WIDTH), _resident(w.shape)],
        out_specs=row(D_MODEL),
        out_shape=jax.ShapeDtypeStruct((T, D_MODEL), F32),
        compiler_params=pltpu.CompilerParams(dimension_semantics=("arbitrary",),
                                             vmem_limit_bytes=VMEM_LIMIT),
        name="out_proj",
    )(x2d, y_rwkv, y_attn, w)


def _ffn_kernel(x_ref, xp_ref, g_ref, wup_ref, wgate_ref, cw_ref, cb_ref, wdown_ref, gf_ref, o_ref, hext_ref):
    i = pl.program_id(0)
    f = pl.program_id(1)

    @pl.when(f == 0)
    def _():
        g = g_ref[...]
        x = x_ref[...]
        hext_ref[PREV_ROWS:, :] = _rms(x, g).astype(BF16)
        hp = jnp.where(i > 0, _rms(xp_ref[...], g), 0.0)
        hext_ref[:PREV_ROWS, :] = hp.astype(BF16)
        o_ref[...] = x

    hx = hext_ref[...]
    u = _dot(hx, wup_ref[...])
    cw = cw_ref[...]
    conv = cb_ref[...] + cw[2:3, :] * u + cw[1:2, :] * pltpu.roll(u, 1, axis=0) + cw[0:1, :] * pltpu.roll(u, 2, axis=0)
    conv = conv[PREV_ROWS:, :]
    gate = _dot(hext_ref[PREV_ROWS:, :], wgate_ref[...])
    act = conv * _sigmoid(conv) * gate
    o_ref[...] += _dot(act.astype(BF16), wdown_ref[...])

    @pl.when(f == pl.num_programs(1) - 1)
    def _():
        o_ref[...] = _rms(o_ref[...], gf_ref[...])


def _ffn(x1, g, w_up, w_gate, conv_w, conv_b, w_down, g_final, *, tm=1024, tf=512):
    T = x1.shape[0]
    prev_blocks = tm // PREV_ROWS
    return pl.pallas_call(
        _ffn_kernel,
        grid=(T // tm, D_FF // tf),
        in_specs=[
            pl.BlockSpec((tm, D_MODEL), lambda i, f: (i, 0)),
            pl.BlockSpec((PREV_ROWS, D_MODEL), lambda i, f: (jnp.maximum(i * prev_blocks - 1, 0), 0)),
            pl.BlockSpec((1, D_MODEL), lambda i, f: (0, 0)),
            pl.BlockSpec((D_MODEL, tf), lambda i, f: (0, f)),
            pl.BlockSpec((D_MODEL, tf), lambda i, f: (0, f)),
            pl.BlockSpec((3, tf), lambda i, f: (0, f)),
            pl.BlockSpec((1, tf), lambda i, f: (0, f)),
            pl.BlockSpec((tf, D_MODEL), lambda i, f: (f, 0)),
            pl.BlockSpec((1, D_MODEL), lambda i, f: (0, 0)),
        ],
        out_specs=pl.BlockSpec((tm, D_MODEL), lambda i, f: (i, 0)),
        out_shape=jax.ShapeDtypeStruct((T, D_MODEL), F32),
        scratch_shapes=[pltpu.VMEM((tm + PREV_ROWS, D_MODEL), BF16)],
        compiler_params=pltpu.CompilerParams(dimension_semantics=("arbitrary", "arbitrary"),
                                             vmem_limit_bytes=FFN_VMEM_LIMIT),
        name="ffn",
    )(x1, x1, g, w_up, w_gate, conv_w, conv_b, w_down, g_final)


def _pad_cols(a, width):
    return jnp.pad(a, ((0, 0), (0, width - a.shape[1])))


def _pad_rows(a, rows):
    return jnp.pad(a, ((0, rows - a.shape[0]), (0, 0)))


def _rope_lane_tables(T):
    half = ROT_DIM // 2
    inv_freq = ROPE_THETA ** (-jnp.arange(0, ROT_DIM, 2, dtype=F32) / ROT_DIM)
    ang = jnp.arange(T, dtype=F32)[:, None] * inv_freq[None, :]
    cos, sin = jnp.cos(ang), jnp.sin(ang)
    ones = jnp.ones((T, HEAD_DIM - ROT_DIM), F32)
    zeros_h = jnp.zeros((T, half), F32)
    zeros_r = jnp.zeros((T, HEAD_DIM - ROT_DIM), F32)
    cos_h = jnp.concatenate([cos, cos, ones], axis=1)
    sa_h = jnp.concatenate([-sin, zeros_h, zeros_r], axis=1)
    sb_h = jnp.concatenate([zeros_h, sin, zeros_r], axis=1)
    rep = LANES // HEAD_DIM
    return jnp.tile(cos_h, (1, rep)), jnp.tile(sa_h, (1, rep)), jnp.tile(sb_h, (1, rep))


_IN_PROJ_SEGMENTS = (
    (3 * RWKV_WIDTH, W_LORA, LANES),
    (3 * RWKV_WIDTH + W_LORA, A_LORA, LANES),
    (3 * RWKV_WIDTH + W_LORA + A_LORA, G_LORA, 2 * LANES),
    (RWKV_WIDTH, RWKV_WIDTH, RWKV_WIDTH),
    (0, RWKV_WIDTH, RWKV_WIDTH),
    (2 * RWKV_WIDTH, RWKV_WIDTH, RWKV_WIDTH),
    (3 * RWKV_WIDTH + W_LORA + A_LORA + G_LORA, ATTN_WIDTH + 2 * KV_WIDTH, ATTN_WIDTH + 2 * KV_WIDTH),
)


def _in_proj_cols(a):
    return jnp.concatenate([_pad_cols(a[:, src:src + width], slot) for src, width, slot in _IN_PROJ_SEGMENTS],
                           axis=1)


def _regroup_kernel(wt_ref, o_ref):
    dst = 0
    for src, width, slot in _IN_PROJ_SEGMENTS:
        o_ref[dst:dst + width, :] = wt_ref[src:src + width, :].astype(o_ref.dtype)
        if slot > width:
            o_ref[dst + width:dst + slot, :] = jnp.zeros((slot - width, o_ref.shape[1]), o_ref.dtype)
        dst += slot


def _regroup_w_in_t(w_t, *, tc=256):
    n_out, n_in = w_t.shape
    out_rows = sum(slot for _, _, slot in _IN_PROJ_SEGMENTS)
    return pl.pallas_call(
        _regroup_kernel,
        grid=(n_in // tc,),
        in_specs=[pl.BlockSpec((n_out, tc), lambda i: (0, i))],
        out_specs=pl.BlockSpec((out_rows, tc), lambda i: (0, i)),
        out_shape=jax.ShapeDtypeStruct((out_rows, n_in), BF16),
        compiler_params=pltpu.CompilerParams(dimension_semantics=("arbitrary",), vmem_limit_bytes=VMEM_LIMIT),
        name="regroup_w_in",
    )(w_t)


def kernel(x, ln_mix_g, w_in, b_attn_qkv, rwkv_shift_mu, rwkv_w0, rwkv_w2, rwkv_a0, rwkv_a2, rwkv_g2, rwkv_k_k, rwkv_k_a, rwkv_r_k, rwkv_lnx_g, rwkv_lnx_b, attn_sinks, attn_out_g, w_out, ln_ffn_g, ffn_w_up, ffn_w_gate, ffn_conv_w, ffn_conv_b, ffn_w_down, ln_final_g):
    B, T, _ = x.shape
    assert B == 1 and ln_mix_g.shape[0] == 1
    l = 0
    x2d = x.reshape(T, D_MODEL)
    row = lambda a: a.reshape(1, -1)

    w_all = _regroup_w_in_t(jnp.swapaxes(w_in[l], 0, 1))
    mu_shift = _in_proj_cols(row(rwkv_shift_mu[l]))
    (a_t, k_h, b_h, r_t, k_e, b_e, v_b, bonus, gate, p_c, attn_in, w_up, w_gate) = _in_proj(
        x2d, row(ln_mix_g[l]), w_all, mu_shift, row(b_attn_qkv[l]),
        row(rwkv_w0[l]), _pad_rows(rwkv_w2[l], LANES).astype(BF16),
        row(rwkv_a0[l]), _pad_rows(rwkv_a2[l], LANES).astype(BF16),
        _pad_rows(rwkv_g2[l], 2 * LANES).astype(BF16), row(rwkv_k_k[l]), row(rwkv_k_a[l]), row(rwkv_r_k[l]),
        _rope_lane_tables(T), side_casts=(ffn_w_up[l], ffn_w_gate[l]))

    y_rwkv, w_down, w_o = _rwkv(a_t, k_h, b_h, r_t, k_e, b_e, v_b, bonus, gate, p_c,
                                row(rwkv_lnx_g[l]), row(rwkv_lnx_b[l]), side_casts=(ffn_w_down[l], w_out[l]))

    y_attn = _attn(attn_in, attn_sinks[l], row(attn_out_g[l]))

    x1 = _out_proj(x2d, y_rwkv, y_attn, w_o)

    out = _ffn(x1, row(ln_ffn_g[l]), w_up, w_gate, ffn_conv_w[l], row(ffn_conv_b[l]), w_down, row(ln_final_g))
    return out.reshape(B, T, D_MODEL)
```

```python
import functools
import math

import jax
import jax.numpy as jnp
from jax import lax
from jax.experimental import pallas as pl
from jax.experimental.pallas import tpu as pltpu

F32 = jnp.float32
BF16 = jnp.bfloat16

D_MODEL = 2048
HEAD_DIM = 64
RWKV_WIDTH = 1024
ATTN_WIDTH = 1024
ATTN_HEADS = 16
KV_HEADS = 2
KV_WIDTH = KV_HEADS * HEAD_DIM
W_LORA = 64
A_LORA = 64
G_LORA = 160
WINDOW = 128
ROPE_THETA = 500000.0
ROT_DIM = 16
D_FF = 5632
NORM_EPS = 1e-5
LNX_EPS = 64e-5

LANES = 128
SUBLANES = 8
PREV_ROWS = 16
LORA_PAD = 512
GROUP = 4
GROUP_W = GROUP * HEAD_DIM
CHUNK = 64
SUB_CHUNKS = 4
SUB_BLOCK = SUB_CHUNKS * CHUNK
VMEM_LIMIT = 56 * 1024 * 1024
IN_PROJ_VMEM_LIMIT = 60 * 1024 * 1024
FFN_VMEM_LIMIT = 60 * 1024 * 1024

assert CHUNK == HEAD_DIM and SUB_CHUNKS <= SUBLANES


def _dot(a, b):
    return lax.dot_general(a, b, (((1,), (0,)), ((), ())), preferred_element_type=F32)


def _dot_nt(a, b):
    return lax.dot_general(a, b, (((1,), (1,)), ((), ())), preferred_element_type=F32)


def _dot_tn(a, b):
    return lax.dot_general(a, b, (((0,), (0,)), ((), ())), preferred_element_type=F32)


def _sigmoid(x):
    return 1.0 / (1.0 + jnp.exp(-x))


def _sigmoid_tanh(x):
    return 0.5 + 0.5 * jnp.tanh(0.5 * x)


def _rms(xv, g):
    ms = jnp.mean(xv * xv, axis=-1, keepdims=True)
    return xv * lax.rsqrt(ms + NORM_EPS) * g


def _resident(shape):
    nd = len(shape)
    return pl.BlockSpec(shape, lambda *_: (0,) * nd, pipeline_mode=pl.Buffered(1))


def _split_bf16(x):
    hi = x.astype(BF16)
    lo = (x - hi.astype(F32)).astype(BF16)
    return hi, lo


def _same_head_mask():
    r_i = lax.broadcasted_iota(jnp.int32, (GROUP_W, GROUP_W), 0) // HEAD_DIM
    c_i = lax.broadcasted_iota(jnp.int32, (GROUP_W, GROUP_W), 1) // HEAD_DIM
    return r_i == c_i


def _head_sums(x, ones_bd):
    xb = x.astype(BF16)
    return jnp.concatenate([_dot(xb[:, q * GROUP_W:(q + 1) * GROUP_W], ones_bd)
                            for q in range(RWKV_WIDTH // GROUP_W)], axis=1)


def _rope(x, cos_l, sin_a, sin_b):
    return x * cos_l + pltpu.roll(x, LANES - ROT_DIM // 2, axis=1) * sin_a + pltpu.roll(x, ROT_DIM // 2, axis=1) * sin_b


def _split_refs(refs, *counts):
    out, pos = [], 0
    for n in counts:
        out.append(tuple(refs[pos:pos + n]))
        pos += n
    assert pos == len(refs)
    return out


def _side_cast_specs(arrays, n_steps):
    assert all(a.shape[0] % (n_steps * PREV_ROWS) == 0 for a in arrays)
    specs = [pl.BlockSpec((a.shape[0] // n_steps, a.shape[1]), lambda i: (i, 0)) for a in arrays]
    shapes = [jax.ShapeDtypeStruct(a.shape, BF16) for a in arrays]
    return specs, shapes


def _side_casts(src_refs, dst_refs):
    for src, dst in zip(src_refs, dst_refs):
        dst[...] = src[...].astype(dst.dtype)


def _in_proj_kernel(x_ref, g_ref, w_ref, mu_ref, battn_ref, w0_ref, w2_ref, a0_ref, a2_ref, g2_ref,
                    kk_ref, ka_ref, rk_ref, cos_ref, sa_ref, sb_ref, *rest, col_chunk, n_cast):
    (cast_in, (at_ref, kh_ref, bh_ref, rt_ref, ke_ref, be_ref, vb_ref, bonus_ref, gate_ref, pc_ref, attn_ref),
     cast_out, (h_ref, carry_ref)) = _split_refs(rest, n_cast, 11, n_cast, 2)
    i = pl.program_id(0)
    tm = x_ref.shape[0]
    n_chunks = tm // CHUNK
    C = CHUNK
    _side_casts(cast_in, cast_out)
    h_ref[...] = _rms(x_ref[...], g_ref[...]).astype(BF16)
    hx = h_ref[...]
    first_rows = lax.broadcasted_iota(jnp.int32, (SUBLANES, col_chunk), 0) == 0

    def shifted(base, width):
        outs = []
        for c0 in range(base, base + width, col_chunk):
            z = _dot_nt(hx, w_ref[c0:c0 + col_chunk, :])
            prev_last = jnp.where(i > 0, carry_ref[0:1, c0:c0 + col_chunk], 0.0)
            zp = pltpu.roll(z, 1, axis=0)
            zp = jnp.concatenate([jnp.where(first_rows, prev_last, zp[:SUBLANES]), zp[SUBLANES:]], axis=0)
            carry_ref[0:1, c0:c0 + col_chunk] = z[tm - 1:tm, :]
            outs.append(z + mu_ref[:, c0:c0 + col_chunk] * (zp - z))
        return jnp.concatenate(outs, axis=1)

    def attn_cols(c0):
        base = LORA_PAD + 3 * RWKV_WIDTH
        c1 = min(c0 + col_chunk, attn_ref.shape[1])
        z = _dot_nt(hx, w_ref[base + c0:base + c1, :]) + battn_ref[:, c0:c1]
        slabs = []
        for col in range(c0, c1, LANES):
            zs = z[:, col - c0:col - c0 + LANES]
            if col < ATTN_WIDTH + KV_WIDTH:
                zs = _rope(zs, cos_ref[...], sa_ref[...], sb_ref[...])
            if col < ATTN_WIDTH:
                zs = zs * (HEAD_DIM ** -0.5)
            slabs.append(zs)
        attn_ref[:, c0:c1] = jnp.concatenate(slabs, axis=1)

    ones_bd = _same_head_mask().astype(BF16)
    t_i = lax.broadcasted_iota(jnp.int32, (tm, tm), 0)
    t_j = lax.broadcasted_iota(jnp.int32, (tm, tm), 1)
    tri = ((t_j <= t_i) & (t_j // C == t_i // C)).astype(BF16)
    attn_chunks = list(range(0, attn_ref.shape[1], col_chunk))

    lo = shifted(0, LORA_PAD)
    k = shifted(LORA_PAD, RWKV_WIDTH)
    xw = lo[:, 0:LANES]
    xa = lo[:, LANES:2 * LANES]
    xg = lo[:, 2 * LANES:LORA_PAD]
    u = w0_ref[...] + _dot(jnp.tanh(xw).astype(BF16), w2_ref[...])
    lw = -math.exp(-0.5) * _sigmoid_tanh(u)
    a = _sigmoid_tanh(a0_ref[...] + _dot(xa.astype(BF16), a2_ref[...]))
    gate_ref[...] = _dot(_sigmoid_tanh(xg).astype(BF16), g2_ref[...]).astype(gate_ref.dtype)
    lw_hi, lw_lo = _split_bf16(lw)
    cl = _dot(tri, lw_hi) + _dot(tri, lw_lo)
    last_rows = [cl[(c + 1) * C - 1:(c + 1) * C, :] for c in range(n_chunks)]
    cl_last = jnp.concatenate([jnp.broadcast_to(row, (C, RWKV_WIDTH)) for row in last_rows], axis=0)
    pc_ref[...] = jnp.concatenate([jnp.exp(row) for row in last_rows]
                                  + [jnp.zeros((SUBLANES - n_chunks, RWKV_WIDTH), F32)], axis=0)

    r = shifted(LORA_PAD + RWKV_WIDTH, RWKV_WIDTH)
    kkraw = k * kk_ref[...]
    k2 = k * (1.0 + (a - 1.0) * ka_ref[...])

    v = shifted(LORA_PAD + 2 * RWKV_WIDTH, RWKV_WIDTH)
    sums = _head_sums(jnp.concatenate([kkraw * kkraw, r * k2 * rk_ref[...]], axis=0), ones_bd)
    kk = kkraw / jnp.maximum(jnp.sqrt(sums[:tm]), 1e-12)
    b = kk * a

    attn_cols(attn_chunks[0])
    at_ref[...] = (-kk * jnp.exp(cl - lw)).astype(at_ref.dtype)
    e_neg = jnp.exp(-cl)
    kh_ref[...] = (k2 * e_neg).astype(kh_ref.dtype)
    bh_ref[...] = (b * e_neg).astype(bh_ref.dtype)

    for c0 in attn_chunks[1:2]:
        attn_cols(c0)
    rt_ref[...] = (r * jnp.exp(cl)).astype(rt_ref.dtype)
    e_end = jnp.exp(cl_last - cl)
    ke_ref[...] = (k2 * e_end).astype(ke_ref.dtype)
    be_ref[...] = (b * e_end).astype(be_ref.dtype)

    for c0 in attn_chunks[2:]:
        attn_cols(c0)
    bonus_ref[...] = (sums[tm:] * v).astype(bonus_ref.dtype)
    vb_ref[...] = v.astype(vb_ref.dtype)


def _in_proj(x2d, g, w_all, mu_shift, b_attn, w0, w2, a0, a2, g2, k_k, k_a, r_k, rope_tables, side_casts=(), *,
             tm=SUB_BLOCK, col_chunk=512):
    T = x2d.shape[0]
    n_attn = b_attn.shape[1]
    assert w_all.shape == (3 * RWKV_WIDTH + LORA_PAD + n_attn, D_MODEL) and tm % CHUNK == 0
    n_tiles = T // tm
    cast_specs, cast_shapes = _side_cast_specs(side_casts, n_tiles)
    out = lambda i: (i, 0)
    vec = _resident((1, RWKV_WIDTH))
    tile = pl.BlockSpec((tm, RWKV_WIDTH), out)
    tile_shape = jax.ShapeDtypeStruct((T, RWKV_WIDTH), BF16)
    n_shift = LORA_PAD + 3 * RWKV_WIDTH
    return pl.pallas_call(
        functools.partial(_in_proj_kernel, col_chunk=col_chunk, n_cast=len(side_casts)),
        grid=(n_tiles,),
        in_specs=[
            pl.BlockSpec((tm, D_MODEL), out),
            _resident((1, D_MODEL)),
            _resident(w_all.shape), _resident(mu_shift.shape), _resident(b_attn.shape),
            vec, _resident(w2.shape), vec, _resident(a2.shape), _resident(g2.shape), vec, vec, vec,
        ] + [pl.BlockSpec((tm, LANES), out)] * 3 + cast_specs,
        out_specs=[tile] * 9 + [
            pl.BlockSpec((None, SUBLANES, RWKV_WIDTH), lambda i: (i, 0, 0)),
            pl.BlockSpec((tm, n_attn), out),
        ] + cast_specs,
        out_shape=[tile_shape] * 9 + [
            jax.ShapeDtypeStruct((T // tm, SUBLANES, RWKV_WIDTH), F32),
            jax.ShapeDtypeStruct((T, n_attn), F32),
        ] + cast_shapes,
        scratch_shapes=[pltpu.VMEM((tm, D_MODEL), BF16), pltpu.VMEM((SUBLANES, n_shift), F32)],
        compiler_params=pltpu.CompilerParams(dimension_semantics=("arbitrary",),
                                             vmem_limit_bytes=IN_PROJ_VMEM_LIMIT),
        name="in_proj",
    )(x2d, g, w_all, mu_shift, b_attn, w0, w2, a0, a2, g2, k_k, k_a, r_k, *rope_tables, *side_casts)


def _block_diag(x):
    blk = lax.broadcasted_iota(jnp.int32, x.shape, 1) // HEAD_DIM
    zero = jnp.zeros_like(x)
    return jnp.concatenate([jnp.where(blk == h, x, zero) for h in range(GROUP)], axis=0)


def _rwkv_kernel(at_ref, kh_ref, bh_ref, rt_ref, ke_ref, be_ref, vb_ref, bonus_ref, gate_ref, pc_ref,
                 lng_ref, lnb_ref, *rest, n_sub, sub_chunks, n_cast):
    cast_in, (y_ref,), cast_out, (s_ref,) = _split_refs(rest, n_cast, 1, n_cast, 1)
    C = CHUNK
    SB = sub_chunks * C
    n_groups = RWKV_WIDTH // GROUP_W

    @pl.when(pl.program_id(0) == 0)
    def _():
        s_ref[...] = jnp.zeros_like(s_ref)

    _side_casts(cast_in, cast_out)
    row = lax.broadcasted_iota(jnp.int32, (C, GROUP_W), 0)
    col = lax.broadcasted_iota(jnp.int32, (C, GROUP_W), 1) % C
    strict = col < row
    incl = col <= row
    eye = (col == row).astype(F32)
    lane_head = lax.broadcasted_iota(jnp.int32, (HEAD_DIM, GROUP_W), 1) // HEAD_DIM
    ones_bd = _same_head_mask().astype(BF16)
    streams = [(c, q) for c in range(sub_chunks) for q in range(n_groups)]
    state = [s_ref[q] for q in range(n_groups)]
    subs = [dict() for _ in range(n_sub)]

    def blk(ref, h, s):
        c, q = s
        return ref[h * SB + c * C:h * SB + (c + 1) * C, q * GROUP_W:(q + 1) * GROUP_W]

    def local_pieces(h):
        H = subs[h]
        a_ak, a_rk, l_ab, pw = {}, {}, {}, {}
        H["a_rb"], H["t_inv"], H["av"] = {}, {}, {}
        a_rb, t_inv, av = H["a_rb"], H["t_inv"], H["av"]

        def scores():
            for s in streams:
                ar = jnp.concatenate([blk(at_ref, h, s), blk(rt_ref, h, s)], axis=0)
                ak = _dot_nt(ar, _block_diag(blk(kh_ref, h, s)))
                ab = _dot_nt(ar, _block_diag(blk(bh_ref, h, s)))
                a_ak[s] = jnp.where(strict, ak[:C], 0.0)
                a_rk[s] = jnp.where(incl, ak[C:], 0.0)
                l_ab[s] = jnp.where(strict, ab[:C], 0.0)
                a_rb[s] = jnp.where(incl, ab[C:], 0.0).astype(BF16)

        def inv_first():
            for s in streams:
                t_inv[s] = eye + l_ab[s]
                lb = l_ab[s].astype(BF16)
                pw[s] = _dot(lb, _block_diag(lb))

        def inv_level():
            for s in streams:
                tp = _dot(jnp.concatenate([t_inv[s], pw[s]], axis=0).astype(BF16),
                          _block_diag(pw[s].astype(BF16)))
                t_inv[s] = t_inv[s] + tp[:C]
                pw[s] = tp[C:]

        def inv_last():
            for s in streams:
                t_inv[s] = (t_inv[s] + _dot(t_inv[s].astype(BF16), _block_diag(pw[s].astype(BF16)))).astype(BF16)

        def values():
            for s in streams:
                av[s] = _dot(jnp.concatenate([a_ak[s], a_rk[s]], axis=0).astype(BF16),
                             _block_diag(blk(vb_ref, h, s)))

        return [scores, inv_first] + [inv_level] * (int(math.log2(C)) - 2) + [inv_last, values]

    def chain_pieces(h):
        H = subs[h]
        xs, u_b, ys = {}, {}, {}

        def read_state(c):
            def f():
                for q in range(n_groups):
                    s = (c, q)
                    ar = jnp.concatenate([blk(at_ref, h, s), blk(rt_ref, h, s)], axis=0)
                    xs[s] = _dot_nt(ar, _block_diag(state[q].astype(BF16)))
            return f

        def solve(c):
            def f():
                for q in range(n_groups):
                    s = (c, q)
                    x_u = (xs[s][:C] + H["av"][s][:C]).astype(BF16)
                    u_b[s] = _dot(H["t_inv"][s], _block_diag(x_u)).astype(BF16)
            return f

        def update(c):
            def f():
                for q in range(n_groups):
                    s = (c, q)
                    ys[s] = xs[s][C:] + H["av"][s][C:] + _dot(H["a_rb"][s], _block_diag(u_b[s]))
                    vu = jnp.concatenate([blk(vb_ref, h, s), u_b[s]], axis=0)
                    kb = jnp.concatenate([blk(ke_ref, h, s), blk(be_ref, h, s)], axis=0)
                    upd = _dot_tn(vu, kb)
                    upd_c = sum(jnp.where(lane_head == hd, upd[hd * HEAD_DIM:(hd + 1) * HEAD_DIM, :], 0.0)
                                for hd in range(GROUP))
                    tile, slot = divmod(h * sub_chunks + c, SUB_CHUNKS)
                    p_c = pc_ref[tile, slot:slot + 1, q * GROUP_W:(q + 1) * GROUP_W]
                    state[q] = state[q] * p_c + upd_c
                if c == sub_chunks - 1:
                    H["y"] = jnp.concatenate([jnp.concatenate([ys[(cc, q)] for q in range(n_groups)], axis=1)
                                              for cc in range(sub_chunks)], axis=0)
            return f

        return [p for c in range(sub_chunks) for p in (read_state(c), solve(c), update(c))]

    def epilogue_pieces(h):
        H = subs[h]
        rows = slice(h * SB, (h + 1) * SB)

        def center():
            mu = _head_sums(H["y"], ones_bd) * (1.0 / HEAD_DIM)
            H["d"] = H["y"] - mu

        def scale():
            d = H["d"]
            var = _head_sums(d * d, ones_bd) * (1.0 / HEAD_DIM)
            yn = d * lax.rsqrt(var + LNX_EPS) * lng_ref[...] + lnb_ref[...]
            y_ref[rows, :] = ((yn + bonus_ref[rows, :].astype(F32)) * gate_ref[rows, :].astype(F32)).astype(y_ref.dtype)

        return [center, scale]

    def run_merged(*lists):
        lists = [pieces for pieces in lists if pieces]
        if not lists:
            return
        main = max(lists, key=len)
        others = [pieces for pieces in lists if pieces is not main]
        n = len(main)
        for i, piece in enumerate(main):
            piece()
            for other in others:
                for j, o in enumerate(other):
                    if j * n // len(other) == i:
                        o()

    for h in range(n_sub + 2):
        run_merged(chain_pieces(h - 1) if 1 <= h <= n_sub else [],
                   local_pieces(h) if h < n_sub else [],
                   epilogue_pieces(h - 2) if h >= 2 else [])
    for q in range(n_groups):
        s_ref[q] = state[q]


def _rwkv(a_t, k_h, b_h, r_t, k_e, b_e, v_b, bonus, gate, p_c, ln_g, ln_b, side_casts=(), *, tb=2 * SUB_BLOCK,
          sub_chunks=2):
    T = a_t.shape[0]
    n_sub = tb // (sub_chunks * CHUNK)
    assert p_c.shape == (T // SUB_BLOCK, SUBLANES, RWKV_WIDTH) and tb % SUB_BLOCK == 0
    vec = _resident((1, RWKV_WIDTH))
    tile = pl.BlockSpec((tb, RWKV_WIDTH), lambda i: (i, 0))
    cast_specs, cast_shapes = _side_cast_specs(side_casts, T // tb)
    return pl.pallas_call(
        functools.partial(_rwkv_kernel, n_sub=n_sub, sub_chunks=sub_chunks, n_cast=len(side_casts)),
        grid=(T // tb,),
        in_specs=[tile] * 9 + [pl.BlockSpec((tb // SUB_BLOCK, SUBLANES, RWKV_WIDTH), lambda i: (i, 0, 0)), vec, vec]
        + cast_specs,
        out_specs=[tile] + cast_specs,
        out_shape=[jax.ShapeDtypeStruct((T, RWKV_WIDTH), BF16)] + cast_shapes,
        scratch_shapes=[pltpu.VMEM((RWKV_WIDTH // GROUP_W, HEAD_DIM, GROUP_W), F32)],
        compiler_params=pltpu.CompilerParams(dimension_semantics=("arbitrary",),
                                             vmem_limit_bytes=VMEM_LIMIT),
        name="rwkv7",
    )(a_t, k_h, b_h, r_t, k_e, b_e, v_b, bonus, gate, p_c, ln_g, ln_b, *side_casts)


def _attn_kernel(sink_ref, q_ref, kv_ref, kvp_ref, g_ref, o_ref, *, n_blocks):
    n = pl.program_id(0)
    B = WINDOW
    key0 = lax.broadcasted_iota(jnp.int32, (2 * B, LANES), 0) == 0
    low = lax.broadcasted_iota(jnp.int32, (2 * B, LANES), 1) < HEAD_DIM
    qi = lax.broadcasted_iota(jnp.int32, (2 * B, B), 0) % B
    si = lax.broadcasted_iota(jnp.int32, (2 * B, B), 1)
    allowed_cur = si <= qi
    sink_col = si == 0
    top = lax.broadcasted_iota(jnp.int32, (2 * B, 1), 0) < B
    qlow = lax.broadcasted_iota(jnp.int32, (B, LANES), 1) < HEAD_DIM
    pairs_per_kv = (ATTN_HEADS // KV_HEADS) // 2

    k_dup, v_top, v_bot, allowed_prev = [], [], [], []
    for b in range(n_blocks):
        kv = kv_ref[b * B:(b + 1) * B, :]
        kvp = kvp_ref[...] if b == 0 else kv_ref[(b - 1) * B:b * B, :]
        k_win = jnp.concatenate([kvp[:, :LANES], kv[:, :LANES]], axis=0)
        v_win = jnp.where(key0, 0.0, jnp.concatenate([kvp[:, LANES:], kv[:, LANES:]], axis=0))
        k_sw = pltpu.roll(k_win, HEAD_DIM, axis=1)
        v_sw = pltpu.roll(v_win, HEAD_DIM, axis=1)
        k_dup.append([jnp.where(low, k_win, k_sw).astype(BF16), jnp.where(low, k_sw, k_win).astype(BF16)])
        v_top.append([jnp.where(low, v_win, 1.0).astype(BF16), jnp.where(low, v_sw, 1.0).astype(BF16)])
        v_bot.append([jnp.where(low, 1.0, v_sw).astype(BF16), jnp.where(low, 1.0, v_win).astype(BF16)])
        allowed_prev.append((si > qi) & (n > 0) if b == 0 else si > qi)

    work = [(b, p) for b in range(n_blocks) for p in range(ATTN_HEADS // 2)]
    scores, es, outs = {}, {}, {}
    for b, p in work:
        q = q_ref[b * B:(b + 1) * B, p * LANES:(p + 1) * LANES]
        zero = jnp.zeros_like(q)
        q_st = jnp.concatenate([jnp.where(qlow, q, zero), jnp.where(qlow, zero, q)], axis=0).astype(BF16)
        scores[b, p] = _dot_nt(q_st, k_dup[b][p // pairs_per_kv])
    for b, p in work:
        sink = jnp.where(top, sink_ref[2 * p], sink_ref[2 * p + 1])
        sc = scores[b, p]
        s = jnp.concatenate([jnp.where(allowed_prev[b], sc[:, :B], jnp.where(sink_col, sink, -1e30)),
                             jnp.where(allowed_cur, sc[:, B:], -1e30)], axis=1)
        m = jnp.max(s, axis=-1, keepdims=True)
        es[b, p] = jnp.exp((s - m).astype(BF16))
    for b, p in work:
        kvh = p // pairs_per_kv
        o_top = _dot(es[b, p][:B], v_top[b][kvh])
        o_bot = _dot(es[b, p][B:], v_bot[b][kvh])
        sums = pltpu.roll(jnp.where(qlow, o_bot, o_top), HEAD_DIM, axis=1)
        outs[b, p] = jnp.where(qlow, o_top, o_bot) / sums
    g = g_ref[...]
    for b in range(n_blocks):
        o = jnp.concatenate([outs[b, p] for p in range(ATTN_HEADS // 2)], axis=1)
        o_ref[b * B:(b + 1) * B, :] = _rms(o, g).astype(o_ref.dtype)


def _attn(attn_in, sinks, g, *, n_blocks=4):
    T = attn_in.shape[0]
    B = WINDOW
    tq = n_blocks * B
    kv_blk = ATTN_WIDTH // (2 * KV_WIDTH)
    return pl.pallas_call(
        functools.partial(_attn_kernel, n_blocks=n_blocks),
        grid=(T // tq,),
        in_specs=[
            pl.BlockSpec(memory_space=pltpu.SMEM),
            pl.BlockSpec((tq, ATTN_WIDTH), lambda n: (n, 0)),
            pl.BlockSpec((tq, 2 * KV_WIDTH), lambda n: (n, kv_blk)),
            pl.BlockSpec((B, 2 * KV_WIDTH), lambda n: (jnp.maximum(n * n_blocks - 1, 0), kv_blk)),
            _resident((1, ATTN_WIDTH)),
        ],
        out_specs=pl.BlockSpec((tq, ATTN_WIDTH), lambda n: (n, 0)),
        out_shape=jax.ShapeDtypeStruct((T, ATTN_WIDTH), BF16),
        compiler_params=pltpu.CompilerParams(dimension_semantics=("arbitrary",),
                                             vmem_limit_bytes=VMEM_LIMIT),
        name="swa_attn",
    )(sinks, attn_in, attn_in, attn_in, g)


def _out_proj_kernel(x_ref, yr_ref, ya_ref, w_ref, o_ref):
    o_ref[...] = (x_ref[...] + _dot(yr_ref[...], w_ref[:RWKV_WIDTH, :])
                  + _dot(ya_ref[...], w_ref[RWKV_WIDTH:, :]))


def _out_proj(x2d, y_rwkv, y_attn, w, *, tm=512):
    T = x2d.shape[0]
    row = lambda width: pl.BlockSpec((tm, width), lambda i: (i, 0))
    return pl.pallas_call(
        _out_proj_kernel,
        grid=(T // tm,),
        in_specs=[row(D_MODEL), row(RWKV_WIDTH), row(ATTN_WIDTH), _resident(w.shape)],
        out_specs=row(D_MODEL),
        out_shape=jax.ShapeDtypeStruct((T, D_MODEL), F32),
        compiler_params=pltpu.CompilerParams(dimension_semantics=("arbitrary",),
                                             vmem_limit_bytes=VMEM_LIMIT),
        name="out_proj",
    )(x2d, y_rwkv, y_attn, w)


def _ffn_kernel(x_ref, xp_ref, g_ref, wup_ref, wgate_ref, cw_ref, cb_ref, wdown_ref, gf_ref, o_ref, hext_ref):
    i = pl.program_id(0)
    f = pl.program_id(1)

    @pl.when(f == 0)
    def _():
        g = g_ref[...]
        x = x_ref[...]
        hext_ref[PREV_ROWS:, :] = _rms(x, g).astype(BF16)
        hp = jnp.where(i > 0, _rms(xp_ref[...], g), 0.0)
        hext_ref[:PREV_ROWS, :] = hp.astype(BF16)
        o_ref[...] = x

    hx = hext_ref[...]
    u = _dot(hx, wup_ref[...])
    cw = cw_ref[...]
    conv = cb_ref[...] + cw[2:3, :] * u + cw[1:2, :] * pltpu.roll(u, 1, axis=0) + cw[0:1, :] * pltpu.roll(u, 2, axis=0)
    conv = conv[PREV_ROWS:, :]
    gate = _dot(hext_ref[PREV_ROWS:, :], wgate_ref[...])
    act = conv * _sigmoid(conv) * gate
    o_ref[...] += _dot(act.astype(BF16), wdown_ref[...])

    @pl.when(f == pl.num_programs(1) - 1)
    def _():
        o_ref[...] = _rms(o_ref[...], gf_ref[...])


def _ffn(x1, g, w_up, w_gate, conv_w, conv_b, w_down, g_final, *, tm=1024, tf=512):
    T = x1.shape[0]
    prev_blocks = tm // PREV_ROWS
    return pl.pallas_call(
        _ffn_kernel,
        grid=(T // tm, D_FF // tf),
        in_specs=[
            pl.BlockSpec((tm, D_MODEL), lambda i, f: (i, 0)),
            pl.BlockSpec((PREV_ROWS, D_MODEL), lambda i, f: (jnp.maximum(i * prev_blocks - 1, 0), 0)),
            pl.BlockSpec((1, D_MODEL), lambda i, f: (0, 0)),
            pl.BlockSpec((D_MODEL, tf), lambda i, f: (0, f)),
            pl.BlockSpec((D_MODEL, tf), lambda i, f: (0, f)),
            pl.BlockSpec((3, tf), lambda i, f: (0, f)),
            pl.BlockSpec((1, tf), lambda i, f: (0, f)),
            pl.BlockSpec((tf, D_MODEL), lambda i, f: (f, 0)),
            pl.BlockSpec((1, D_MODEL), lambda i, f: (0, 0)),
        ],
        out_specs=pl.BlockSpec((tm, D_MODEL), lambda i, f: (i, 0)),
        out_shape=jax.ShapeDtypeStruct((T, D_MODEL), F32),
        scratch_shapes=[pltpu.VMEM((tm + PREV_ROWS, D_MODEL), BF16)],
        compiler_params=pltpu.CompilerParams(dimension_semantics=("arbitrary", "arbitrary"),
                                             vmem_limit_bytes=FFN_VMEM_LIMIT),
        name="ffn",
    )(x1, x1, g, w_up, w_gate, conv_w, conv_b, w_down, g_final)


def _pad_cols(a, width):
    return jnp.pad(a, ((0, 0), (0, width - a.shape[1])))


def _pad_rows(a, rows):
    return jnp.pad(a, ((0, rows - a.shape[0]), (0, 0)))


def _rope_lane_tables(T):
    half = ROT_DIM // 2
    inv_freq = ROPE_THETA ** (-jnp.arange(0, ROT_DIM, 2, dtype=F32) / ROT_DIM)
    ang = jnp.arange(T, dtype=F32)[:, None] * inv_freq[None, :]
    cos, sin = jnp.cos(ang), jnp.sin(ang)
    ones = jnp.ones((T, HEAD_DIM - ROT_DIM), F32)
    zeros_h = jnp.zeros((T, half), F32)
    zeros_r = jnp.zeros((T, HEAD_DIM - ROT_DIM), F32)
    rep = LANES // HEAD_DIM
    cos_l = jnp.concatenate([cos, cos, ones] * rep, axis=1)
    sin_a = jnp.concatenate([-sin, zeros_h, zeros_r] * rep, axis=1)
    sin_b = jnp.concatenate([zeros_h, sin, zeros_r] * rep, axis=1)
    return cos_l, sin_a, sin_b


_IN_PROJ_SEGMENTS = (
    (3 * RWKV_WIDTH, W_LORA, LANES),
    (3 * RWKV_WIDTH + W_LORA, A_LORA, LANES),
    (3 * RWKV_WIDTH + W_LORA + A_LORA, G_LORA, 2 * LANES),
    (RWKV_WIDTH, RWKV_WIDTH, RWKV_WIDTH),
    (0, RWKV_WIDTH, RWKV_WIDTH),
    (2 * RWKV_WIDTH, RWKV_WIDTH, RWKV_WIDTH),
    (3 * RWKV_WIDTH + W_LORA + A_LORA + G_LORA, ATTN_WIDTH + 2 * KV_WIDTH, ATTN_WIDTH + 2 * KV_WIDTH),
)


def _in_proj_cols(a):
    return jnp.concatenate([_pad_cols(a[:, src:src + width], slot) for src, width, slot in _IN_PROJ_SEGMENTS],
                           axis=1)


def _regroup_kernel(wt_ref, o_ref):
    dst = 0
    for src, width, slot in _IN_PROJ_SEGMENTS:
        o_ref[dst:dst + width, :] = wt_ref[src:src + width, :].astype(o_ref.dtype)
        if slot > width:
            o_ref[dst + width:dst + slot, :] = jnp.zeros((slot - width, o_ref.shape[1]), o_ref.dtype)
        dst += slot


def _regroup_w_in_t(w_t, *, tc=256):
    n_out, n_in = w_t.shape
    out_rows = sum(slot for _, _, slot in _IN_PROJ_SEGMENTS)
    return pl.pallas_call(
        _regroup_kernel,
        grid=(n_in // tc,),
        in_specs=[pl.BlockSpec((n_out, tc), lambda i: (0, i))],
        out_specs=pl.BlockSpec((out_rows, tc), lambda i: (0, i)),
        out_shape=jax.ShapeDtypeStruct((out_rows, n_in), BF16),
        compiler_params=pltpu.CompilerParams(dimension_semantics=("arbitrary",), vmem_limit_bytes=VMEM_LIMIT),
        name="regroup_w_in",
    )(w_t)


def kernel(x, ln_mix_g, w_in, b_attn_qkv, rwkv_shift_mu, rwkv_w0, rwkv_w2, rwkv_a0, rwkv_a2, rwkv_g2, rwkv_k_k, rwkv_k_a, rwkv_r_k, rwkv_lnx_g, rwkv_lnx_b, attn_sinks, attn_out_g, w_out, ln_ffn_g, ffn_w_up, ffn_w_gate, ffn_conv_w, ffn_conv_b, ffn_w_down, ln_final_g):
    B, T, _ = x.shape
    assert B == 1 and ln_mix_g.shape[0] == 1
    l = 0
    x2d = x.reshape(T, D_MODEL)
    row = lambda a: a.reshape(1, -1)

    w_all = _regroup_w_in_t(jnp.swapaxes(w_in[l], 0, 1))
    mu_shift = _in_proj_cols(row(rwkv_shift_mu[l]))
    (a_t, k_h, b_h, r_t, k_e, b_e, v_b, bonus, gate, p_c, attn_in, w_up, w_gate) = _in_proj(
        x2d, row(ln_mix_g[l]), w_all, mu_shift, row(b_attn_qkv[l]),
        row(rwkv_w0[l]), _pad_rows(rwkv_w2[l], LANES).astype(BF16),
        row(rwkv_a0[l]), _pad_rows(rwkv_a2[l], LANES).astype(BF16),
        _pad_rows(rwkv_g2[l], 2 * LANES).astype(BF16), row(rwkv_k_k[l]), row(rwkv_k_a[l]), row(rwkv_r_k[l]),
        _rope_lane_tables(T), side_casts=(ffn_w_up[l], ffn_w_gate[l]))

    y_rwkv, w_down, w_o = _rwkv(a_t, k_h, b_h, r_t, k_e, b_e, v_b, bonus, gate, p_c,
                                row(rwkv_lnx_g[l]), row(rwkv_lnx_b[l]), side_casts=(ffn_w_down[l], w_out[l]))

    y_attn = _attn(attn_in, attn_sinks[l], row(attn_out_g[l]))

    x1 = _out_proj(x2d, y_rwkv, y_attn, w_o)

    out = _ffn(x1, row(ln_ffn_g[l]), w_up, w_gate, ffn_conv_w[l], row(ffn_conv_b[l]), w_down, row(ln_final_g))
    return out.reshape(B, T, D_MODEL)
```

```python
import functools
import math

import jax
import jax.numpy as jnp
from jax import lax
from jax.experimental import pallas as pl
from jax.experimental.pallas import tpu as pltpu

F32 = jnp.float32
BF16 = jnp.bfloat16

D_MODEL = 2048
HEAD_DIM = 64
RWKV_WIDTH = 1024
ATTN_WIDTH = 1024
ATTN_HEADS = 16
KV_HEADS = 2
KV_WIDTH = KV_HEADS * HEAD_DIM
W_LORA = 64
A_LORA = 64
G_LORA = 160
WINDOW = 128
ROPE_THETA = 500000.0
ROT_DIM = 16
D_FF = 5632
NORM_EPS = 1e-5
LNX_EPS = 64e-5

LANES = 128
SUBLANES = 8
PREV_ROWS = 16
LORA_PAD = 512
GROUP = 4
GROUP_W = GROUP * HEAD_DIM
CHUNK = 64
SUB_CHUNKS = 4
SUB_BLOCK = SUB_CHUNKS * CHUNK
VMEM_LIMIT = 56 * 1024 * 1024
IN_PROJ_VMEM_LIMIT = 60 * 1024 * 1024
FFN_VMEM_LIMIT = 60 * 1024 * 1024

assert CHUNK == HEAD_DIM and SUB_CHUNKS <= SUBLANES


def _dot(a, b):
    return lax.dot_general(a, b, (((1,), (0,)), ((), ())), preferred_element_type=F32)


def _dot_nt(a, b):
    return lax.dot_general(a, b, (((1,), (1,)), ((), ())), preferred_element_type=F32)


def _dot_tn(a, b):
    return lax.dot_general(a, b, (((0,), (0,)), ((), ())), preferred_element_type=F32)


def _sigmoid(x):
    return 1.0 / (1.0 + jnp.exp(-x))


def _sigmoid_tanh(x):
    return 0.5 + 0.5 * jnp.tanh(0.5 * x)


def _rms(xv, g):
    ms = jnp.mean(xv * xv, axis=-1, keepdims=True)
    return xv * lax.rsqrt(ms + NORM_EPS) * g


def _resident(shape):
    nd = len(shape)
    return pl.BlockSpec(shape, lambda *_: (0,) * nd, pipeline_mode=pl.Buffered(1))


def _split_bf16(x):
    hi = x.astype(BF16)
    lo = (x - hi.astype(F32)).astype(BF16)
    return hi, lo


def _same_head_mask():
    r_i = lax.broadcasted_iota(jnp.int32, (GROUP_W, GROUP_W), 0) // HEAD_DIM
    c_i = lax.broadcasted_iota(jnp.int32, (GROUP_W, GROUP_W), 1) // HEAD_DIM
    return r_i == c_i


def _head_sums(x, ones_bd):
    xb = x.astype(BF16)
    return jnp.concatenate([_dot(xb[:, q * GROUP_W:(q + 1) * GROUP_W], ones_bd)
                            for q in range(RWKV_WIDTH // GROUP_W)], axis=1)


def _rope(x, cos_l, sin_a, sin_b):
    return x * cos_l + pltpu.roll(x, LANES - ROT_DIM // 2, axis=1) * sin_a + pltpu.roll(x, ROT_DIM // 2, axis=1) * sin_b


def _split_refs(refs, *counts):
    out, pos = [], 0
    for n in counts:
        out.append(tuple(refs[pos:pos + n]))
        pos += n
    assert pos == len(refs)
    return out


def _side_cast_specs(arrays, n_steps):
    assert all(a.shape[0] % (n_steps * PREV_ROWS) == 0 for a in arrays)
    specs = [pl.BlockSpec((a.shape[0] // n_steps, a.shape[1]), lambda i: (i, 0)) for a in arrays]
    shapes = [jax.ShapeDtypeStruct(a.shape, BF16) for a in arrays]
    return specs, shapes


def _side_casts(src_refs, dst_refs):
    for src, dst in zip(src_refs, dst_refs):
        dst[...] = src[...].astype(dst.dtype)


def _in_proj_kernel(x_ref, g_ref, w_ref, mu_ref, battn_ref, w0_ref, w2_ref, a0_ref, a2_ref, g2_ref,
                    kk_ref, ka_ref, rk_ref, cj_ref, sj_ref, cb_ref, sb_ref, *rest, col_chunk, n_cast):
    (cast_in, (at_ref, kh_ref, bh_ref, rt_ref, ke_ref, be_ref, vb_ref, bonus_ref, gate_ref, pc_ref, attn_ref),
     cast_out, (h_ref, carry_ref)) = _split_refs(rest, n_cast, 11, n_cast, 2)
    i = pl.program_id(0)
    tm = x_ref.shape[0]
    n_chunks = tm // CHUNK
    C = CHUNK
    _side_casts(cast_in, cast_out)
    h_ref[...] = _rms(x_ref[...], g_ref[...]).astype(BF16)
    hx = h_ref[...]
    first_rows = lax.broadcasted_iota(jnp.int32, (SUBLANES, col_chunk), 0) == 0

    def shifted(base, width):
        outs = []
        for c0 in range(base, base + width, col_chunk):
            z = _dot_nt(hx, w_ref[c0:c0 + col_chunk, :])
            prev_last = jnp.where(i > 0, carry_ref[0:1, c0:c0 + col_chunk], 0.0)
            zp = pltpu.roll(z, 1, axis=0)
            zp = jnp.concatenate([jnp.where(first_rows, prev_last, zp[:SUBLANES]), zp[SUBLANES:]], axis=0)
            carry_ref[0:1, c0:c0 + col_chunk] = z[tm - 1:tm, :]
            outs.append(z + mu_ref[:, c0:c0 + col_chunk] * (zp - z))
        return jnp.concatenate(outs, axis=1)

    cb, sb = cb_ref[0:1, :], sb_ref[0:1, :]
    cos_l = cb * cj_ref[...] - sb * sj_ref[...]
    sin_t = sb * cj_ref[...] + cb * sj_ref[...]
    head_lane = lax.broadcasted_iota(jnp.int32, (tm, LANES), 1) % HEAD_DIM
    sin_a = jnp.where(head_lane < ROT_DIM // 2, -sin_t, 0.0)
    sin_b = jnp.where((head_lane >= ROT_DIM // 2) & (head_lane < ROT_DIM), sin_t, 0.0)

    def attn_cols(c0):
        base = LORA_PAD + 3 * RWKV_WIDTH
        c1 = min(c0 + col_chunk, attn_ref.shape[1])
        z = _dot_nt(hx, w_ref[base + c0:base + c1, :]) + battn_ref[:, c0:c1]
        slabs = []
        for col in range(c0, c1, LANES):
            zs = z[:, col - c0:col - c0 + LANES]
            if col < ATTN_WIDTH + KV_WIDTH:
                zs = _rope(zs, cos_l, sin_a, sin_b)
            if col < ATTN_WIDTH:
                zs = zs * (HEAD_DIM ** -0.5)
            slabs.append(zs)
        attn_ref[:, c0:c1] = jnp.concatenate(slabs, axis=1)

    ones_bd = _same_head_mask().astype(BF16)
    t_i = lax.broadcasted_iota(jnp.int32, (tm, tm), 0)
    t_j = lax.broadcasted_iota(jnp.int32, (tm, tm), 1)
    tri = ((t_j <= t_i) & (t_j // C == t_i // C)).astype(BF16)
    attn_chunks = list(range(0, attn_ref.shape[1], col_chunk))

    lo = shifted(0, LORA_PAD)
    k = shifted(LORA_PAD, RWKV_WIDTH)
    xw = lo[:, 0:LANES]
    xa = lo[:, LANES:2 * LANES]
    xg = lo[:, 2 * LANES:LORA_PAD]
    u = w0_ref[...] + _dot(jnp.tanh(xw).astype(BF16), w2_ref[...])
    lw = -math.exp(-0.5) * _sigmoid_tanh(u)
    a = _sigmoid_tanh(a0_ref[...] + _dot(xa.astype(BF16), a2_ref[...]))
    gate_ref[...] = _dot(_sigmoid_tanh(xg).astype(BF16), g2_ref[...]).astype(gate_ref.dtype)
    lw_hi, lw_lo = _split_bf16(lw)
    cl = _dot(tri, lw_hi) + _dot(tri, lw_lo)
    last_rows = [cl[(c + 1) * C - 1:(c + 1) * C, :] for c in range(n_chunks)]
    cl_last = jnp.concatenate([jnp.broadcast_to(row, (C, RWKV_WIDTH)) for row in last_rows], axis=0)
    pc_ref[...] = jnp.concatenate([jnp.exp(row) for row in last_rows]
                                  + [jnp.zeros((SUBLANES - n_chunks, RWKV_WIDTH), F32)], axis=0)

    r = shifted(LORA_PAD + RWKV_WIDTH, RWKV_WIDTH)
    kkraw = k * kk_ref[...]
    k2 = k * (1.0 + (a - 1.0) * ka_ref[...])

    v = shifted(LORA_PAD + 2 * RWKV_WIDTH, RWKV_WIDTH)
    sums = _head_sums(jnp.concatenate([kkraw * kkraw, r * k2 * rk_ref[...]], axis=0), ones_bd)
    kk = kkraw / jnp.maximum(jnp.sqrt(sums[:tm]), 1e-12)
    b = kk * a

    attn_cols(attn_chunks[0])
    at_ref[...] = (-kk * jnp.exp(cl - lw)).astype(at_ref.dtype)
    e_neg = jnp.exp(-cl)
    kh_ref[...] = (k2 * e_neg).astype(kh_ref.dtype)
    bh_ref[...] = (b * e_neg).astype(bh_ref.dtype)

    for c0 in attn_chunks[1:2]:
        attn_cols(c0)
    rt_ref[...] = (r * jnp.exp(cl)).astype(rt_ref.dtype)
    e_end = jnp.exp(cl_last - cl)
    ke_ref[...] = (k2 * e_end).astype(ke_ref.dtype)
    be_ref[...] = (b * e_end).astype(be_ref.dtype)

    for c0 in attn_chunks[2:]:
        attn_cols(c0)
    bonus_ref[...] = (sums[tm:] * v).astype(bonus_ref.dtype)
    vb_ref[...] = v.astype(vb_ref.dtype)


def _in_proj(x2d, g, w_all, mu_shift, b_attn, w0, w2, a0, a2, g2, k_k, k_a, r_k, rope_tables, side_casts=(), *,
             tm=SUB_BLOCK, col_chunk=512):
    T = x2d.shape[0]
    n_attn = b_attn.shape[1]
    assert w_all.shape == (3 * RWKV_WIDTH + LORA_PAD + n_attn, D_MODEL) and tm % CHUNK == 0
    n_tiles = T // tm
    cast_specs, cast_shapes = _side_cast_specs(side_casts, n_tiles)
    out = lambda i: (i, 0)
    vec = _resident((1, RWKV_WIDTH))
    tile = pl.BlockSpec((tm, RWKV_WIDTH), out)
    tile_shape = jax.ShapeDtypeStruct((T, RWKV_WIDTH), BF16)
    n_shift = LORA_PAD + 3 * RWKV_WIDTH
    return pl.pallas_call(
        functools.partial(_in_proj_kernel, col_chunk=col_chunk, n_cast=len(side_casts)),
        grid=(n_tiles,),
        in_specs=[
            pl.BlockSpec((tm, D_MODEL), out),
            _resident((1, D_MODEL)),
            _resident(w_all.shape), _resident(mu_shift.shape), _resident(b_attn.shape),
            vec, _resident(w2.shape), vec, _resident(a2.shape), _resident(g2.shape), vec, vec, vec,
        ] + [_resident((tm, LANES))] * 2 + [pl.BlockSpec((SUBLANES, LANES), out)] * 2 + cast_specs,
        out_specs=[tile] * 9 + [
            pl.BlockSpec((None, SUBLANES, RWKV_WIDTH), lambda i: (i, 0, 0)),
            pl.BlockSpec((tm, n_attn), out),
        ] + cast_specs,
        out_shape=[tile_shape] * 9 + [
            jax.ShapeDtypeStruct((T // tm, SUBLANES, RWKV_WIDTH), F32),
            jax.ShapeDtypeStruct((T, n_attn), F32),
        ] + cast_shapes,
        scratch_shapes=[pltpu.VMEM((tm, D_MODEL), BF16), pltpu.VMEM((SUBLANES, n_shift), F32)],
        compiler_params=pltpu.CompilerParams(dimension_semantics=("arbitrary",),
                                             vmem_limit_bytes=IN_PROJ_VMEM_LIMIT),
        name="in_proj",
    )(x2d, g, w_all, mu_shift, b_attn, w0, w2, a0, a2, g2, k_k, k_a, r_k, *rope_tables, *side_casts)


def _block_diag(x):
    blk = lax.broadcasted_iota(jnp.int32, x.shape, 1) // HEAD_DIM
    zero = jnp.zeros_like(x)
    return jnp.concatenate([jnp.where(blk == h, x, zero) for h in range(GROUP)], axis=0)


def _rwkv_kernel(at_ref, kh_ref, bh_ref, rt_ref, ke_ref, be_ref, vb_ref, bonus_ref, gate_ref, pc_ref,
                 lng_ref, lnb_ref, *rest, n_sub, sub_chunks, n_cast):
    cast_in, (y_ref,), cast_out, (s_ref,) = _split_refs(rest, n_cast, 1, n_cast, 1)
    C = CHUNK
    SB = sub_chunks * C
    n_groups = RWKV_WIDTH // GROUP_W

    @pl.when(pl.program_id(0) == 0)
    def _():
        s_ref[...] = jnp.zeros_like(s_ref)

    _side_casts(cast_in, cast_out)
    row = lax.broadcasted_iota(jnp.int32, (C, GROUP_W), 0)
    col = lax.broadcasted_iota(jnp.int32, (C, GROUP_W), 1) % C
    strict = col < row
    incl = col <= row
    eye = (col == row).astype(F32)
    lane_head = lax.broadcasted_iota(jnp.int32, (HEAD_DIM, GROUP_W), 1) // HEAD_DIM
    ones_bd = _same_head_mask().astype(BF16)
    streams = [(c, q) for c in range(sub_chunks) for q in range(n_groups)]
    state = [s_ref[q] for q in range(n_groups)]
    subs = [dict() for _ in range(n_sub)]

    def blk(ref, h, s):
        c, q = s
        return ref[h * SB + c * C:h * SB + (c + 1) * C, q * GROUP_W:(q + 1) * GROUP_W]

    def local_pieces(h):
        H = subs[h]
        a_ak, a_rk, l_ab, pw = {}, {}, {}, {}
        H["a_rb"], H["t_inv"], H["av"] = {}, {}, {}
        a_rb, t_inv, av = H["a_rb"], H["t_inv"], H["av"]

        def scores():
            for s in streams:
                ar = jnp.concatenate([blk(at_ref, h, s), blk(rt_ref, h, s)], axis=0)
                ak = _dot_nt(ar, _block_diag(blk(kh_ref, h, s)))
                ab = _dot_nt(ar, _block_diag(blk(bh_ref, h, s)))
                a_ak[s] = jnp.where(strict, ak[:C], 0.0)
                a_rk[s] = jnp.where(incl, ak[C:], 0.0)
                l_ab[s] = jnp.where(strict, ab[:C], 0.0)
                a_rb[s] = jnp.where(incl, ab[C:], 0.0).astype(BF16)

        def inv_first():
            for s in streams:
                t_inv[s] = eye + l_ab[s]
                lb = l_ab[s].astype(BF16)
                pw[s] = _dot(lb, _block_diag(lb))

        def inv_level():
            for s in streams:
                tp = _dot(jnp.concatenate([t_inv[s], pw[s]], axis=0).astype(BF16),
                          _block_diag(pw[s].astype(BF16)))
                t_inv[s] = t_inv[s] + tp[:C]
                pw[s] = tp[C:]

        def inv_last():
            for s in streams:
                t_inv[s] = (t_inv[s] + _dot(t_inv[s].astype(BF16), _block_diag(pw[s].astype(BF16)))).astype(BF16)

        def values():
            for s in streams:
                av[s] = _dot(jnp.concatenate([a_ak[s], a_rk[s]], axis=0).astype(BF16),
                             _block_diag(blk(vb_ref, h, s)))

        return [scores, inv_first] + [inv_level] * (int(math.log2(C)) - 2) + [inv_last, values]

    def chain_pieces(h):
        H = subs[h]
        xs, u_b, ys = {}, {}, {}

        def read_state(c):
            def f():
                for q in range(n_groups):
                    s = (c, q)
                    ar = jnp.concatenate([blk(at_ref, h, s), blk(rt_ref, h, s)], axis=0)
                    xs[s] = _dot_nt(ar, _block_diag(state[q].astype(BF16)))
            return f

        def solve(c):
            def f():
                for q in range(n_groups):
                    s = (c, q)
                    x_u = (xs[s][:C] + H["av"][s][:C]).astype(BF16)
                    u_b[s] = _dot(H["t_inv"][s], _block_diag(x_u)).astype(BF16)
            return f

        def update(c):
            def f():
                for q in range(n_groups):
                    s = (c, q)
                    ys[s] = xs[s][C:] + H["av"][s][C:] + _dot(H["a_rb"][s], _block_diag(u_b[s]))
                    vu = jnp.concatenate([blk(vb_ref, h, s), u_b[s]], axis=0)
                    kb = jnp.concatenate([blk(ke_ref, h, s), blk(be_ref, h, s)], axis=0)
                    upd = _dot_tn(vu, kb)
                    upd_c = sum(jnp.where(lane_head == hd, upd[hd * HEAD_DIM:(hd + 1) * HEAD_DIM, :], 0.0)
                                for hd in range(GROUP))
                    tile, slot = divmod(h * sub_chunks + c, SUB_CHUNKS)
                    p_c = pc_ref[tile, slot:slot + 1, q * GROUP_W:(q + 1) * GROUP_W]
                    state[q] = state[q] * p_c + upd_c
                if c == sub_chunks - 1:
                    H["y"] = jnp.concatenate([jnp.concatenate([ys[(cc, q)] for q in range(n_groups)], axis=1)
                                              for cc in range(sub_chunks)], axis=0)
            return f

        return [p for c in range(sub_chunks) for p in (read_state(c), solve(c), update(c))]

    def epilogue_pieces(h):
        H = subs[h]
        rows = slice(h * SB, (h + 1) * SB)

        def center():
            mu = _head_sums(H["y"], ones_bd) * (1.0 / HEAD_DIM)
            H["d"] = H["y"] - mu

        def scale():
            d = H["d"]
            var = _head_sums(d * d, ones_bd) * (1.0 / HEAD_DIM)
            yn = d * lax.rsqrt(var + LNX_EPS) * lng_ref[...] + lnb_ref[...]
            y_ref[rows, :] = ((yn + bonus_ref[rows, :].astype(F32)) * gate_ref[rows, :].astype(F32)).astype(y_ref.dtype)

        return [center, scale]

    def run_merged(*lists):
        lists = [pieces for pieces in lists if pieces]
        if not lists:
            return
        main = max(lists, key=len)
        others = [pieces for pieces in lists if pieces is not main]
        n = len(main)
        for i, piece in enumerate(main):
            piece()
            for other in others:
                for j, o in enumerate(other):
                    if j * n // len(other) == i:
                        o()

    for h in range(n_sub + 2):
        run_merged(chain_pieces(h - 1) if 1 <= h <= n_sub else [],
                   local_pieces(h) if h < n_sub else [],
                   epilogue_pieces(h - 2) if h >= 2 else [])
    for q in range(n_groups):
        s_ref[q] = state[q]


def _rwkv(a_t, k_h, b_h, r_t, k_e, b_e, v_b, bonus, gate, p_c, ln_g, ln_b, side_casts=(), *, tb=2 * SUB_BLOCK,
          sub_chunks=2):
    T = a_t.shape[0]
    n_sub = tb // (sub_chunks * CHUNK)
    assert p_c.shape == (T // SUB_BLOCK, SUBLANES, RWKV_WIDTH) and tb % SUB_BLOCK == 0
    vec = _resident((1, RWKV_WIDTH))
    tile = pl.BlockSpec((tb, RWKV_WIDTH), lambda i: (i, 0))
    cast_specs, cast_shapes = _side_cast_specs(side_casts, T // tb)
    return pl.pallas_call(
        functools.partial(_rwkv_kernel, n_sub=n_sub, sub_chunks=sub_chunks, n_cast=len(side_casts)),
        grid=(T // tb,),
        in_specs=[tile] * 9 + [pl.BlockSpec((tb // SUB_BLOCK, SUBLANES, RWKV_WIDTH), lambda i: (i, 0, 0)), vec, vec]
        + cast_specs,
        out_specs=[tile] + cast_specs,
        out_shape=[jax.ShapeDtypeStruct((T, RWKV_WIDTH), BF16)] + cast_shapes,
        scratch_shapes=[pltpu.VMEM((RWKV_WIDTH // GROUP_W, HEAD_DIM, GROUP_W), F32)],
        compiler_params=pltpu.CompilerParams(dimension_semantics=("arbitrary",),
                                             vmem_limit_bytes=VMEM_LIMIT),
        name="rwkv7",
    )(a_t, k_h, b_h, r_t, k_e, b_e, v_b, bonus, gate, p_c, ln_g, ln_b, *side_casts)


def _attn_kernel(sink_ref, q_ref, kv_ref, kvp_ref, g_ref, o_ref, *, n_blocks):
    n = pl.program_id(0)
    B = WINDOW
    key0 = lax.broadcasted_iota(jnp.int32, (2 * B, LANES), 0) == 0
    low = lax.broadcasted_iota(jnp.int32, (2 * B, LANES), 1) < HEAD_DIM
    qi = lax.broadcasted_iota(jnp.int32, (2 * B, B), 0) % B
    si = lax.broadcasted_iota(jnp.int32, (2 * B, B), 1)
    allowed_cur = si <= qi
    sink_col = si == 0
    top = lax.broadcasted_iota(jnp.int32, (2 * B, 1), 0) < B
    qlow = lax.broadcasted_iota(jnp.int32, (B, LANES), 1) < HEAD_DIM
    pairs_per_kv = (ATTN_HEADS // KV_HEADS) // 2

    k_dup, v_top, v_bot, allowed_prev = [], [], [], []
    for b in range(n_blocks):
        kv = kv_ref[b * B:(b + 1) * B, :]
        kvp = kvp_ref[...] if b == 0 else kv_ref[(b - 1) * B:b * B, :]
        k_win = jnp.concatenate([kvp[:, :LANES], kv[:, :LANES]], axis=0)
        v_win = jnp.where(key0, 0.0, jnp.concatenate([kvp[:, LANES:], kv[:, LANES:]], axis=0))
        k_sw = pltpu.roll(k_win, HEAD_DIM, axis=1)
        v_sw = pltpu.roll(v_win, HEAD_DIM, axis=1)
        k_dup.append([jnp.where(low, k_win, k_sw).astype(BF16), jnp.where(low, k_sw, k_win).astype(BF16)])
        v_top.append([jnp.where(low, v_win, 1.0).astype(BF16), jnp.where(low, v_sw, 1.0).astype(BF16)])
        v_bot.append([jnp.where(low, 1.0, v_sw).astype(BF16), jnp.where(low, 1.0, v_win).astype(BF16)])
        allowed_prev.append((si > qi) & (n > 0) if b == 0 else si > qi)

    work = [(b, p) for b in range(n_blocks) for p in range(ATTN_HEADS // 2)]
    scores, es, outs = {}, {}, {}
    for b, p in work:
        q = q_ref[b * B:(b + 1) * B, p * LANES:(p + 1) * LANES]
        zero = jnp.zeros_like(q)
        q_st = jnp.concatenate([jnp.where(qlow, q, zero), jnp.where(qlow, zero, q)], axis=0).astype(BF16)
        scores[b, p] = _dot_nt(q_st, k_dup[b][p // pairs_per_kv])
    for b, p in work:
        sink = jnp.where(top, sink_ref[2 * p], sink_ref[2 * p + 1])
        sc = scores[b, p]
        s = jnp.concatenate([jnp.where(allowed_prev[b], sc[:, :B], jnp.where(sink_col, sink, -1e30)),
                             jnp.where(allowed_cur, sc[:, B:], -1e30)], axis=1)
        m = jnp.max(s, axis=-1, keepdims=True)
        es[b, p] = jnp.exp((s - m).astype(BF16))
    for b, p in work:
        kvh = p // pairs_per_kv
        o_top = _dot(es[b, p][:B], v_top[b][kvh])
        o_bot = _dot(es[b, p][B:], v_bot[b][kvh])
        sums = pltpu.roll(jnp.where(qlow, o_bot, o_top), HEAD_DIM, axis=1)
        outs[b, p] = jnp.where(qlow, o_top, o_bot) / sums
    g = g_ref[...]
    for b in range(n_blocks):
        o = jnp.concatenate([outs[b, p] for p in range(ATTN_HEADS // 2)], axis=1)
        o_ref[b * B:(b + 1) * B, :] = _rms(o, g).astype(o_ref.dtype)


def _attn(attn_in, sinks, g, *, n_blocks=4):
    T = attn_in.shape[0]
    B = WINDOW
    tq = n_blocks * B
    kv_blk = ATTN_WIDTH // (2 * KV_WIDTH)
    return pl.pallas_call(
        functools.partial(_attn_kernel, n_blocks=n_blocks),
        grid=(T // tq,),
        in_specs=[
            pl.BlockSpec(memory_space=pltpu.SMEM),
            pl.BlockSpec((tq, ATTN_WIDTH), lambda n: (n, 0)),
            pl.BlockSpec((tq, 2 * KV_WIDTH), lambda n: (n, kv_blk)),
            pl.BlockSpec((B, 2 * KV_WIDTH), lambda n: (jnp.maximum(n * n_blocks - 1, 0), kv_blk)),
            _resident((1, ATTN_WIDTH)),
        ],
        out_specs=pl.BlockSpec((tq, ATTN_WIDTH), lambda n: (n, 0)),
        out_shape=jax.ShapeDtypeStruct((T, ATTN_WIDTH), BF16),
        compiler_params=pltpu.CompilerParams(dimension_semantics=("arbitrary",),
                                             vmem_limit_bytes=VMEM_LIMIT),
        name="swa_attn",
    )(sinks, attn_in, attn_in, attn_in, g)


def _out_proj_kernel(x_ref, yr_ref, ya_ref, w_ref, o_ref):
    o_ref[...] = (x_ref[...] + _dot(yr_ref[...], w_ref[:RWKV_WIDTH, :])
                  + _dot(ya_ref[...], w_ref[RWKV_WIDTH:, :]))


def _out_proj(x2d, y_rwkv, y_attn, w, *, tm=512):
    T = x2d.shape[0]
    row = lambda width: pl.BlockSpec((tm, width), lambda i: (i, 0))
    return pl.pallas_call(
        _out_proj_kernel,
        grid=(T // tm,),
        in_specs=[row(D_MODEL), row(RWKV_WIDTH), row(ATTN_WIDTH), _resident(w.shape)],
        out_specs=row(D_MODEL),
        out_shape=jax.ShapeDtypeStruct((T, D_MODEL), F32),
        compiler_params=pltpu.CompilerParams(dimension_semantics=("arbitrary",),
                                             vmem_limit_bytes=VMEM_LIMIT),
        name="out_proj",
    )(x2d, y_rwkv, y_attn, w)


def _ffn_kernel(x_ref, xp_ref, g_ref, wup_ref, wgate_ref, cw_ref, cb_ref, wdown_ref, gf_ref, o_ref, hext_ref):
    i = pl.program_id(0)
    f = pl.program_id(1)

    @pl.when(f == 0)
    def _():
        g = g_ref[...]
        x = x_ref[...]
        hext_ref[PREV_ROWS:, :] = _rms(x, g).astype(BF16)
        hp = jnp.where(i > 0, _rms(xp_ref[...], g), 0.0)
        hext_ref[:PREV_ROWS, :] = hp.astype(BF16)
        o_ref[...] = x

    hx = hext_ref[...]
    u = _dot(hx, wup_ref[...])
    cw = cw_ref[...]
    conv = cb_ref[...] + cw[2:3, :] * u + cw[1:2, :] * pltpu.roll(u, 1, axis=0) + cw[0:1, :] * pltpu.roll(u, 2, axis=0)
    conv = conv[PREV_ROWS:, :]
    gate = _dot(hext_ref[PREV_ROWS:, :], wgate_ref[...])
    act = conv * _sigmoid(conv) * gate
    o_ref[...] += _dot(act.astype(BF16), wdown_ref[...])

    @pl.when(f == pl.num_programs(1) - 1)
    def _():
        o_ref[...] = _rms(o_ref[...], gf_ref[...])


def _ffn(x1, g, w_up, w_gate, conv_w, conv_b, w_down, g_final, *, tm=1024, tf=512):
    T = x1.shape[0]
    prev_blocks = tm // PREV_ROWS
    return pl.pallas_call(
        _ffn_kernel,
        grid=(T // tm, D_FF // tf),
        in_specs=[
            pl.BlockSpec((tm, D_MODEL), lambda i, f: (i, 0)),
            pl.BlockSpec((PREV_ROWS, D_MODEL), lambda i, f: (jnp.maximum(i * prev_blocks - 1, 0), 0)),
            pl.BlockSpec((1, D_MODEL), lambda i, f: (0, 0)),
            pl.BlockSpec((D_MODEL, tf), lambda i, f: (0, f)),
            pl.BlockSpec((D_MODEL, tf), lambda i, f: (0, f)),
            pl.BlockSpec((3, tf), lambda i, f: (0, f)),
            pl.BlockSpec((1, tf), lambda i, f: (0, f)),
            pl.BlockSpec((tf, D_MODEL), lambda i, f: (f, 0)),
            pl.BlockSpec((1, D_MODEL), lambda i, f: (0, 0)),
        ],
        out_specs=pl.BlockSpec((tm, D_MODEL), lambda i, f: (i, 0)),
        out_shape=jax.ShapeDtypeStruct((T, D_MODEL), F32),
        scratch_shapes=[pltpu.VMEM((tm + PREV_ROWS, D_MODEL), BF16)],
        compiler_params=pltpu.CompilerParams(dimension_semantics=("arbitrary", "arbitrary"),
                                             vmem_limit_bytes=FFN_VMEM_LIMIT),
        name="ffn",
    )(x1, x1, g, w_up, w_gate, conv_w, conv_b, w_down, g_final)


def _pad_cols(a, width):
    return jnp.pad(a, ((0, 0), (0, width - a.shape[1])))


def _pad_rows(a, rows):
    return jnp.pad(a, ((0, rows - a.shape[0]), (0, 0)))


def _rope_angle_tables(T, tm):
    inv_freq = ROPE_THETA ** (-jnp.arange(0, ROT_DIM, 2, dtype=F32) / ROT_DIM)
    f_head = jnp.concatenate([inv_freq, inv_freq, jnp.zeros((HEAD_DIM - ROT_DIM,), F32)])
    f_lane = jnp.tile(f_head, LANES // HEAD_DIM)[None, :]
    offs = jnp.arange(tm, dtype=F32)[:, None] * f_lane
    starts = jnp.repeat(jnp.arange(0, T, tm, dtype=F32), SUBLANES)[:, None] * f_lane
    return jnp.cos(offs), jnp.sin(offs), jnp.cos(starts), jnp.sin(starts)


_IN_PROJ_SEGMENTS = (
    (3 * RWKV_WIDTH, W_LORA, LANES),
    (3 * RWKV_WIDTH + W_LORA, A_LORA, LANES),
    (3 * RWKV_WIDTH + W_LORA + A_LORA, G_LORA, 2 * LANES),
    (RWKV_WIDTH, RWKV_WIDTH, RWKV_WIDTH),
    (0, RWKV_WIDTH, RWKV_WIDTH),
    (2 * RWKV_WIDTH, RWKV_WIDTH, RWKV_WIDTH),
    (3 * RWKV_WIDTH + W_LORA + A_LORA + G_LORA, ATTN_WIDTH + 2 * KV_WIDTH, ATTN_WIDTH + 2 * KV_WIDTH),
)


def _in_proj_cols(a):
    return jnp.concatenate([_pad_cols(a[:, src:src + width], slot) for src, width, slot in _IN_PROJ_SEGMENTS],
                           axis=1)


def _regroup_kernel(wt_ref, o_ref):
    dst = 0
    for src, width, slot in _IN_PROJ_SEGMENTS:
        o_ref[dst:dst + width, :] = wt_ref[src:src + width, :].astype(o_ref.dtype)
        if slot > width:
            o_ref[dst + width:dst + slot, :] = jnp.zeros((slot - width, o_ref.shape[1]), o_ref.dtype)
        dst += slot


def _regroup_w_in_t(w_t, *, tc=256):
    n_out, n_in = w_t.shape
    out_rows = sum(slot for _, _, slot in _IN_PROJ_SEGMENTS)
    return pl.pallas_call(
        _regroup_kernel,
        grid=(n_in // tc,),
        in_specs=[pl.BlockSpec((n_out, tc), lambda i: (0, i))],
        out_specs=pl.BlockSpec((out_rows, tc), lambda i: (0, i)),
        out_shape=jax.ShapeDtypeStruct((out_rows, n_in), BF16),
        compiler_params=pltpu.CompilerParams(dimension_semantics=("arbitrary",), vmem_limit_bytes=VMEM_LIMIT),
        name="regroup_w_in",
    )(w_t)


def kernel(x, ln_mix_g, w_in, b_attn_qkv, rwkv_shift_mu, rwkv_w0, rwkv_w2, rwkv_a0, rwkv_a2, rwkv_g2, rwkv_k_k, rwkv_k_a, rwkv_r_k, rwkv_lnx_g, rwkv_lnx_b, attn_sinks, attn_out_g, w_out, ln_ffn_g, ffn_w_up, ffn_w_gate, ffn_conv_w, ffn_conv_b, ffn_w_down, ln_final_g):
    B, T, _ = x.shape
    assert B == 1 and ln_mix_g.shape[0] == 1
    l = 0
    x2d = x.reshape(T, D_MODEL)
    row = lambda a: a.reshape(1, -1)

    w_all = _regroup_w_in_t(jnp.swapaxes(w_in[l], 0, 1))
    mu_shift = _in_proj_cols(row(rwkv_shift_mu[l]))
    (a_t, k_h, b_h, r_t, k_e, b_e, v_b, bonus, gate, p_c, attn_in, w_up, w_gate) = _in_proj(
        x2d, row(ln_mix_g[l]), w_all, mu_shift, row(b_attn_qkv[l]),
        row(rwkv_w0[l]), _pad_rows(rwkv_w2[l], LANES).astype(BF16),
        row(rwkv_a0[l]), _pad_rows(rwkv_a2[l], LANES).astype(BF16),
        _pad_rows(rwkv_g2[l], 2 * LANES).astype(BF16), row(rwkv_k_k[l]), row(rwkv_k_a[l]), row(rwkv_r_k[l]),
        _rope_angle_tables(T, SUB_BLOCK), side_casts=(ffn_w_up[l], ffn_w_gate[l]))

    y_rwkv, w_down, w_o = _rwkv(a_t, k_h, b_h, r_t, k_e, b_e, v_b, bonus, gate, p_c,
                                row(rwkv_lnx_g[l]), row(rwkv_lnx_b[l]), side_casts=(ffn_w_down[l], w_out[l]))

    y_attn = _attn(attn_in, attn_sinks[l], row(attn_out_g[l]))

    x1 = _out_proj(x2d, y_rwkv, y_attn, w_o)

    out = _ffn(x1, row(ln_ffn_g[l]), w_up, w_gate, ffn_conv_w[l], row(ffn_conv_b[l]), w_down, row(ln_final_g))
    return out.reshape(B, T, D_MODEL)
```

```python
import functools
import math

import jax
import jax.numpy as jnp
from jax import lax
from jax.experimental import pallas as pl
from jax.experimental.pallas import tpu as pltpu

F32 = jnp.float32
BF16 = jnp.bfloat16

D_MODEL = 2048
HEAD_DIM = 64
RWKV_WIDTH = 1024
ATTN_WIDTH = 1024
ATTN_HEADS = 16
KV_HEADS = 2
KV_WIDTH = KV_HEADS * HEAD_DIM
W_LORA = 64
A_LORA = 64
G_LORA = 160
WINDOW = 128
ROPE_THETA = 500000.0
ROT_DIM = 16
D_FF = 5632
NORM_EPS = 1e-5
LNX_EPS = 64e-5

LANES = 128
SUBLANES = 8
PREV_ROWS = 16
LORA_PAD = 512
GROUP = 4
GROUP_W = GROUP * HEAD_DIM
CHUNK = 64
SUB_CHUNKS = 4
SUB_BLOCK = SUB_CHUNKS * CHUNK
VMEM_LIMIT = 56 * 1024 * 1024
IN_PROJ_VMEM_LIMIT = 60 * 1024 * 1024
FFN_VMEM_LIMIT = 60 * 1024 * 1024

assert CHUNK == HEAD_DIM and SUB_CHUNKS <= SUBLANES


def _dot(a, b):
    return lax.dot_general(a, b, (((1,), (0,)), ((), ())), preferred_element_type=F32)


def _dot_nt(a, b):
    return lax.dot_general(a, b, (((1,), (1,)), ((), ())), preferred_element_type=F32)


def _dot_tn(a, b):
    return lax.dot_general(a, b, (((0,), (0,)), ((), ())), preferred_element_type=F32)


def _sigmoid(x):
    return 1.0 / (1.0 + jnp.exp(-x))


def _sigmoid_tanh(x):
    return 0.5 + 0.5 * jnp.tanh(0.5 * x)


def _rms(xv, g):
    ms = jnp.mean(xv * xv, axis=-1, keepdims=True)
    return xv * lax.rsqrt(ms + NORM_EPS) * g


def _resident(shape):
    nd = len(shape)
    return pl.BlockSpec(shape, lambda *_: (0,) * nd, pipeline_mode=pl.Buffered(1))


def _split_bf16(x):
    hi = x.astype(BF16)
    lo = (x - hi.astype(F32)).astype(BF16)
    return hi, lo


def _same_head_mask():
    r_i = lax.broadcasted_iota(jnp.int32, (GROUP_W, GROUP_W), 0) // HEAD_DIM
    c_i = lax.broadcasted_iota(jnp.int32, (GROUP_W, GROUP_W), 1) // HEAD_DIM
    return r_i == c_i


def _head_sums(x, ones_bd):
    xb = x.astype(BF16)
    return jnp.concatenate([_dot(xb[:, q * GROUP_W:(q + 1) * GROUP_W], ones_bd)
                            for q in range(RWKV_WIDTH // GROUP_W)], axis=1)


def _rope(x, cos_l, sin_a, sin_b):
    return x * cos_l + pltpu.roll(x, LANES - ROT_DIM // 2, axis=1) * sin_a + pltpu.roll(x, ROT_DIM // 2, axis=1) * sin_b


def _split_refs(refs, *counts):
    out, pos = [], 0
    for n in counts:
        out.append(tuple(refs[pos:pos + n]))
        pos += n
    assert pos == len(refs)
    return out


def _side_cast_specs(arrays, n_steps):
    assert all(a.shape[0] % (n_steps * PREV_ROWS) == 0 for a in arrays)
    specs = [pl.BlockSpec((a.shape[0] // n_steps, a.shape[1]), lambda i: (i, 0)) for a in arrays]
    shapes = [jax.ShapeDtypeStruct(a.shape, BF16) for a in arrays]
    return specs, shapes


def _side_casts(src_refs, dst_refs):
    for src, dst in zip(src_refs, dst_refs):
        dst[...] = src[...].astype(dst.dtype)


def _in_proj_kernel(x_ref, g_ref, w_ref, mu_ref, battn_ref, w0_ref, w2_ref, a0_ref, a2_ref, g2_ref,
                    kk_ref, ka_ref, rk_ref, cj_ref, sj_ref, cb_ref, sb_ref, *rest, col_chunk, n_cast):
    (cast_in, (at_ref, kh_ref, bh_ref, rt_ref, ke_ref, be_ref, vb_ref, bonus_ref, gate_ref, pc_ref, attn_ref),
     cast_out, (h_ref, carry_ref)) = _split_refs(rest, n_cast, 11, n_cast, 2)
    i = pl.program_id(0)
    tm = x_ref.shape[0]
    n_chunks = tm // CHUNK
    C = CHUNK
    _side_casts(cast_in, cast_out)
    h_ref[...] = _rms(x_ref[...], g_ref[...]).astype(BF16)
    hx = h_ref[...]
    first_rows = lax.broadcasted_iota(jnp.int32, (SUBLANES, col_chunk), 0) == 0

    def shifted(base, width):
        outs = []
        for c0 in range(base, base + width, col_chunk):
            z = _dot(hx, w_ref[:, c0:c0 + col_chunk])
            prev_last = jnp.where(i > 0, carry_ref[0:1, c0:c0 + col_chunk], 0.0)
            zp = pltpu.roll(z, 1, axis=0)
            zp = jnp.concatenate([jnp.where(first_rows, prev_last, zp[:SUBLANES]), zp[SUBLANES:]], axis=0)
            carry_ref[0:1, c0:c0 + col_chunk] = z[tm - 1:tm, :]
            outs.append(z + mu_ref[:, c0:c0 + col_chunk] * (zp - z))
        return jnp.concatenate(outs, axis=1)

    cb, sb = cb_ref[0:1, :], sb_ref[0:1, :]
    cos_l = cb * cj_ref[...] - sb * sj_ref[...]
    sin_t = sb * cj_ref[...] + cb * sj_ref[...]
    head_lane = lax.broadcasted_iota(jnp.int32, (tm, LANES), 1) % HEAD_DIM
    sin_a = jnp.where(head_lane < ROT_DIM // 2, -sin_t, 0.0)
    sin_b = jnp.where((head_lane >= ROT_DIM // 2) & (head_lane < ROT_DIM), sin_t, 0.0)

    def attn_cols(c0):
        base = LORA_PAD + 3 * RWKV_WIDTH
        c1 = min(c0 + col_chunk, attn_ref.shape[1])
        z = _dot(hx, w_ref[:, base + c0:base + c1]) + battn_ref[:, c0:c1]
        slabs = []
        for col in range(c0, c1, LANES):
            zs = z[:, col - c0:col - c0 + LANES]
            if col < ATTN_WIDTH + KV_WIDTH:
                zs = _rope(zs, cos_l, sin_a, sin_b)
            if col < ATTN_WIDTH:
                zs = zs * (HEAD_DIM ** -0.5)
            slabs.append(zs)
        attn_ref[:, c0:c1] = jnp.concatenate(slabs, axis=1)

    ones_bd = _same_head_mask().astype(BF16)
    t_i = lax.broadcasted_iota(jnp.int32, (tm, tm), 0)
    t_j = lax.broadcasted_iota(jnp.int32, (tm, tm), 1)
    tri = ((t_j <= t_i) & (t_j // C == t_i // C)).astype(BF16)
    attn_chunks = list(range(0, attn_ref.shape[1], col_chunk))

    lo = shifted(0, LORA_PAD)
    k = shifted(LORA_PAD, RWKV_WIDTH)
    xw = lo[:, 0:LANES]
    xa = lo[:, LANES:2 * LANES]
    xg = lo[:, 2 * LANES:LORA_PAD]
    u = w0_ref[...] + _dot(jnp.tanh(xw).astype(BF16), w2_ref[...])
    lw = -math.exp(-0.5) * _sigmoid_tanh(u)
    a = _sigmoid_tanh(a0_ref[...] + _dot(xa.astype(BF16), a2_ref[...]))
    gate_ref[...] = _dot(_sigmoid_tanh(xg).astype(BF16), g2_ref[...]).astype(gate_ref.dtype)
    lw_hi, lw_lo = _split_bf16(lw)
    cl = _dot(tri, lw_hi) + _dot(tri, lw_lo)
    last_rows = [cl[(c + 1) * C - 1:(c + 1) * C, :] for c in range(n_chunks)]
    cl_last = jnp.concatenate([jnp.broadcast_to(row, (C, RWKV_WIDTH)) for row in last_rows], axis=0)
    pc_ref[...] = jnp.concatenate([jnp.exp(row) for row in last_rows]
                                  + [jnp.zeros((SUBLANES - n_chunks, RWKV_WIDTH), F32)], axis=0)

    r = shifted(LORA_PAD + RWKV_WIDTH, RWKV_WIDTH)
    kkraw = k * kk_ref[...]
    k2 = k * (1.0 + (a - 1.0) * ka_ref[...])

    v = shifted(LORA_PAD + 2 * RWKV_WIDTH, RWKV_WIDTH)
    sums = _head_sums(jnp.concatenate([kkraw * kkraw, r * k2 * rk_ref[...]], axis=0), ones_bd)
    kk = kkraw / jnp.maximum(jnp.sqrt(sums[:tm]), 1e-12)
    b = kk * a

    attn_cols(attn_chunks[0])
    at_ref[...] = (-kk * jnp.exp(cl - lw)).astype(at_ref.dtype)
    e_neg = jnp.exp(-cl)
    kh_ref[...] = (k2 * e_neg).astype(kh_ref.dtype)
    bh_ref[...] = (b * e_neg).astype(bh_ref.dtype)

    for c0 in attn_chunks[1:2]:
        attn_cols(c0)
    rt_ref[...] = (r * jnp.exp(cl)).astype(rt_ref.dtype)
    e_end = jnp.exp(cl_last - cl)
    ke_ref[...] = (k2 * e_end).astype(ke_ref.dtype)
    be_ref[...] = (b * e_end).astype(be_ref.dtype)

    for c0 in attn_chunks[2:]:
        attn_cols(c0)
    bonus_ref[...] = (sums[tm:] * v).astype(bonus_ref.dtype)
    vb_ref[...] = v.astype(vb_ref.dtype)


def _in_proj(x2d, g, w_all, mu_shift, b_attn, w0, w2, a0, a2, g2, k_k, k_a, r_k, rope_tables, side_casts=(), *,
             tm=SUB_BLOCK, col_chunk=512):
    T = x2d.shape[0]
    n_attn = b_attn.shape[1]
    assert w_all.shape == (D_MODEL, 3 * RWKV_WIDTH + LORA_PAD + n_attn) and tm % CHUNK == 0
    n_tiles = T // tm
    cast_specs, cast_shapes = _side_cast_specs(side_casts, n_tiles)
    out = lambda i: (i, 0)
    vec = _resident((1, RWKV_WIDTH))
    tile = pl.BlockSpec((tm, RWKV_WIDTH), out)
    tile_shape = jax.ShapeDtypeStruct((T, RWKV_WIDTH), BF16)
    n_shift = LORA_PAD + 3 * RWKV_WIDTH
    return pl.pallas_call(
        functools.partial(_in_proj_kernel, col_chunk=col_chunk, n_cast=len(side_casts)),
        grid=(n_tiles,),
        in_specs=[
            pl.BlockSpec((tm, D_MODEL), out),
            _resident((1, D_MODEL)),
            _resident(w_all.shape), _resident(mu_shift.shape), _resident(b_attn.shape),
            vec, _resident(w2.shape), vec, _resident(a2.shape), _resident(g2.shape), vec, vec, vec,
        ] + [_resident((tm, LANES))] * 2 + [pl.BlockSpec((SUBLANES, LANES), out)] * 2 + cast_specs,
        out_specs=[tile] * 9 + [
            pl.BlockSpec((None, SUBLANES, RWKV_WIDTH), lambda i: (i, 0, 0)),
            pl.BlockSpec((tm, n_attn), out),
        ] + cast_specs,
        out_shape=[tile_shape] * 9 + [
            jax.ShapeDtypeStruct((T // tm, SUBLANES, RWKV_WIDTH), F32),
            jax.ShapeDtypeStruct((T, n_attn), F32),
        ] + cast_shapes,
        scratch_shapes=[pltpu.VMEM((tm, D_MODEL), BF16), pltpu.VMEM((SUBLANES, n_shift), F32)],
        compiler_params=pltpu.CompilerParams(dimension_semantics=("arbitrary",),
                                             vmem_limit_bytes=IN_PROJ_VMEM_LIMIT),
        name="in_proj",
    )(x2d, g, w_all, mu_shift, b_attn, w0, w2, a0, a2, g2, k_k, k_a, r_k, *rope_tables, *side_casts)


def _block_diag(x):
    blk = lax.broadcasted_iota(jnp.int32, x.shape, 1) // HEAD_DIM
    zero = jnp.zeros_like(x)
    return jnp.concatenate([jnp.where(blk == h, x, zero) for h in range(GROUP)], axis=0)


def _rwkv_kernel(at_ref, kh_ref, bh_ref, rt_ref, ke_ref, be_ref, vb_ref, bonus_ref, gate_ref, pc_ref,
                 lng_ref, lnb_ref, *rest, n_sub, sub_chunks, n_cast):
    cast_in, (y_ref,), cast_out, (s_ref,) = _split_refs(rest, n_cast, 1, n_cast, 1)
    C = CHUNK
    SB = sub_chunks * C
    n_groups = RWKV_WIDTH // GROUP_W

    @pl.when(pl.program_id(0) == 0)
    def _():
        s_ref[...] = jnp.zeros_like(s_ref)

    _side_casts(cast_in, cast_out)
    row = lax.broadcasted_iota(jnp.int32, (C, GROUP_W), 0)
    col = lax.broadcasted_iota(jnp.int32, (C, GROUP_W), 1) % C
    strict = col < row
    incl = col <= row
    eye = (col == row).astype(F32)
    lane_head = lax.broadcasted_iota(jnp.int32, (HEAD_DIM, GROUP_W), 1) // HEAD_DIM
    ones_bd = _same_head_mask().astype(BF16)
    streams = [(c, q) for c in range(sub_chunks) for q in range(n_groups)]
    state = [s_ref[q] for q in range(n_groups)]
    subs = [dict() for _ in range(n_sub)]

    def blk(ref, h, s):
        c, q = s
        return ref[h * SB + c * C:h * SB + (c + 1) * C, q * GROUP_W:(q + 1) * GROUP_W]

    def local_pieces(h):
        H = subs[h]
        a_ak, a_rk, l_ab, pw = {}, {}, {}, {}
        H["a_rb"], H["t_inv"], H["av"] = {}, {}, {}
        a_rb, t_inv, av = H["a_rb"], H["t_inv"], H["av"]

        def scores():
            for s in streams:
                ar = jnp.concatenate([blk(at_ref, h, s), blk(rt_ref, h, s)], axis=0)
                ak = _dot_nt(ar, _block_diag(blk(kh_ref, h, s)))
                ab = _dot_nt(ar, _block_diag(blk(bh_ref, h, s)))
                a_ak[s] = jnp.where(strict, ak[:C], 0.0)
                a_rk[s] = jnp.where(incl, ak[C:], 0.0)
                l_ab[s] = jnp.where(strict, ab[:C], 0.0)
                a_rb[s] = jnp.where(incl, ab[C:], 0.0).astype(BF16)

        def inv_first():
            for s in streams:
                t_inv[s] = eye + l_ab[s]
                lb = l_ab[s].astype(BF16)
                pw[s] = _dot(lb, _block_diag(lb))

        def inv_level():
            for s in streams:
                tp = _dot(jnp.concatenate([t_inv[s], pw[s]], axis=0).astype(BF16),
                          _block_diag(pw[s].astype(BF16)))
                t_inv[s] = t_inv[s] + tp[:C]
                pw[s] = tp[C:]

        def inv_last():
            for s in streams:
                t_inv[s] = (t_inv[s] + _dot(t_inv[s].astype(BF16), _block_diag(pw[s].astype(BF16)))).astype(BF16)

        def values():
            for s in streams:
                av[s] = _dot(jnp.concatenate([a_ak[s], a_rk[s]], axis=0).astype(BF16),
                             _block_diag(blk(vb_ref, h, s)))

        return [scores, inv_first] + [inv_level] * (int(math.log2(C)) - 2) + [inv_last, values]

    def chain_pieces(h):
        H = subs[h]
        xs, u_b, ys = {}, {}, {}

        def read_state(c):
            def f():
                for q in range(n_groups):
                    s = (c, q)
                    ar = jnp.concatenate([blk(at_ref, h, s), blk(rt_ref, h, s)], axis=0)
                    xs[s] = _dot_nt(ar, _block_diag(state[q].astype(BF16)))
            return f

        def solve(c):
            def f():
                for q in range(n_groups):
                    s = (c, q)
                    x_u = (xs[s][:C] + H["av"][s][:C]).astype(BF16)
                    u_b[s] = _dot(H["t_inv"][s], _block_diag(x_u)).astype(BF16)
            return f

        def update(c):
            def f():
                for q in range(n_groups):
                    s = (c, q)
                    ys[s] = xs[s][C:] + H["av"][s][C:] + _dot(H["a_rb"][s], _block_diag(u_b[s]))
                    vu = jnp.concatenate([blk(vb_ref, h, s), u_b[s]], axis=0)
                    kb = jnp.concatenate([blk(ke_ref, h, s), blk(be_ref, h, s)], axis=0)
                    upd = _dot_tn(vu, kb)
                    upd_c = sum(jnp.where(lane_head == hd, upd[hd * HEAD_DIM:(hd + 1) * HEAD_DIM, :], 0.0)
                                for hd in range(GROUP))
                    tile, slot = divmod(h * sub_chunks + c, SUB_CHUNKS)
                    p_c = pc_ref[tile, slot:slot + 1, q * GROUP_W:(q + 1) * GROUP_W]
                    state[q] = state[q] * p_c + upd_c
                if c == sub_chunks - 1:
                    H["y"] = jnp.concatenate([jnp.concatenate([ys[(cc, q)] for q in range(n_groups)], axis=1)
                                              for cc in range(sub_chunks)], axis=0)
            return f

        return [p for c in range(sub_chunks) for p in (read_state(c), solve(c), update(c))]

    def epilogue_pieces(h):
        H = subs[h]
        rows = slice(h * SB, (h + 1) * SB)

        def center():
            mu = _head_sums(H["y"], ones_bd) * (1.0 / HEAD_DIM)
            H["d"] = H["y"] - mu

        def scale():
            d = H["d"]
            var = _head_sums(d * d, ones_bd) * (1.0 / HEAD_DIM)
            yn = d * lax.rsqrt(var + LNX_EPS) * lng_ref[...] + lnb_ref[...]
            y_ref[rows, :] = ((yn + bonus_ref[rows, :].astype(F32)) * gate_ref[rows, :].astype(F32)).astype(y_ref.dtype)

        return [center, scale]

    def run_merged(*lists):
        lists = [pieces for pieces in lists if pieces]
        if not lists:
            return
        main = max(lists, key=len)
        others = [pieces for pieces in lists if pieces is not main]
        n = len(main)
        for i, piece in enumerate(main):
            piece()
            for other in others:
                for j, o in enumerate(other):
                    if j * n // len(other) == i:
                        o()

    for h in range(n_sub + 2):
        run_merged(chain_pieces(h - 1) if 1 <= h <= n_sub else [],
                   local_pieces(h) if h < n_sub else [],
                   epilogue_pieces(h - 2) if h >= 2 else [])
    for q in range(n_groups):
        s_ref[q] = state[q]


def _rwkv(a_t, k_h, b_h, r_t, k_e, b_e, v_b, bonus, gate, p_c, ln_g, ln_b, side_casts=(), *, tb=2 * SUB_BLOCK,
          sub_chunks=2):
    T = a_t.shape[0]
    n_sub = tb // (sub_chunks * CHUNK)
    assert p_c.shape == (T // SUB_BLOCK, SUBLANES, RWKV_WIDTH) and tb % SUB_BLOCK == 0
    vec = _resident((1, RWKV_WIDTH))
    tile = pl.BlockSpec((tb, RWKV_WIDTH), lambda i: (i, 0))
    cast_specs, cast_shapes = _side_cast_specs(side_casts, T // tb)
    return pl.pallas_call(
        functools.partial(_rwkv_kernel, n_sub=n_sub, sub_chunks=sub_chunks, n_cast=len(side_casts)),
        grid=(T // tb,),
        in_specs=[tile] * 9 + [pl.BlockSpec((tb // SUB_BLOCK, SUBLANES, RWKV_WIDTH), lambda i: (i, 0, 0)), vec, vec]
        + cast_specs,
        out_specs=[tile] + cast_specs,
        out_shape=[jax.ShapeDtypeStruct((T, RWKV_WIDTH), BF16)] + cast_shapes,
        scratch_shapes=[pltpu.VMEM((RWKV_WIDTH // GROUP_W, HEAD_DIM, GROUP_W), F32)],
        compiler_params=pltpu.CompilerParams(dimension_semantics=("arbitrary",),
                                             vmem_limit_bytes=VMEM_LIMIT),
        name="rwkv7",
    )(a_t, k_h, b_h, r_t, k_e, b_e, v_b, bonus, gate, p_c, ln_g, ln_b, *side_casts)


def _attn_kernel(sink_ref, q_ref, kv_ref, kvp_ref, g_ref, o_ref, *, n_blocks):
    n = pl.program_id(0)
    B = WINDOW
    key0 = lax.broadcasted_iota(jnp.int32, (2 * B, LANES), 0) == 0
    low = lax.broadcasted_iota(jnp.int32, (2 * B, LANES), 1) < HEAD_DIM
    qi = lax.broadcasted_iota(jnp.int32, (2 * B, B), 0) % B
    si = lax.broadcasted_iota(jnp.int32, (2 * B, B), 1)
    allowed_cur = si <= qi
    sink_col = si == 0
    top = lax.broadcasted_iota(jnp.int32, (2 * B, 1), 0) < B
    qlow = lax.broadcasted_iota(jnp.int32, (B, LANES), 1) < HEAD_DIM
    pairs_per_kv = (ATTN_HEADS // KV_HEADS) // 2

    k_dup, v_top, v_bot, allowed_prev = [], [], [], []
    for b in range(n_blocks):
        kv = kv_ref[b * B:(b + 1) * B, :]
        kvp = kvp_ref[...] if b == 0 else kv_ref[(b - 1) * B:b * B, :]
        k_win = jnp.concatenate([kvp[:, :LANES], kv[:, :LANES]], axis=0)
        v_win = jnp.where(key0, 0.0, jnp.concatenate([kvp[:, LANES:], kv[:, LANES:]], axis=0))
        k_sw = pltpu.roll(k_win, HEAD_DIM, axis=1)
        v_sw = pltpu.roll(v_win, HEAD_DIM, axis=1)
        k_dup.append([jnp.where(low, k_win, k_sw).astype(BF16), jnp.where(low, k_sw, k_win).astype(BF16)])
        v_top.append([jnp.where(low, v_win, 1.0).astype(BF16), jnp.where(low, v_sw, 1.0).astype(BF16)])
        v_bot.append([jnp.where(low, 1.0, v_sw).astype(BF16), jnp.where(low, 1.0, v_win).astype(BF16)])
        allowed_prev.append((si > qi) & (n > 0) if b == 0 else si > qi)

    work = [(b, p) for b in range(n_blocks) for p in range(ATTN_HEADS // 2)]
    scores, es, outs = {}, {}, {}
    for b, p in work:
        q = q_ref[b * B:(b + 1) * B, p * LANES:(p + 1) * LANES]
        zero = jnp.zeros_like(q)
        q_st = jnp.concatenate([jnp.where(qlow, q, zero), jnp.where(qlow, zero, q)], axis=0).astype(BF16)
        scores[b, p] = _dot_nt(q_st, k_dup[b][p // pairs_per_kv])
    for b, p in work:
        sink = jnp.where(top, sink_ref[2 * p], sink_ref[2 * p + 1])
        sc = scores[b, p]
        s = jnp.concatenate([jnp.where(allowed_prev[b], sc[:, :B], jnp.where(sink_col, sink, -1e30)),
                             jnp.where(allowed_cur, sc[:, B:], -1e30)], axis=1)
        m = jnp.max(s, axis=-1, keepdims=True)
        es[b, p] = jnp.exp((s - m).astype(BF16))
    for b, p in work:
        kvh = p // pairs_per_kv
        o_top = _dot(es[b, p][:B], v_top[b][kvh])
        o_bot = _dot(es[b, p][B:], v_bot[b][kvh])
        sums = pltpu.roll(jnp.where(qlow, o_bot, o_top), HEAD_DIM, axis=1)
        outs[b, p] = jnp.where(qlow, o_top, o_bot) / sums
    g = g_ref[...]
    for b in range(n_blocks):
        o = jnp.concatenate([outs[b, p] for p in range(ATTN_HEADS // 2)], axis=1)
        o_ref[b * B:(b + 1) * B, :] = _rms(o, g).astype(o_ref.dtype)


def _attn(attn_in, sinks, g, *, n_blocks=4):
    T = attn_in.shape[0]
    B = WINDOW
    tq = n_blocks * B
    kv_blk = ATTN_WIDTH // (2 * KV_WIDTH)
    return pl.pallas_call(
        functools.partial(_attn_kernel, n_blocks=n_blocks),
        grid=(T // tq,),
        in_specs=[
            pl.BlockSpec(memory_space=pltpu.SMEM),
            pl.BlockSpec((tq, ATTN_WIDTH), lambda n: (n, 0)),
            pl.BlockSpec((tq, 2 * KV_WIDTH), lambda n: (n, kv_blk)),
            pl.BlockSpec((B, 2 * KV_WIDTH), lambda n: (jnp.maximum(n * n_blocks - 1, 0), kv_blk)),
            _resident((1, ATTN_WIDTH)),
        ],
        out_specs=pl.BlockSpec((tq, ATTN_WIDTH), lambda n: (n, 0)),
        out_shape=jax.ShapeDtypeStruct((T, ATTN_WIDTH), BF16),
        compiler_params=pltpu.CompilerParams(dimension_semantics=("arbitrary",),
                                             vmem_limit_bytes=VMEM_LIMIT),
        name="swa_attn",
    )(sinks, attn_in, attn_in, attn_in, g)


def _out_proj_kernel(x_ref, yr_ref, ya_ref, w_ref, o_ref):
    o_ref[...] = (x_ref[...] + _dot(yr_ref[...], w_ref[:RWKV_WIDTH, :])
                  + _dot(ya_ref[...], w_ref[RWKV_WIDTH:, :]))


def _out_proj(x2d, y_rwkv, y_attn, w, *, tm=512):
    T = x2d.shape[0]
    row = lambda width: pl.BlockSpec((tm, width), lambda i: (i, 0))
    return pl.pallas_call(
        _out_proj_kernel,
        grid=(T // tm,),
        in_specs=[row(D_MODEL), row(RWKV_WIDTH), row(ATTN_WIDTH), _resident(w.shape)],
        out_specs=row(D_MODEL),
        out_shape=jax.ShapeDtypeStruct((T, D_MODEL), F32),
        compiler_params=pltpu.CompilerParams(dimension_semantics=("arbitrary",),
                                             vmem_limit_bytes=VMEM_LIMIT),
        name="out_proj",
    )(x2d, y_rwkv, y_attn, w)


def _ffn_kernel(x_ref, xp_ref, g_ref, wup_ref, wgate_ref, cw_ref, cb_ref, wdown_ref, gf_ref, o_ref, hext_ref):
    i = pl.program_id(0)
    f = pl.program_id(1)

    @pl.when(f == 0)
    def _():
        g = g_ref[...]
        x = x_ref[...]
        hext_ref[PREV_ROWS:, :] = _rms(x, g).astype(BF16)
        hp = jnp.where(i > 0, _rms(xp_ref[...], g), 0.0)
        hext_ref[:PREV_ROWS, :] = hp.astype(BF16)
        o_ref[...] = x

    hx = hext_ref[...]
    u = _dot(hx, wup_ref[...])
    cw = cw_ref[...]
    conv = cb_ref[...] + cw[2:3, :] * u + cw[1:2, :] * pltpu.roll(u, 1, axis=0) + cw[0:1, :] * pltpu.roll(u, 2, axis=0)
    conv = conv[PREV_ROWS:, :]
    gate = _dot(hext_ref[PREV_ROWS:, :], wgate_ref[...])
    act = conv * _sigmoid(conv) * gate
    o_ref[...] += _dot(act.astype(BF16), wdown_ref[...])

    @pl.when(f == pl.num_programs(1) - 1)
    def _():
        o_ref[...] = _rms(o_ref[...], gf_ref[...])


def _ffn(x1, g, w_up, w_gate, conv_w, conv_b, w_down, g_final, *, tm=1024, tf=512):
    T = x1.shape[0]
    prev_blocks = tm // PREV_ROWS
    return pl.pallas_call(
        _ffn_kernel,
        grid=(T // tm, D_FF // tf),
        in_specs=[
            pl.BlockSpec((tm, D_MODEL), lambda i, f: (i, 0)),
            pl.BlockSpec((PREV_ROWS, D_MODEL), lambda i, f: (jnp.maximum(i * prev_blocks - 1, 0), 0)),
            pl.BlockSpec((1, D_MODEL), lambda i, f: (0, 0)),
            pl.BlockSpec((D_MODEL, tf), lambda i, f: (0, f)),
            pl.BlockSpec((D_MODEL, tf), lambda i, f: (0, f)),
            pl.BlockSpec((3, tf), lambda i, f: (0, f)),
            pl.BlockSpec((1, tf), lambda i, f: (0, f)),
            pl.BlockSpec((tf, D_MODEL), lambda i, f: (f, 0)),
            pl.BlockSpec((1, D_MODEL), lambda i, f: (0, 0)),
        ],
        out_specs=pl.BlockSpec((tm, D_MODEL), lambda i, f: (i, 0)),
        out_shape=jax.ShapeDtypeStruct((T, D_MODEL), F32),
        scratch_shapes=[pltpu.VMEM((tm + PREV_ROWS, D_MODEL), BF16)],
        compiler_params=pltpu.CompilerParams(dimension_semantics=("arbitrary", "arbitrary"),
                                             vmem_limit_bytes=FFN_VMEM_LIMIT),
        name="ffn",
    )(x1, x1, g, w_up, w_gate, conv_w, conv_b, w_down, g_final)


def _pad_cols(a, width):
    return jnp.pad(a, ((0, 0), (0, width - a.shape[1])))


def _pad_rows(a, rows):
    return jnp.pad(a, ((0, rows - a.shape[0]), (0, 0)))


def _rope_angle_tables(T, tm):
    inv_freq = ROPE_THETA ** (-jnp.arange(0, ROT_DIM, 2, dtype=F32) / ROT_DIM)
    f_head = jnp.concatenate([inv_freq, inv_freq, jnp.zeros((HEAD_DIM - ROT_DIM,), F32)])
    f_lane = jnp.tile(f_head, LANES // HEAD_DIM)[None, :]
    offs = jnp.arange(tm, dtype=F32)[:, None] * f_lane
    starts = jnp.repeat(jnp.arange(0, T, tm, dtype=F32), SUBLANES)[:, None] * f_lane
    return jnp.cos(offs), jnp.sin(offs), jnp.cos(starts), jnp.sin(starts)


_IN_PROJ_SEGMENTS = (
    (3 * RWKV_WIDTH, W_LORA, LANES),
    (3 * RWKV_WIDTH + W_LORA, A_LORA, LANES),
    (3 * RWKV_WIDTH + W_LORA + A_LORA, G_LORA, 2 * LANES),
    (RWKV_WIDTH, RWKV_WIDTH, RWKV_WIDTH),
    (0, RWKV_WIDTH, RWKV_WIDTH),
    (2 * RWKV_WIDTH, RWKV_WIDTH, RWKV_WIDTH),
    (3 * RWKV_WIDTH + W_LORA + A_LORA + G_LORA, ATTN_WIDTH + 2 * KV_WIDTH, ATTN_WIDTH + 2 * KV_WIDTH),
)


def _in_proj_cols(a):
    return jnp.concatenate([_pad_cols(a[:, src:src + width], slot) for src, width, slot in _IN_PROJ_SEGMENTS],
                           axis=1)


def _regroup_kernel(wt_ref, o_ref):
    lane = lax.broadcasted_iota(jnp.int32, (o_ref.shape[0], LANES), 1)
    dst = 0
    for src, width, slot in _IN_PROJ_SEGMENTS:
        for p in range(0, slot, LANES):
            valid = min(max(width - p, 0), LANES)
            if valid == 0:
                o_ref[:, dst + p:dst + p + LANES] = jnp.zeros((o_ref.shape[0], LANES), o_ref.dtype)
                continue
            piece = wt_ref[src + p:src + p + LANES, :].T
            if valid < LANES:
                piece = jnp.where(lane < valid, piece, 0.0)
            o_ref[:, dst + p:dst + p + LANES] = piece.astype(o_ref.dtype)
        dst += slot


def _regroup_w_in_t(w_t, *, tc=256):
    n_out, n_in = w_t.shape
    out_cols = sum(slot for _, _, slot in _IN_PROJ_SEGMENTS)
    return pl.pallas_call(
        _regroup_kernel,
        grid=(n_in // tc,),
        in_specs=[pl.BlockSpec((n_out, tc), lambda i: (0, i))],
        out_specs=pl.BlockSpec((tc, out_cols), lambda i: (i, 0)),
        out_shape=jax.ShapeDtypeStruct((n_in, out_cols), BF16),
        compiler_params=pltpu.CompilerParams(dimension_semantics=("arbitrary",), vmem_limit_bytes=VMEM_LIMIT),
        name="regroup_w_in",
    )(w_t)


def kernel(x, ln_mix_g, w_in, b_attn_qkv, rwkv_shift_mu, rwkv_w0, rwkv_w2, rwkv_a0, rwkv_a2, rwkv_g2, rwkv_k_k, rwkv_k_a, rwkv_r_k, rwkv_lnx_g, rwkv_lnx_b, attn_sinks, attn_out_g, w_out, ln_ffn_g, ffn_w_up, ffn_w_gate, ffn_conv_w, ffn_conv_b, ffn_w_down, ln_final_g):
    B, T, _ = x.shape
    assert B == 1 and ln_mix_g.shape[0] == 1
    l = 0
    x2d = x.reshape(T, D_MODEL)
    row = lambda a: a.reshape(1, -1)

    w_all = _regroup_w_in_t(jnp.swapaxes(w_in[l], 0, 1))
    mu_shift = _in_proj_cols(row(rwkv_shift_mu[l]))
    (a_t, k_h, b_h, r_t, k_e, b_e, v_b, bonus, gate, p_c, attn_in, w_up, w_gate) = _in_proj(
        x2d, row(ln_mix_g[l]), w_all, mu_shift, row(b_attn_qkv[l]),
        row(rwkv_w0[l]), _pad_rows(rwkv_w2[l], LANES).astype(BF16),
        row(rwkv_a0[l]), _pad_rows(rwkv_a2[l], LANES).astype(BF16),
        _pad_rows(rwkv_g2[l], 2 * LANES).astype(BF16), row(rwkv_k_k[l]), row(rwkv_k_a[l]), row(rwkv_r_k[l]),
        _rope_angle_tables(T, SUB_BLOCK), side_casts=(ffn_w_up[l], ffn_w_gate[l]))

    y_rwkv, w_down, w_o = _rwkv(a_t, k_h, b_h, r_t, k_e, b_e, v_b, bonus, gate, p_c,
                                row(rwkv_lnx_g[l]), row(rwkv_lnx_b[l]), side_casts=(ffn_w_down[l], w_out[l]))

    y_attn = _attn(attn_in, attn_sinks[l], row(attn_out_g[l]))

    x1 = _out_proj(x2d, y_rwkv, y_attn, w_o)

    out = _ffn(x1, row(ln_ffn_g[l]), w_up, w_gate, ffn_conv_w[l], row(ffn_conv_b[l]), w_down, row(ln_final_g))
    return out.reshape(B, T, D_MODEL)
```

```python
import functools
import math

import jax
import jax.numpy as jnp
from jax import lax
from jax.experimental import pallas as pl
from jax.experimental.pallas import tpu as pltpu

F32 = jnp.float32
BF16 = jnp.bfloat16

D_MODEL = 2048
HEAD_DIM = 64
RWKV_WIDTH = 1024
ATTN_WIDTH = 1024
ATTN_HEADS = 16
KV_HEADS = 2
KV_WIDTH = KV_HEADS * HEAD_DIM
W_LORA = 64
A_LORA = 64
G_LORA = 160
WINDOW = 128
ROPE_THETA = 500000.0
ROT_DIM = 16
D_FF = 5632
NORM_EPS = 1e-5
LNX_EPS = 64e-5

LANES = 128
SUBLANES = 8
PREV_ROWS = 16
LORA_PAD = 512
GROUP = 4
GROUP_W = GROUP * HEAD_DIM
CHUNK = 64
SUB_CHUNKS = 4
SUB_BLOCK = SUB_CHUNKS * CHUNK
VMEM_LIMIT = 56 * 1024 * 1024
IN_PROJ_VMEM_LIMIT = 60 * 1024 * 1024
FFN_VMEM_LIMIT = 60 * 1024 * 1024

assert CHUNK == HEAD_DIM and SUB_CHUNKS <= SUBLANES


def _dot(a, b):
    return lax.dot_general(a, b, (((1,), (0,)), ((), ())), preferred_element_type=F32)


def _dot_nt(a, b):
    return lax.dot_general(a, b, (((1,), (1,)), ((), ())), preferred_element_type=F32)


def _dot_tn(a, b):
    return lax.dot_general(a, b, (((0,), (0,)), ((), ())), preferred_element_type=F32)


def _sigmoid(x):
    return 1.0 / (1.0 + jnp.exp(-x))


def _sigmoid_tanh(x):
    return 0.5 + 0.5 * jnp.tanh(0.5 * x)


def _rms(xv, g):
    ms = jnp.mean(xv * xv, axis=-1, keepdims=True)
    return xv * lax.rsqrt(ms + NORM_EPS) * g


def _resident(shape):
    nd = len(shape)
    return pl.BlockSpec(shape, lambda *_: (0,) * nd, pipeline_mode=pl.Buffered(1))


def _split_bf16(x):
    hi = x.astype(BF16)
    lo = (x - hi.astype(F32)).astype(BF16)
    return hi, lo


def _same_head_mask():
    r_i = lax.broadcasted_iota(jnp.int32, (GROUP_W, GROUP_W), 0) // HEAD_DIM
    c_i = lax.broadcasted_iota(jnp.int32, (GROUP_W, GROUP_W), 1) // HEAD_DIM
    return r_i == c_i


def _head_sums(x, ones_bd):
    xb = x.astype(BF16)
    return jnp.concatenate([_dot(xb[:, q * GROUP_W:(q + 1) * GROUP_W], ones_bd)
                            for q in range(RWKV_WIDTH // GROUP_W)], axis=1)


def _rope(x, cos_l, sin_a, sin_b):
    return x * cos_l + pltpu.roll(x, LANES - ROT_DIM // 2, axis=1) * sin_a + pltpu.roll(x, ROT_DIM // 2, axis=1) * sin_b


def _split_refs(refs, *counts):
    out, pos = [], 0
    for n in counts:
        out.append(tuple(refs[pos:pos + n]))
        pos += n
    assert pos == len(refs)
    return out


def _side_cast_specs(arrays, n_steps):
    assert all(a.shape[0] % (n_steps * PREV_ROWS) == 0 for a in arrays)
    specs = [pl.BlockSpec((a.shape[0] // n_steps, a.shape[1]), lambda i: (i, 0)) for a in arrays]
    shapes = [jax.ShapeDtypeStruct(a.shape, BF16) for a in arrays]
    return specs, shapes


def _side_casts(src_refs, dst_refs):
    for src, dst in zip(src_refs, dst_refs):
        dst[...] = src[...].astype(dst.dtype)


def _in_proj_kernel(x_ref, g_ref, w_ref, mu_ref, battn_ref, w0_ref, w2_ref, a0_ref, a2_ref, g2_ref,
                    kk_ref, ka_ref, rk_ref, cj_ref, sj_ref, cb_ref, sb_ref, *rest, col_chunk, n_cast):
    (cast_in, (at_ref, kh_ref, bh_ref, rt_ref, ke_ref, be_ref, vb_ref, bonus_ref, gate_ref, pc_ref, attn_ref),
     cast_out, (h_ref, carry_ref)) = _split_refs(rest, n_cast, 11, n_cast, 2)
    i = pl.program_id(0)
    tm = x_ref.shape[0]
    n_chunks = tm // CHUNK
    C = CHUNK
    _side_casts(cast_in, cast_out)
    h_ref[...] = _rms(x_ref[...], g_ref[...]).astype(BF16)
    hx = h_ref[...]
    first_rows = lax.broadcasted_iota(jnp.int32, (SUBLANES, col_chunk), 0) == 0

    def shifted(base, width):
        outs = []
        for c0 in range(base, base + width, col_chunk):
            z = _dot(hx, w_ref[:, c0:c0 + col_chunk])
            prev_last = jnp.where(i > 0, carry_ref[0:1, c0:c0 + col_chunk], 0.0)
            zp = pltpu.roll(z, 1, axis=0)
            zp = jnp.concatenate([jnp.where(first_rows, prev_last, zp[:SUBLANES]), zp[SUBLANES:]], axis=0)
            carry_ref[0:1, c0:c0 + col_chunk] = z[tm - 1:tm, :]
            outs.append(z + mu_ref[:, c0:c0 + col_chunk] * (zp - z))
        return jnp.concatenate(outs, axis=1)

    cb, sb = cb_ref[0:1, :], sb_ref[0:1, :]
    cos_l = cb * cj_ref[...] - sb * sj_ref[...]
    sin_t = sb * cj_ref[...] + cb * sj_ref[...]
    head_lane = lax.broadcasted_iota(jnp.int32, (tm, LANES), 1) % HEAD_DIM
    sin_a = jnp.where(head_lane < ROT_DIM // 2, -sin_t, 0.0)
    sin_b = jnp.where((head_lane >= ROT_DIM // 2) & (head_lane < ROT_DIM), sin_t, 0.0)

    def attn_cols(c0):
        base = LORA_PAD + 3 * RWKV_WIDTH
        c1 = min(c0 + col_chunk, attn_ref.shape[1])
        z = _dot(hx, w_ref[:, base + c0:base + c1]) + battn_ref[:, c0:c1]
        slabs = []
        for col in range(c0, c1, LANES):
            zs = z[:, col - c0:col - c0 + LANES]
            if col < ATTN_WIDTH + KV_WIDTH:
                zs = _rope(zs, cos_l, sin_a, sin_b)
            if col < ATTN_WIDTH:
                zs = zs * (HEAD_DIM ** -0.5)
            slabs.append(zs)
        attn_ref[:, c0:c1] = jnp.concatenate(slabs, axis=1)

    ones_bd = _same_head_mask().astype(BF16)
    t_i = lax.broadcasted_iota(jnp.int32, (tm, tm), 0)
    t_j = lax.broadcasted_iota(jnp.int32, (tm, tm), 1)
    tri = ((t_j <= t_i) & (t_j // C == t_i // C)).astype(BF16)
    attn_chunks = list(range(0, attn_ref.shape[1], col_chunk))

    lo = shifted(0, LORA_PAD)
    k = shifted(LORA_PAD, RWKV_WIDTH)
    xw = lo[:, 0:LANES]
    xa = lo[:, LANES:2 * LANES]
    xg = lo[:, 2 * LANES:LORA_PAD]
    u = w0_ref[...] + _dot(jnp.tanh(xw).astype(BF16), w2_ref[...])
    lw = -math.exp(-0.5) * _sigmoid_tanh(u)
    a = _sigmoid_tanh(a0_ref[...] + _dot(xa.astype(BF16), a2_ref[...]))
    gate_ref[...] = _dot(_sigmoid_tanh(xg).astype(BF16), g2_ref[...]).astype(gate_ref.dtype)
    lw_hi, lw_lo = _split_bf16(lw)
    cl = _dot(tri, lw_hi) + _dot(tri, lw_lo)
    last_rows = [cl[(c + 1) * C - 1:(c + 1) * C, :] for c in range(n_chunks)]
    cl_last = jnp.concatenate([jnp.broadcast_to(row, (C, RWKV_WIDTH)) for row in last_rows], axis=0)
    pc_ref[...] = jnp.concatenate([jnp.exp(row) for row in last_rows]
                                  + [jnp.zeros((SUBLANES - n_chunks, RWKV_WIDTH), F32)], axis=0)

    r = shifted(LORA_PAD + RWKV_WIDTH, RWKV_WIDTH)
    kkraw = k * kk_ref[...]
    k2 = k * (1.0 + (a - 1.0) * ka_ref[...])

    v = shifted(LORA_PAD + 2 * RWKV_WIDTH, RWKV_WIDTH)
    sums = _head_sums(jnp.concatenate([kkraw * kkraw, r * k2 * rk_ref[...]], axis=0), ones_bd)
    kk = kkraw / jnp.maximum(jnp.sqrt(sums[:tm]), 1e-12)
    b = kk * a

    attn_cols(attn_chunks[0])
    at_ref[...] = (-kk * jnp.exp(cl - lw)).astype(at_ref.dtype)
    e_neg = jnp.exp(-cl)
    kh_ref[...] = (k2 * e_neg).astype(kh_ref.dtype)
    bh_ref[...] = (b * e_neg).astype(bh_ref.dtype)

    for c0 in attn_chunks[1:2]:
        attn_cols(c0)
    rt_ref[...] = (r * jnp.exp(cl)).astype(rt_ref.dtype)
    e_end = jnp.exp(cl_last - cl)
    ke_ref[...] = (k2 * e_end).astype(ke_ref.dtype)
    be_ref[...] = (b * e_end).astype(be_ref.dtype)

    for c0 in attn_chunks[2:]:
        attn_cols(c0)
    bonus_ref[...] = (sums[tm:] * v).astype(bonus_ref.dtype)
    vb_ref[...] = v.astype(vb_ref.dtype)


def _in_proj(x2d, g, w_all, mu_shift, b_attn, w0, w2, a0, a2, g2, k_k, k_a, r_k, rope_tables, side_casts=(), *,
             tm=SUB_BLOCK, col_chunk=512):
    T = x2d.shape[0]
    n_attn = b_attn.shape[1]
    assert w_all.shape == (D_MODEL, 3 * RWKV_WIDTH + LORA_PAD + n_attn) and tm % CHUNK == 0
    n_tiles = T // tm
    cast_specs, cast_shapes = _side_cast_specs(side_casts, n_tiles)
    out = lambda i: (i, 0)
    vec = _resident((1, RWKV_WIDTH))
    tile = pl.BlockSpec((tm, RWKV_WIDTH), out)
    tile_shape = jax.ShapeDtypeStruct((T, RWKV_WIDTH), BF16)
    n_shift = LORA_PAD + 3 * RWKV_WIDTH
    return pl.pallas_call(
        functools.partial(_in_proj_kernel, col_chunk=col_chunk, n_cast=len(side_casts)),
        grid=(n_tiles,),
        in_specs=[
            pl.BlockSpec((tm, D_MODEL), out),
            _resident((1, D_MODEL)),
            _resident(w_all.shape), _resident(mu_shift.shape), _resident(b_attn.shape),
            vec, _resident(w2.shape), vec, _resident(a2.shape), _resident(g2.shape), vec, vec, vec,
        ] + [_resident((tm, LANES))] * 2 + [pl.BlockSpec((SUBLANES, LANES), out)] * 2 + cast_specs,
        out_specs=[tile] * 9 + [
            pl.BlockSpec((None, SUBLANES, RWKV_WIDTH), lambda i: (i, 0, 0)),
            pl.BlockSpec((tm, n_attn), out),
        ] + cast_specs,
        out_shape=[tile_shape] * 9 + [
            jax.ShapeDtypeStruct((T // tm, SUBLANES, RWKV_WIDTH), F32),
            jax.ShapeDtypeStruct((T, n_attn), F32),
        ] + cast_shapes,
        scratch_shapes=[pltpu.VMEM((tm, D_MODEL), BF16), pltpu.VMEM((SUBLANES, n_shift), F32)],
        compiler_params=pltpu.CompilerParams(dimension_semantics=("arbitrary",),
                                             vmem_limit_bytes=IN_PROJ_VMEM_LIMIT),
        name="in_proj",
    )(x2d, g, w_all, mu_shift, b_attn, w0, w2, a0, a2, g2, k_k, k_a, r_k, *rope_tables, *side_casts)


def _block_diag(x):
    blk = lax.broadcasted_iota(jnp.int32, x.shape, 1) // HEAD_DIM
    zero = jnp.zeros_like(x)
    return jnp.concatenate([jnp.where(blk == h, x, zero) for h in range(GROUP)], axis=0)


def _rwkv_kernel(at_ref, kh_ref, bh_ref, rt_ref, ke_ref, be_ref, vb_ref, bonus_ref, gate_ref, pc_ref,
                 lng_ref, lnb_ref, *rest, n_sub, sub_chunks, n_cast):
    cast_in, (y_ref,), cast_out, (s_ref,) = _split_refs(rest, n_cast, 1, n_cast, 1)
    C = CHUNK
    SB = sub_chunks * C
    n_groups = RWKV_WIDTH // GROUP_W

    @pl.when(pl.program_id(0) == 0)
    def _():
        s_ref[...] = jnp.zeros_like(s_ref)

    _side_casts(cast_in, cast_out)
    row = lax.broadcasted_iota(jnp.int32, (C, GROUP_W), 0)
    col = lax.broadcasted_iota(jnp.int32, (C, GROUP_W), 1) % C
    strict = col < row
    incl = col <= row
    eye = (col == row).astype(F32)
    lane_head = lax.broadcasted_iota(jnp.int32, (HEAD_DIM, GROUP_W), 1) // HEAD_DIM
    ones_bd = _same_head_mask().astype(BF16)
    streams = [(c, q) for c in range(sub_chunks) for q in range(n_groups)]
    state = [s_ref[q] for q in range(n_groups)]
    subs = [dict() for _ in range(n_sub)]

    def blk(ref, h, s):
        c, q = s
        return ref[h * SB + c * C:h * SB + (c + 1) * C, q * GROUP_W:(q + 1) * GROUP_W]

    def local_pieces(h):
        H = subs[h]
        a_ak, a_rk, l_ab, pw = {}, {}, {}, {}
        H["a_rb"], H["t_inv"], H["av"] = {}, {}, {}
        a_rb, t_inv, av = H["a_rb"], H["t_inv"], H["av"]

        def scores():
            for s in streams:
                ar = jnp.concatenate([blk(at_ref, h, s), blk(rt_ref, h, s)], axis=0)
                ak = _dot_nt(ar, _block_diag(blk(kh_ref, h, s)))
                ab = _dot_nt(ar, _block_diag(blk(bh_ref, h, s)))
                a_ak[s] = jnp.where(strict, ak[:C], 0.0)
                a_rk[s] = jnp.where(incl, ak[C:], 0.0)
                l_ab[s] = jnp.where(strict, ab[:C], 0.0)
                a_rb[s] = jnp.where(incl, ab[C:], 0.0).astype(BF16)

        def inv_first():
            for s in streams:
                t_inv[s] = eye + l_ab[s]
                lb = l_ab[s].astype(BF16)
                pw[s] = _dot(lb, _block_diag(lb))

        def inv_level():
            for s in streams:
                tp = _dot(jnp.concatenate([t_inv[s], pw[s]], axis=0).astype(BF16),
                          _block_diag(pw[s].astype(BF16)))
                t_inv[s] = t_inv[s] + tp[:C]
                pw[s] = tp[C:]

        def inv_last():
            for s in streams:
                t_inv[s] = (t_inv[s] + _dot(t_inv[s].astype(BF16), _block_diag(pw[s].astype(BF16)))).astype(BF16)

        def values():
            for s in streams:
                av[s] = _dot(jnp.concatenate([a_ak[s], a_rk[s]], axis=0).astype(BF16),
                             _block_diag(blk(vb_ref, h, s)))

        return [scores, inv_first] + [inv_level] * (int(math.log2(C)) - 2) + [inv_last, values]

    def chain_pieces(h):
        H = subs[h]
        xs, u_b, ys = {}, {}, {}

        def read_state(c):
            def f():
                for q in range(n_groups):
                    s = (c, q)
                    ar = jnp.concatenate([blk(at_ref, h, s), blk(rt_ref, h, s)], axis=0)
                    xs[s] = _dot_nt(ar, _block_diag(state[q].astype(BF16)))
            return f

        def solve(c):
            def f():
                for q in range(n_groups):
                    s = (c, q)
                    x_u = (xs[s][:C] + H["av"][s][:C]).astype(BF16)
                    u_b[s] = _dot(H["t_inv"][s], _block_diag(x_u)).astype(BF16)
            return f

        def update(c):
            def f():
                for q in range(n_groups):
                    s = (c, q)
                    ys[s] = xs[s][C:] + H["av"][s][C:] + _dot(H["a_rb"][s], _block_diag(u_b[s]))
                    vu = jnp.concatenate([blk(vb_ref, h, s), u_b[s]], axis=0)
                    kb = jnp.concatenate([blk(ke_ref, h, s), blk(be_ref, h, s)], axis=0)
                    upd = _dot_tn(vu, kb)
                    upd_c = sum(jnp.where(lane_head == hd, upd[hd * HEAD_DIM:(hd + 1) * HEAD_DIM, :], 0.0)
                                for hd in range(GROUP))
                    tile, slot = divmod(h * sub_chunks + c, SUB_CHUNKS)
                    p_c = pc_ref[tile, slot:slot + 1, q * GROUP_W:(q + 1) * GROUP_W]
                    state[q] = state[q] * p_c + upd_c
                if c == sub_chunks - 1:
                    H["y"] = jnp.concatenate([jnp.concatenate([ys[(cc, q)] for q in range(n_groups)], axis=1)
                                              for cc in range(sub_chunks)], axis=0)
            return f

        return [p for c in range(sub_chunks) for p in (read_state(c), solve(c), update(c))]

    def epilogue_pieces(h):
        H = subs[h]
        rows = slice(h * SB, (h + 1) * SB)

        def center():
            mu = _head_sums(H["y"], ones_bd) * (1.0 / HEAD_DIM)
            H["d"] = H["y"] - mu

        def scale():
            d = H["d"]
            var = _head_sums(d * d, ones_bd) * (1.0 / HEAD_DIM)
            yn = d * lax.rsqrt(var + LNX_EPS) * lng_ref[...] + lnb_ref[...]
            y_ref[rows, :] = ((yn + bonus_ref[rows, :].astype(F32)) * gate_ref[rows, :].astype(F32)).astype(y_ref.dtype)

        return [center, scale]

    def run_merged(*lists):
        lists = [pieces for pieces in lists if pieces]
        if not lists:
            return
        main = max(lists, key=len)
        others = [pieces for pieces in lists if pieces is not main]
        n = len(main)
        for i, piece in enumerate(main):
            piece()
            for other in others:
                for j, o in enumerate(other):
                    if j * n // len(other) == i:
                        o()

    for h in range(n_sub + 2):
        run_merged(chain_pieces(h - 1) if 1 <= h <= n_sub else [],
                   local_pieces(h) if h < n_sub else [],
                   epilogue_pieces(h - 2) if h >= 2 else [])
    for q in range(n_groups):
        s_ref[q] = state[q]


def _rwkv(a_t, k_h, b_h, r_t, k_e, b_e, v_b, bonus, gate, p_c, ln_g, ln_b, side_casts=(), *, tb=2 * SUB_BLOCK,
          sub_chunks=2):
    T = a_t.shape[0]
    n_sub = tb // (sub_chunks * CHUNK)
    assert p_c.shape == (T // SUB_BLOCK, SUBLANES, RWKV_WIDTH) and tb % SUB_BLOCK == 0
    vec = _resident((1, RWKV_WIDTH))
    tile = pl.BlockSpec((tb, RWKV_WIDTH), lambda i: (i, 0))
    cast_specs, cast_shapes = _side_cast_specs(side_casts, T // tb)
    return pl.pallas_call(
        functools.partial(_rwkv_kernel, n_sub=n_sub, sub_chunks=sub_chunks, n_cast=len(side_casts)),
        grid=(T // tb,),
        in_specs=[tile] * 9 + [pl.BlockSpec((tb // SUB_BLOCK, SUBLANES, RWKV_WIDTH), lambda i: (i, 0, 0)), vec, vec]
        + cast_specs,
        out_specs=[tile] + cast_specs,
        out_shape=[jax.ShapeDtypeStruct((T, RWKV_WIDTH), BF16)] + cast_shapes,
        scratch_shapes=[pltpu.VMEM((RWKV_WIDTH // GROUP_W, HEAD_DIM, GROUP_W), F32)],
        compiler_params=pltpu.CompilerParams(dimension_semantics=("arbitrary",),
                                             vmem_limit_bytes=VMEM_LIMIT),
        name="rwkv7",
    )(a_t, k_h, b_h, r_t, k_e, b_e, v_b, bonus, gate, p_c, ln_g, ln_b, *side_casts)


def _attn_kernel(sink_ref, q_ref, kv_ref, kvp_ref, g_ref, o_ref, *, n_blocks):
    n = pl.program_id(0)
    B = WINDOW
    key0 = lax.broadcasted_iota(jnp.int32, (2 * B, LANES), 0) == 0
    low = lax.broadcasted_iota(jnp.int32, (2 * B, LANES), 1) < HEAD_DIM
    qi = lax.broadcasted_iota(jnp.int32, (2 * B, B), 0) % B
    si = lax.broadcasted_iota(jnp.int32, (2 * B, B), 1)
    allowed_cur = si <= qi
    sink_col = si == 0
    top = lax.broadcasted_iota(jnp.int32, (2 * B, 1), 0) < B
    qlow = lax.broadcasted_iota(jnp.int32, (B, LANES), 1) < HEAD_DIM
    pairs_per_kv = (ATTN_HEADS // KV_HEADS) // 2

    k_dup, v_top, v_bot, allowed_prev = [], [], [], []
    for b in range(n_blocks):
        kv = kv_ref[b * B:(b + 1) * B, :]
        kvp = kvp_ref[...] if b == 0 else kv_ref[(b - 1) * B:b * B, :]
        k_win = jnp.concatenate([kvp[:, :LANES], kv[:, :LANES]], axis=0)
        v_win = jnp.where(key0, 0.0, jnp.concatenate([kvp[:, LANES:], kv[:, LANES:]], axis=0))
        k_sw = pltpu.roll(k_win, HEAD_DIM, axis=1)
        v_sw = pltpu.roll(v_win, HEAD_DIM, axis=1)
        k_dup.append([jnp.where(low, k_win, k_sw).astype(BF16), jnp.where(low, k_sw, k_win).astype(BF16)])
        v_top.append([jnp.where(low, v_win, 1.0).astype(BF16), jnp.where(low, v_sw, 1.0).astype(BF16)])
        v_bot.append([jnp.where(low, 1.0, v_sw).astype(BF16), jnp.where(low, 1.0, v_win).astype(BF16)])
        allowed_prev.append((si > qi) & (n > 0) if b == 0 else si > qi)

    work = [(b, p) for b in range(n_blocks) for p in range(ATTN_HEADS // 2)]
    scores, es, outs = {}, {}, {}
    for b, p in work:
        q = q_ref[b * B:(b + 1) * B, p * LANES:(p + 1) * LANES]
        zero = jnp.zeros_like(q)
        q_st = jnp.concatenate([jnp.where(qlow, q, zero), jnp.where(qlow, zero, q)], axis=0).astype(BF16)
        scores[b, p] = _dot_nt(q_st, k_dup[b][p // pairs_per_kv])
    for b, p in work:
        sink = jnp.where(top, sink_ref[2 * p], sink_ref[2 * p + 1])
        sc = scores[b, p]
        s = jnp.concatenate([jnp.where(allowed_prev[b], sc[:, :B], jnp.where(sink_col, sink, -1e30)),
                             jnp.where(allowed_cur, sc[:, B:], -1e30)], axis=1)
        m = jnp.max(s, axis=-1, keepdims=True)
        es[b, p] = jnp.exp((s - m).astype(BF16))
    for b, p in work:
        kvh = p // pairs_per_kv
        o_top = _dot(es[b, p][:B], v_top[b][kvh])
        o_bot = _dot(es[b, p][B:], v_bot[b][kvh])
        sums = pltpu.roll(jnp.where(qlow, o_bot, o_top), HEAD_DIM, axis=1)
        outs[b, p] = jnp.where(qlow, o_top, o_bot) / sums
    g = g_ref[...]
    for b in range(n_blocks):
        o = jnp.concatenate([outs[b, p] for p in range(ATTN_HEADS // 2)], axis=1)
        o_ref[b * B:(b + 1) * B, :] = _rms(o, g).astype(o_ref.dtype)


def _attn(attn_in, sinks, g, *, n_blocks=4):
    T = attn_in.shape[0]
    B = WINDOW
    tq = n_blocks * B
    kv_blk = ATTN_WIDTH // (2 * KV_WIDTH)
    return pl.pallas_call(
        functools.partial(_attn_kernel, n_blocks=n_blocks),
        grid=(T // tq,),
        in_specs=[
            pl.BlockSpec(memory_space=pltpu.SMEM),
            pl.BlockSpec((tq, ATTN_WIDTH), lambda n: (n, 0)),
            pl.BlockSpec((tq, 2 * KV_WIDTH), lambda n: (n, kv_blk)),
            pl.BlockSpec((B, 2 * KV_WIDTH), lambda n: (jnp.maximum(n * n_blocks - 1, 0), kv_blk)),
            _resident((1, ATTN_WIDTH)),
        ],
        out_specs=pl.BlockSpec((tq, ATTN_WIDTH), lambda n: (n, 0)),
        out_shape=jax.ShapeDtypeStruct((T, ATTN_WIDTH), BF16),
        compiler_params=pltpu.CompilerParams(dimension_semantics=("arbitrary",),
                                             vmem_limit_bytes=VMEM_LIMIT),
        name="swa_attn",
    )(sinks, attn_in, attn_in, attn_in, g)


def _out_proj_kernel(x_ref, yr_ref, ya_ref, w_ref, o_ref):
    o_ref[...] = (x_ref[...] + _dot(yr_ref[...], w_ref[:RWKV_WIDTH, :])
                  + _dot(ya_ref[...], w_ref[RWKV_WIDTH:, :]))


def _out_proj(x2d, y_rwkv, y_attn, w, *, tm=512):
    T = x2d.shape[0]
    row = lambda width: pl.BlockSpec((tm, width), lambda i: (i, 0))
    return pl.pallas_call(
        _out_proj_kernel,
        grid=(T // tm,),
        in_specs=[row(D_MODEL), row(RWKV_WIDTH), row(ATTN_WIDTH), _resident(w.shape)],
        out_specs=row(D_MODEL),
        out_shape=jax.ShapeDtypeStruct((T, D_MODEL), F32),
        compiler_params=pltpu.CompilerParams(dimension_semantics=("arbitrary",),
                                             vmem_limit_bytes=VMEM_LIMIT),
        name="out_proj",
    )(x2d, y_rwkv, y_attn, w)


def _ffn_kernel(x_ref, xp_ref, g_ref, wup_ref, wgate_ref, cw_ref, cb_ref, wdown_ref, gf_ref, o_ref, hext_ref):
    i = pl.program_id(0)
    f = pl.program_id(1)

    @pl.when(f == 0)
    def _():
        g = g_ref[...]
        x = x_ref[...]
        hext_ref[PREV_ROWS:, :] = _rms(x, g).astype(BF16)
        hp = jnp.where(i > 0, _rms(xp_ref[...], g), 0.0)
        hext_ref[:PREV_ROWS, :] = hp.astype(BF16)
        o_ref[...] = x

    hx = hext_ref[...]
    u = _dot(hx, wup_ref[...])
    cw = cw_ref[...]
    conv = cb_ref[...] + cw[2:3, :] * u + cw[1:2, :] * pltpu.roll(u, 1, axis=0) + cw[0:1, :] * pltpu.roll(u, 2, axis=0)
    conv = conv[PREV_ROWS:, :]
    gate = _dot(hext_ref[PREV_ROWS:, :], wgate_ref[...])
    act = conv * _sigmoid(conv) * gate
    o_ref[...] += _dot(act.astype(BF16), wdown_ref[...])

    @pl.when(f == pl.num_programs(1) - 1)
    def _():
        o_ref[...] = _rms(o_ref[...], gf_ref[...])


def _ffn(x1, g, w_up, w_gate, conv_w, conv_b, w_down, g_final, *, tm=1024, tf=512):
    T = x1.shape[0]
    prev_blocks = tm // PREV_ROWS
    return pl.pallas_call(
        _ffn_kernel,
        grid=(T // tm, D_FF // tf),
        in_specs=[
            pl.BlockSpec((tm, D_MODEL), lambda i, f: (i, 0)),
            pl.BlockSpec((PREV_ROWS, D_MODEL), lambda i, f: (jnp.maximum(i * prev_blocks - 1, 0), 0)),
            pl.BlockSpec((1, D_MODEL), lambda i, f: (0, 0)),
            pl.BlockSpec((D_MODEL, tf), lambda i, f: (0, f)),
            pl.BlockSpec((D_MODEL, tf), lambda i, f: (0, f)),
            pl.BlockSpec((3, tf), lambda i, f: (0, f)),
            pl.BlockSpec((1, tf), lambda i, f: (0, f)),
            pl.BlockSpec((tf, D_MODEL), lambda i, f: (f, 0)),
            pl.BlockSpec((1, D_MODEL), lambda i, f: (0, 0)),
        ],
        out_specs=pl.BlockSpec((tm, D_MODEL), lambda i, f: (i, 0)),
        out_shape=jax.ShapeDtypeStruct((T, D_MODEL), F32),
        scratch_shapes=[pltpu.VMEM((tm + PREV_ROWS, D_MODEL), BF16)],
        compiler_params=pltpu.CompilerParams(dimension_semantics=("arbitrary", "arbitrary"),
                                             vmem_limit_bytes=FFN_VMEM_LIMIT),
        name="ffn",
    )(x1, x1, g, w_up, w_gate, conv_w, conv_b, w_down, g_final)


def _pad_cols(a, width):
    return jnp.pad(a, ((0, 0), (0, width - a.shape[1])))


def _pad_rows(a, rows):
    return jnp.pad(a, ((0, rows - a.shape[0]), (0, 0)))


def _rope_angle_tables(T, tm):
    inv_freq = ROPE_THETA ** (-jnp.arange(0, ROT_DIM, 2, dtype=F32) / ROT_DIM)
    f_head = jnp.concatenate([inv_freq, inv_freq, jnp.zeros((HEAD_DIM - ROT_DIM,), F32)])
    f_lane = jnp.tile(f_head, LANES // HEAD_DIM)[None, :]
    offs = jnp.arange(tm, dtype=F32)[:, None] * f_lane
    starts = jnp.repeat(jnp.arange(0, T, tm, dtype=F32), SUBLANES)[:, None] * f_lane
    return jnp.cos(offs), jnp.sin(offs), jnp.cos(starts), jnp.sin(starts)


_IN_PROJ_SEGMENTS = (
    (3 * RWKV_WIDTH, W_LORA, LANES),
    (3 * RWKV_WIDTH + W_LORA, A_LORA, LANES),
    (3 * RWKV_WIDTH + W_LORA + A_LORA, G_LORA, 2 * LANES),
    (RWKV_WIDTH, RWKV_WIDTH, RWKV_WIDTH),
    (0, RWKV_WIDTH, RWKV_WIDTH),
    (2 * RWKV_WIDTH, RWKV_WIDTH, RWKV_WIDTH),
    (3 * RWKV_WIDTH + W_LORA + A_LORA + G_LORA, ATTN_WIDTH + 2 * KV_WIDTH, ATTN_WIDTH + 2 * KV_WIDTH),
)


def _in_proj_cols(a):
    return jnp.concatenate([_pad_cols(a[:, src:src + width], slot) for src, width, slot in _IN_PROJ_SEGMENTS],
                           axis=1)


def _regroup_kernel(wt_ref, o_ref):
    lane = lax.broadcasted_iota(jnp.int32, (o_ref.shape[0], LANES), 1)
    dst = 0
    for src, width, slot in _IN_PROJ_SEGMENTS:
        for p in range(0, slot, LANES):
            valid = min(max(width - p, 0), LANES)
            if valid == 0:
                o_ref[:, dst + p:dst + p + LANES] = jnp.zeros((o_ref.shape[0], LANES), o_ref.dtype)
                continue
            piece = wt_ref[src + p:src + p + LANES, :].T
            if valid < LANES:
                piece = jnp.where(lane < valid, piece, 0.0)
            o_ref[:, dst + p:dst + p + LANES] = piece.astype(o_ref.dtype)
        dst += slot


def _regroup_w_in_t(w_t, *, tc=512):
    n_out, n_in = w_t.shape
    out_cols = sum(slot for _, _, slot in _IN_PROJ_SEGMENTS)
    return pl.pallas_call(
        _regroup_kernel,
        grid=(n_in // tc,),
        in_specs=[pl.BlockSpec((n_out, tc), lambda i: (0, i))],
        out_specs=pl.BlockSpec((tc, out_cols), lambda i: (i, 0)),
        out_shape=jax.ShapeDtypeStruct((n_in, out_cols), BF16),
        compiler_params=pltpu.CompilerParams(dimension_semantics=("arbitrary",), vmem_limit_bytes=VMEM_LIMIT),
        name="regroup_w_in",
    )(w_t)


def kernel(x, ln_mix_g, w_in, b_attn_qkv, rwkv_shift_mu, rwkv_w0, rwkv_w2, rwkv_a0, rwkv_a2, rwkv_g2, rwkv_k_k, rwkv_k_a, rwkv_r_k, rwkv_lnx_g, rwkv_lnx_b, attn_sinks, attn_out_g, w_out, ln_ffn_g, ffn_w_up, ffn_w_gate, ffn_conv_w, ffn_conv_b, ffn_w_down, ln_final_g):
    B, T, _ = x.shape
    assert B == 1 and ln_mix_g.shape[0] == 1
    l = 0
    x2d = x.reshape(T, D_MODEL)
    row = lambda a: a.reshape(1, -1)

    w_all = _regroup_w_in_t(jnp.swapaxes(w_in[l], 0, 1))
    mu_shift = _in_proj_cols(row(rwkv_shift_mu[l]))
    (a_t, k_h, b_h, r_t, k_e, b_e, v_b, bonus, gate, p_c, attn_in, w_up, w_gate) = _in_proj(
        x2d, row(ln_mix_g[l]), w_all, mu_shift, row(b_attn_qkv[l]),
        row(rwkv_w0[l]), _pad_rows(rwkv_w2[l], LANES).astype(BF16),
        row(rwkv_a0[l]), _pad_rows(rwkv_a2[l], LANES).astype(BF16),
        _pad_rows(rwkv_g2[l], 2 * LANES).astype(BF16), row(rwkv_k_k[l]), row(rwkv_k_a[l]), row(rwkv_r_k[l]),
        _rope_angle_tables(T, SUB_BLOCK), side_casts=(ffn_w_up[l], ffn_w_gate[l]))

    y_rwkv, w_down, w_o = _rwkv(a_t, k_h, b_h, r_t, k_e, b_e, v_b, bonus, gate, p_c,
                                row(rwkv_lnx_g[l]), row(rwkv_lnx_b[l]), side_casts=(ffn_w_down[l], w_out[l]))

    y_attn = _attn(attn_in, attn_sinks[l], row(attn_out_g[l]))

    x1 = _out_proj(x2d, y_rwkv, y_attn, w_o)

    out = _ffn(x1, row(ln_ffn_g[l]), w_up, w_gate, ffn_conv_w[l], row(ffn_conv_b[l]), w_down, row(ln_final_g))
    return out.reshape(B, T, D_MODEL)
```

```python
import functools
import math

import jax
import jax.numpy as jnp
from jax import lax
from jax.experimental import pallas as pl
from jax.experimental.pallas import tpu as pltpu

F32 = jnp.float32
BF16 = jnp.bfloat16

D_MODEL = 2048
HEAD_DIM = 64
RWKV_WIDTH = 1024
ATTN_WIDTH = 1024
ATTN_HEADS = 16
KV_HEADS = 2
KV_WIDTH = KV_HEADS * HEAD_DIM
W_LORA = 64
A_LORA = 64
G_LORA = 160
WINDOW = 128
ROPE_THETA = 500000.0
ROT_DIM = 16
D_FF = 5632
NORM_EPS = 1e-5
LNX_EPS = 64e-5

LANES = 128
SUBLANES = 8
PREV_ROWS = 16
LORA_PAD = 512
GROUP = 4
GROUP_W = GROUP * HEAD_DIM
CHUNK = 64
SUB_CHUNKS = 4
SUB_BLOCK = SUB_CHUNKS * CHUNK
VMEM_LIMIT = 56 * 1024 * 1024
IN_PROJ_VMEM_LIMIT = 60 * 1024 * 1024
FFN_VMEM_LIMIT = 60 * 1024 * 1024

assert CHUNK == HEAD_DIM and SUB_CHUNKS <= SUBLANES


def _dot(a, b):
    return lax.dot_general(a, b, (((1,), (0,)), ((), ())), preferred_element_type=F32)


def _dot_nt(a, b):
    return lax.dot_general(a, b, (((1,), (1,)), ((), ())), preferred_element_type=F32)


def _dot_tn(a, b):
    return lax.dot_general(a, b, (((0,), (0,)), ((), ())), preferred_element_type=F32)


def _sigmoid(x):
    return 1.0 / (1.0 + jnp.exp(-x))


def _sigmoid_tanh(x):
    return 0.5 + 0.5 * jnp.tanh(0.5 * x)


def _rms(xv, g):
    ms = jnp.mean(xv * xv, axis=-1, keepdims=True)
    return xv * lax.rsqrt(ms + NORM_EPS) * g


def _resident(shape):
    nd = len(shape)
    return pl.BlockSpec(shape, lambda *_: (0,) * nd, pipeline_mode=pl.Buffered(1))


def _split_bf16(x):
    hi = x.astype(BF16)
    lo = (x - hi.astype(F32)).astype(BF16)
    return hi, lo


def _same_head_mask():
    r_i = lax.broadcasted_iota(jnp.int32, (GROUP_W, GROUP_W), 0) // HEAD_DIM
    c_i = lax.broadcasted_iota(jnp.int32, (GROUP_W, GROUP_W), 1) // HEAD_DIM
    return r_i == c_i


def _head_sums(x, ones_bd):
    xb = x.astype(BF16)
    return jnp.concatenate([_dot(xb[:, q * GROUP_W:(q + 1) * GROUP_W], ones_bd)
                            for q in range(RWKV_WIDTH // GROUP_W)], axis=1)


def _rope(x, cos_l, sin_a, sin_b):
    return x * cos_l + pltpu.roll(x, LANES - ROT_DIM // 2, axis=1) * sin_a + pltpu.roll(x, ROT_DIM // 2, axis=1) * sin_b


def _split_refs(refs, *counts):
    out, pos = [], 0
    for n in counts:
        out.append(tuple(refs[pos:pos + n]))
        pos += n
    assert pos == len(refs)
    return out


def _side_cast_specs(arrays, n_steps):
    assert all(a.shape[0] % (n_steps * PREV_ROWS) == 0 for a in arrays)
    specs = [pl.BlockSpec((a.shape[0] // n_steps, a.shape[1]), lambda i: (i, 0)) for a in arrays]
    shapes = [jax.ShapeDtypeStruct(a.shape, BF16) for a in arrays]
    return specs, shapes


def _side_casts(src_refs, dst_refs):
    for src, dst in zip(src_refs, dst_refs):
        dst[...] = src[...].astype(dst.dtype)


def _in_proj_kernel(x_ref, g_ref, w_ref, mu_ref, battn_ref, w0_ref, w2_ref, a0_ref, a2_ref, g2_ref,
                    kk_ref, ka_ref, rk_ref, cj_ref, sj_ref, cb_ref, sb_ref, *rest, col_chunk, n_cast):
    (cast_in, (at_ref, kh_ref, bh_ref, rt_ref, ke_ref, be_ref, vb_ref, bonus_ref, gate_ref, pc_ref, attn_ref),
     cast_out, (h_ref, carry_ref)) = _split_refs(rest, n_cast, 11, n_cast, 2)
    i = pl.program_id(0)
    tm = x_ref.shape[0]
    n_chunks = tm // CHUNK
    C = CHUNK
    _side_casts(cast_in, cast_out)
    h_ref[...] = _rms(x_ref[...], g_ref[...]).astype(BF16)
    hx = h_ref[...]
    first_rows = lax.broadcasted_iota(jnp.int32, (SUBLANES, col_chunk), 0) == 0

    def shifted(base, width):
        outs = []
        for c0 in range(base, base + width, col_chunk):
            z = _dot(hx, w_ref[:, c0:c0 + col_chunk])
            prev_last = jnp.where(i > 0, carry_ref[0:1, c0:c0 + col_chunk], 0.0)
            zp = pltpu.roll(z, 1, axis=0)
            zp = jnp.concatenate([jnp.where(first_rows, prev_last, zp[:SUBLANES]), zp[SUBLANES:]], axis=0)
            carry_ref[0:1, c0:c0 + col_chunk] = z[tm - 1:tm, :]
            outs.append(z + mu_ref[:, c0:c0 + col_chunk] * (zp - z))
        return jnp.concatenate(outs, axis=1)

    cb, sb = cb_ref[0:1, :], sb_ref[0:1, :]
    cos_l = cb * cj_ref[...] - sb * sj_ref[...]
    sin_t = sb * cj_ref[...] + cb * sj_ref[...]
    head_lane = lax.broadcasted_iota(jnp.int32, (tm, LANES), 1) % HEAD_DIM
    sin_a = jnp.where(head_lane < ROT_DIM // 2, -sin_t, 0.0)
    sin_b = jnp.where((head_lane >= ROT_DIM // 2) & (head_lane < ROT_DIM), sin_t, 0.0)

    def attn_cols(c0):
        base = LORA_PAD + 3 * RWKV_WIDTH
        c1 = min(c0 + col_chunk, attn_ref.shape[1])
        z = _dot(hx, w_ref[:, base + c0:base + c1]) + battn_ref[:, c0:c1]
        slabs = []
        for col in range(c0, c1, LANES):
            zs = z[:, col - c0:col - c0 + LANES]
            if col < ATTN_WIDTH + KV_WIDTH:
                zs = _rope(zs, cos_l, sin_a, sin_b)
            if col < ATTN_WIDTH:
                zs = zs * (HEAD_DIM ** -0.5)
            slabs.append(zs)
        attn_ref[:, c0:c1] = jnp.concatenate(slabs, axis=1)

    ones_bd = _same_head_mask().astype(BF16)
    t_i = lax.broadcasted_iota(jnp.int32, (tm, tm), 0)
    t_j = lax.broadcasted_iota(jnp.int32, (tm, tm), 1)
    tri = ((t_j <= t_i) & (t_j // C == t_i // C)).astype(BF16)
    attn_chunks = list(range(0, attn_ref.shape[1], col_chunk))

    lo = shifted(0, LORA_PAD)
    k = shifted(LORA_PAD, RWKV_WIDTH)
    xw = lo[:, 0:LANES]
    xa = lo[:, LANES:2 * LANES]
    xg = lo[:, 2 * LANES:LORA_PAD]
    u = w0_ref[...] + _dot(jnp.tanh(xw).astype(BF16), w2_ref[...])
    lw = -math.exp(-0.5) * _sigmoid_tanh(u)
    a = _sigmoid_tanh(a0_ref[...] + _dot(xa.astype(BF16), a2_ref[...]))
    gate_ref[...] = _dot(_sigmoid_tanh(xg).astype(BF16), g2_ref[...]).astype(gate_ref.dtype)
    lw_hi, lw_lo = _split_bf16(lw)
    cl = _dot(tri, lw_hi) + _dot(tri, lw_lo)
    last_rows = [cl[(c + 1) * C - 1:(c + 1) * C, :] for c in range(n_chunks)]
    cl_last = jnp.concatenate([jnp.broadcast_to(row, (C, RWKV_WIDTH)) for row in last_rows], axis=0)
    pc_ref[...] = jnp.concatenate([jnp.exp(row) for row in last_rows]
                                  + [jnp.zeros((SUBLANES - n_chunks, RWKV_WIDTH), F32)], axis=0)

    r = shifted(LORA_PAD + RWKV_WIDTH, RWKV_WIDTH)
    kkraw = k * kk_ref[...]
    k2 = k * (1.0 + (a - 1.0) * ka_ref[...])

    v = shifted(LORA_PAD + 2 * RWKV_WIDTH, RWKV_WIDTH)
    sums = _head_sums(jnp.concatenate([kkraw * kkraw, r * k2 * rk_ref[...]], axis=0), ones_bd)
    kk = kkraw / jnp.maximum(jnp.sqrt(sums[:tm]), 1e-12)
    b = kk * a

    attn_cols(attn_chunks[0])
    at_ref[...] = (-kk * jnp.exp(cl - lw)).astype(at_ref.dtype)
    e_neg = jnp.exp(-cl)
    kh_ref[...] = (k2 * e_neg).astype(kh_ref.dtype)
    bh_ref[...] = (b * e_neg).astype(bh_ref.dtype)

    for c0 in attn_chunks[1:2]:
        attn_cols(c0)
    rt_ref[...] = (r * jnp.exp(cl)).astype(rt_ref.dtype)
    e_end = jnp.exp(cl_last - cl)
    ke_ref[...] = (k2 * e_end).astype(ke_ref.dtype)
    be_ref[...] = (b * e_end).astype(be_ref.dtype)

    for c0 in attn_chunks[2:]:
        attn_cols(c0)
    bonus_ref[...] = (sums[tm:] * v).astype(bonus_ref.dtype)
    vb_ref[...] = v.astype(vb_ref.dtype)


def _in_proj(x2d, g, w_all, mu_shift, b_attn, w0, w2, a0, a2, g2, k_k, k_a, r_k, rope_tables, side_casts=(), *,
             tm=SUB_BLOCK, col_chunk=512):
    T = x2d.shape[0]
    n_attn = b_attn.shape[1]
    assert w_all.shape == (D_MODEL, 3 * RWKV_WIDTH + LORA_PAD + n_attn) and tm % CHUNK == 0
    n_tiles = T // tm
    cast_specs, cast_shapes = _side_cast_specs(side_casts, n_tiles)
    out = lambda i: (i, 0)
    vec = _resident((1, RWKV_WIDTH))
    tile = pl.BlockSpec((tm, RWKV_WIDTH), out)
    tile_shape = jax.ShapeDtypeStruct((T, RWKV_WIDTH), BF16)
    n_shift = LORA_PAD + 3 * RWKV_WIDTH
    return pl.pallas_call(
        functools.partial(_in_proj_kernel, col_chunk=col_chunk, n_cast=len(side_casts)),
        grid=(n_tiles,),
        in_specs=[
            pl.BlockSpec((tm, D_MODEL), out),
            _resident((1, D_MODEL)),
            _resident(w_all.shape), _resident(mu_shift.shape), _resident(b_attn.shape),
            vec, _resident(w2.shape), vec, _resident(a2.shape), _resident(g2.shape), vec, vec, vec,
        ] + [_resident((tm, LANES))] * 2 + [pl.BlockSpec((SUBLANES, LANES), out)] * 2 + cast_specs,
        out_specs=[tile] * 9 + [
            pl.BlockSpec((None, SUBLANES, RWKV_WIDTH), lambda i: (i, 0, 0)),
            pl.BlockSpec((tm, n_attn), out),
        ] + cast_specs,
        out_shape=[tile_shape] * 9 + [
            jax.ShapeDtypeStruct((T // tm, SUBLANES, RWKV_WIDTH), F32),
            jax.ShapeDtypeStruct((T, n_attn), F32),
        ] + cast_shapes,
        scratch_shapes=[pltpu.VMEM((tm, D_MODEL), BF16), pltpu.VMEM((SUBLANES, n_shift), F32)],
        compiler_params=pltpu.CompilerParams(dimension_semantics=("arbitrary",),
                                             vmem_limit_bytes=IN_PROJ_VMEM_LIMIT),
        name="in_proj",
    )(x2d, g, w_all, mu_shift, b_attn, w0, w2, a0, a2, g2, k_k, k_a, r_k, *rope_tables, *side_casts)


def _block_diag(x):
    blk = lax.broadcasted_iota(jnp.int32, x.shape, 1) // HEAD_DIM
    zero = jnp.zeros_like(x)
    return jnp.concatenate([jnp.where(blk == h, x, zero) for h in range(GROUP)], axis=0)


def _rwkv_kernel(at_ref, kh_ref, bh_ref, rt_ref, ke_ref, be_ref, vb_ref, bonus_ref, gate_ref, pc_ref,
                 lng_ref, lnb_ref, *rest, n_sub, sub_chunks, n_cast):
    cast_in, (y_ref,), cast_out, (s_ref,) = _split_refs(rest, n_cast, 1, n_cast, 1)
    C = CHUNK
    SB = sub_chunks * C
    n_groups = RWKV_WIDTH // GROUP_W

    @pl.when(pl.program_id(0) == 0)
    def _():
        s_ref[...] = jnp.zeros_like(s_ref)

    _side_casts(cast_in, cast_out)
    row = lax.broadcasted_iota(jnp.int32, (C, GROUP_W), 0)
    col = lax.broadcasted_iota(jnp.int32, (C, GROUP_W), 1) % C
    strict = col < row
    incl = col <= row
    eye = (col == row).astype(F32)
    lane_head = lax.broadcasted_iota(jnp.int32, (HEAD_DIM, GROUP_W), 1) // HEAD_DIM
    ones_bd = _same_head_mask().astype(BF16)
    streams = [(c, q) for c in range(sub_chunks) for q in range(n_groups)]
    state = [s_ref[q] for q in range(n_groups)]
    subs = [dict() for _ in range(n_sub)]

    def blk(ref, h, s):
        c, q = s
        return ref[h * SB + c * C:h * SB + (c + 1) * C, q * GROUP_W:(q + 1) * GROUP_W]

    def local_pieces(h):
        H = subs[h]
        a_ak, a_rk, l_ab, pw = {}, {}, {}, {}
        H["a_rb"], H["t_inv"], H["av"] = {}, {}, {}
        a_rb, t_inv, av = H["a_rb"], H["t_inv"], H["av"]

        def scores():
            for s in streams:
                ar = jnp.concatenate([blk(at_ref, h, s), blk(rt_ref, h, s)], axis=0)
                ak = _dot_nt(ar, _block_diag(blk(kh_ref, h, s)))
                ab = _dot_nt(ar, _block_diag(blk(bh_ref, h, s)))
                a_ak[s] = jnp.where(strict, ak[:C], 0.0)
                a_rk[s] = jnp.where(incl, ak[C:], 0.0)
                l_ab[s] = jnp.where(strict, ab[:C], 0.0)
                a_rb[s] = jnp.where(incl, ab[C:], 0.0).astype(BF16)

        def inv_first():
            for s in streams:
                t_inv[s] = eye + l_ab[s]
                lb = l_ab[s].astype(BF16)
                pw[s] = _dot(lb, _block_diag(lb))

        def inv_level():
            for s in streams:
                tp = _dot(jnp.concatenate([t_inv[s], pw[s]], axis=0).astype(BF16),
                          _block_diag(pw[s].astype(BF16)))
                t_inv[s] = t_inv[s] + tp[:C]
                pw[s] = tp[C:]

        def inv_last():
            for s in streams:
                t_inv[s] = (t_inv[s] + _dot(t_inv[s].astype(BF16), _block_diag(pw[s].astype(BF16)))).astype(BF16)

        def values():
            for s in streams:
                av[s] = _dot(jnp.concatenate([a_ak[s], a_rk[s]], axis=0).astype(BF16),
                             _block_diag(blk(vb_ref, h, s)))

        return [scores, inv_first] + [inv_level] * (int(math.log2(C)) - 2) + [inv_last, values]

    def chain_pieces(h):
        H = subs[h]
        xs, u_b, ys = {}, {}, {}

        def read_state(c):
            def f():
                for q in range(n_groups):
                    s = (c, q)
                    ar = jnp.concatenate([blk(at_ref, h, s), blk(rt_ref, h, s)], axis=0)
                    xs[s] = _dot_nt(ar, _block_diag(state[q].astype(BF16)))
            return f

        def solve(c):
            def f():
                for q in range(n_groups):
                    s = (c, q)
                    x_u = (xs[s][:C] + H["av"][s][:C]).astype(BF16)
                    u_b[s] = _dot(H["t_inv"][s], _block_diag(x_u)).astype(BF16)
            return f

        def update(c):
            def f():
                for q in range(n_groups):
                    s = (c, q)
                    ys[s] = xs[s][C:] + H["av"][s][C:] + _dot(H["a_rb"][s], _block_diag(u_b[s]))
                    vu = jnp.concatenate([blk(vb_ref, h, s), u_b[s]], axis=0)
                    kb = jnp.concatenate([blk(ke_ref, h, s), blk(be_ref, h, s)], axis=0)
                    upd = _dot_tn(vu, kb)
                    upd_c = sum(jnp.where(lane_head == hd, upd[hd * HEAD_DIM:(hd + 1) * HEAD_DIM, :], 0.0)
                                for hd in range(GROUP))
                    tile, slot = divmod(h * sub_chunks + c, SUB_CHUNKS)
                    p_c = pc_ref[tile, slot:slot + 1, q * GROUP_W:(q + 1) * GROUP_W]
                    state[q] = state[q] * p_c + upd_c
                if c == sub_chunks - 1:
                    H["y"] = jnp.concatenate([jnp.concatenate([ys[(cc, q)] for q in range(n_groups)], axis=1)
                                              for cc in range(sub_chunks)], axis=0)
            return f

        return [p for c in range(sub_chunks) for p in (read_state(c), solve(c), update(c))]

    def epilogue_pieces(h):
        H = subs[h]
        rows = slice(h * SB, (h + 1) * SB)

        def center():
            mu = _head_sums(H["y"], ones_bd) * (1.0 / HEAD_DIM)
            H["d"] = H["y"] - mu

        def scale():
            d = H["d"]
            var = _head_sums(d * d, ones_bd) * (1.0 / HEAD_DIM)
            yn = d * lax.rsqrt(var + LNX_EPS) * lng_ref[...] + lnb_ref[...]
            y_ref[rows, :] = ((yn + bonus_ref[rows, :].astype(F32)) * gate_ref[rows, :].astype(F32)).astype(y_ref.dtype)

        return [center, scale]

    def run_merged(*lists):
        lists = [pieces for pieces in lists if pieces]
        if not lists:
            return
        main = max(lists, key=len)
        others = [pieces for pieces in lists if pieces is not main]
        n = len(main)
        for i, piece in enumerate(main):
            piece()
            for other in others:
                for j, o in enumerate(other):
                    if j * n // len(other) == i:
                        o()

    for h in range(n_sub + 2):
        run_merged(chain_pieces(h - 1) if 1 <= h <= n_sub else [],
                   local_pieces(h) if h < n_sub else [],
                   epilogue_pieces(h - 2) if h >= 2 else [])
    for q in range(n_groups):
        s_ref[q] = state[q]


def _rwkv(a_t, k_h, b_h, r_t, k_e, b_e, v_b, bonus, gate, p_c, ln_g, ln_b, side_casts=(), *, tb=2 * SUB_BLOCK,
          sub_chunks=2):
    T = a_t.shape[0]
    n_sub = tb // (sub_chunks * CHUNK)
    assert p_c.shape == (T // SUB_BLOCK, SUBLANES, RWKV_WIDTH) and tb % SUB_BLOCK == 0
    vec = _resident((1, RWKV_WIDTH))
    tile = pl.BlockSpec((tb, RWKV_WIDTH), lambda i: (i, 0))
    cast_specs, cast_shapes = _side_cast_specs(side_casts, T // tb)
    return pl.pallas_call(
        functools.partial(_rwkv_kernel, n_sub=n_sub, sub_chunks=sub_chunks, n_cast=len(side_casts)),
        grid=(T // tb,),
        in_specs=[tile] * 9 + [pl.BlockSpec((tb // SUB_BLOCK, SUBLANES, RWKV_WIDTH), lambda i: (i, 0, 0)), vec, vec]
        + cast_specs,
        out_specs=[tile] + cast_specs,
        out_shape=[jax.ShapeDtypeStruct((T, RWKV_WIDTH), BF16)] + cast_shapes,
        scratch_shapes=[pltpu.VMEM((RWKV_WIDTH // GROUP_W, HEAD_DIM, GROUP_W), F32)],
        compiler_params=pltpu.CompilerParams(dimension_semantics=("arbitrary",),
                                             vmem_limit_bytes=VMEM_LIMIT),
        name="rwkv7",
    )(a_t, k_h, b_h, r_t, k_e, b_e, v_b, bonus, gate, p_c, ln_g, ln_b, *side_casts)


def _attn_kernel(sink_ref, q_ref, kv_ref, kvp_ref, g_ref, o_ref, *, n_blocks):
    n = pl.program_id(0)
    B = WINDOW
    key0 = lax.broadcasted_iota(jnp.int32, (2 * B, LANES), 0) == 0
    low = lax.broadcasted_iota(jnp.int32, (2 * B, LANES), 1) < HEAD_DIM
    qi = lax.broadcasted_iota(jnp.int32, (2 * B, B), 0) % B
    si = lax.broadcasted_iota(jnp.int32, (2 * B, B), 1)
    allowed_cur = si <= qi
    sink_col = si == 0
    top = lax.broadcasted_iota(jnp.int32, (2 * B, 1), 0) < B
    qlow = lax.broadcasted_iota(jnp.int32, (B, LANES), 1) < HEAD_DIM
    pairs_per_kv = (ATTN_HEADS // KV_HEADS) // 2

    k_dup, v_top, v_bot, allowed_prev = [], [], [], []
    for b in range(n_blocks):
        kv = kv_ref[b * B:(b + 1) * B, :]
        kvp = kvp_ref[...] if b == 0 else kv_ref[(b - 1) * B:b * B, :]
        k_win = jnp.concatenate([kvp[:, :LANES], kv[:, :LANES]], axis=0)
        v_win = jnp.where(key0, 0.0, jnp.concatenate([kvp[:, LANES:], kv[:, LANES:]], axis=0))
        k_sw = pltpu.roll(k_win, HEAD_DIM, axis=1)
        v_sw = pltpu.roll(v_win, HEAD_DIM, axis=1)
        k_dup.append([jnp.where(low, k_win, k_sw).astype(BF16), jnp.where(low, k_sw, k_win).astype(BF16)])
        v_top.append([jnp.where(low, v_win, 1.0).astype(BF16), jnp.where(low, v_sw, 1.0).astype(BF16)])
        v_bot.append([jnp.where(low, 1.0, v_sw).astype(BF16), jnp.where(low, 1.0, v_win).astype(BF16)])
        allowed_prev.append((si > qi) & (n > 0) if b == 0 else si > qi)

    work = [(b, p) for b in range(n_blocks) for p in range(ATTN_HEADS // 2)]
    scores, es, outs = {}, {}, {}
    for b, p in work:
        q = q_ref[b * B:(b + 1) * B, p * LANES:(p + 1) * LANES]
        zero = jnp.zeros_like(q)
        q_st = jnp.concatenate([jnp.where(qlow, q, zero), jnp.where(qlow, zero, q)], axis=0).astype(BF16)
        scores[b, p] = _dot_nt(q_st, k_dup[b][p // pairs_per_kv])
    for b, p in work:
        sink = jnp.where(top, sink_ref[2 * p], sink_ref[2 * p + 1])
        sc = scores[b, p]
        s = jnp.concatenate([jnp.where(allowed_prev[b], sc[:, :B], jnp.where(sink_col, sink, -1e30)),
                             jnp.where(allowed_cur, sc[:, B:], -1e30)], axis=1)
        m = jnp.max(s, axis=-1, keepdims=True)
        es[b, p] = jnp.exp((s - m).astype(BF16))
    for b, p in work:
        kvh = p // pairs_per_kv
        o_top = _dot(es[b, p][:B], v_top[b][kvh])
        o_bot = _dot(es[b, p][B:], v_bot[b][kvh])
        sums = pltpu.roll(jnp.where(qlow, o_bot, o_top), HEAD_DIM, axis=1)
        outs[b, p] = jnp.where(qlow, o_top, o_bot) / sums
    g = g_ref[...]
    for b in range(n_blocks):
        o = jnp.concatenate([outs[b, p] for p in range(ATTN_HEADS // 2)], axis=1)
        o_ref[b * B:(b + 1) * B, :] = _rms(o, g).astype(o_ref.dtype)


def _attn(attn_in, sinks, g, *, n_blocks=4):
    T = attn_in.shape[0]
    B = WINDOW
    tq = n_blocks * B
    kv_blk = ATTN_WIDTH // (2 * KV_WIDTH)
    return pl.pallas_call(
        functools.partial(_attn_kernel, n_blocks=n_blocks),
        grid=(T // tq,),
        in_specs=[
            pl.BlockSpec(memory_space=pltpu.SMEM),
            pl.BlockSpec((tq, ATTN_WIDTH), lambda n: (n, 0)),
            pl.BlockSpec((tq, 2 * KV_WIDTH), lambda n: (n, kv_blk)),
            pl.BlockSpec((B, 2 * KV_WIDTH), lambda n: (jnp.maximum(n * n_blocks - 1, 0), kv_blk)),
            _resident((1, ATTN_WIDTH)),
        ],
        out_specs=pl.BlockSpec((tq, ATTN_WIDTH), lambda n: (n, 0)),
        out_shape=jax.ShapeDtypeStruct((T, ATTN_WIDTH), BF16),
        compiler_params=pltpu.CompilerParams(dimension_semantics=("arbitrary",),
                                             vmem_limit_bytes=VMEM_LIMIT),
        name="swa_attn",
    )(sinks, attn_in, attn_in, attn_in, g)


def _out_proj_kernel(x_ref, yr_ref, ya_ref, w_ref, o_ref, *, col_chunk):
    yr, ya = yr_ref[...], ya_ref[...]
    for c0 in range(0, o_ref.shape[1], col_chunk):
        cols = slice(c0, c0 + col_chunk)
        o_ref[:, cols] = x_ref[:, cols] + _dot(yr, w_ref[:RWKV_WIDTH, cols]) + _dot(ya, w_ref[RWKV_WIDTH:, cols])


def _out_proj(x2d, y_rwkv, y_attn, w, *, tm=1024, col_chunk=512):
    T = x2d.shape[0]
    row = lambda width: pl.BlockSpec((tm, width), lambda i: (i, 0))
    return pl.pallas_call(
        functools.partial(_out_proj_kernel, col_chunk=col_chunk),
        grid=(T // tm,),
        in_specs=[row(D_MODEL), row(RWKV_WIDTH), row(ATTN_WIDTH), _resident(w.shape)],
        out_specs=row(D_MODEL),
        out_shape=jax.ShapeDtypeStruct((T, D_MODEL), F32),
        compiler_params=pltpu.CompilerParams(dimension_semantics=("arbitrary",),
                                             vmem_limit_bytes=VMEM_LIMIT),
        name="out_proj",
    )(x2d, y_rwkv, y_attn, w)


def _ffn_kernel(x_ref, xp_ref, g_ref, wup_ref, wgate_ref, cw_ref, cb_ref, wdown_ref, gf_ref, o_ref, hext_ref):
    i = pl.program_id(0)
    f = pl.program_id(1)

    @pl.when(f == 0)
    def _():
        g = g_ref[...]
        x = x_ref[...]
        hext_ref[PREV_ROWS:, :] = _rms(x, g).astype(BF16)
        hp = jnp.where(i > 0, _rms(xp_ref[...], g), 0.0)
        hext_ref[:PREV_ROWS, :] = hp.astype(BF16)
        o_ref[...] = x

    hx = hext_ref[...]
    u = _dot(hx, wup_ref[...])
    cw = cw_ref[...]
    conv = cb_ref[...] + cw[2:3, :] * u + cw[1:2, :] * pltpu.roll(u, 1, axis=0) + cw[0:1, :] * pltpu.roll(u, 2, axis=0)
    conv = conv[PREV_ROWS:, :]
    gate = _dot(hext_ref[PREV_ROWS:, :], wgate_ref[...])
    act = conv * _sigmoid(conv) * gate
    o_ref[...] += _dot(act.astype(BF16), wdown_ref[...])

    @pl.when(f == pl.num_programs(1) - 1)
    def _():
        o_ref[...] = _rms(o_ref[...], gf_ref[...])


def _ffn(x1, g, w_up, w_gate, conv_w, conv_b, w_down, g_final, *, tm=1024, tf=512):
    T = x1.shape[0]
    prev_blocks = tm // PREV_ROWS
    return pl.pallas_call(
        _ffn_kernel,
        grid=(T // tm, D_FF // tf),
        in_specs=[
            pl.BlockSpec((tm, D_MODEL), lambda i, f: (i, 0)),
            pl.BlockSpec((PREV_ROWS, D_MODEL), lambda i, f: (jnp.maximum(i * prev_blocks - 1, 0), 0)),
            pl.BlockSpec((1, D_MODEL), lambda i, f: (0, 0)),
            pl.BlockSpec((D_MODEL, tf), lambda i, f: (0, f)),
            pl.BlockSpec((D_MODEL, tf), lambda i, f: (0, f)),
            pl.BlockSpec((3, tf), lambda i, f: (0, f)),
            pl.BlockSpec((1, tf), lambda i, f: (0, f)),
            pl.BlockSpec((tf, D_MODEL), lambda i, f: (f, 0)),
            pl.BlockSpec((1, D_MODEL), lambda i, f: (0, 0)),
        ],
        out_specs=pl.BlockSpec((tm, D_MODEL), lambda i, f: (i, 0)),
        out_shape=jax.ShapeDtypeStruct((T, D_MODEL), F32),
        scratch_shapes=[pltpu.VMEM((tm + PREV_ROWS, D_MODEL), BF16)],
        compiler_params=pltpu.CompilerParams(dimension_semantics=("arbitrary", "arbitrary"),
                                             vmem_limit_bytes=FFN_VMEM_LIMIT),
        name="ffn",
    )(x1, x1, g, w_up, w_gate, conv_w, conv_b, w_down, g_final)


def _pad_cols(a, width):
    return jnp.pad(a, ((0, 0), (0, width - a.shape[1])))


def _pad_rows(a, rows):
    return jnp.pad(a, ((0, rows - a.shape[0]), (0, 0)))


def _rope_angle_tables(T, tm):
    inv_freq = ROPE_THETA ** (-jnp.arange(0, ROT_DIM, 2, dtype=F32) / ROT_DIM)
    f_head = jnp.concatenate([inv_freq, inv_freq, jnp.zeros((HEAD_DIM - ROT_DIM,), F32)])
    f_lane = jnp.tile(f_head, LANES // HEAD_DIM)[None, :]
    offs = jnp.arange(tm, dtype=F32)[:, None] * f_lane
    starts = jnp.repeat(jnp.arange(0, T, tm, dtype=F32), SUBLANES)[:, None] * f_lane
    return jnp.cos(offs), jnp.sin(offs), jnp.cos(starts), jnp.sin(starts)


_IN_PROJ_SEGMENTS = (
    (3 * RWKV_WIDTH, W_LORA, LANES),
    (3 * RWKV_WIDTH + W_LORA, A_LORA, LANES),
    (3 * RWKV_WIDTH + W_LORA + A_LORA, G_LORA, 2 * LANES),
    (RWKV_WIDTH, RWKV_WIDTH, RWKV_WIDTH),
    (0, RWKV_WIDTH, RWKV_WIDTH),
    (2 * RWKV_WIDTH, RWKV_WIDTH, RWKV_WIDTH),
    (3 * RWKV_WIDTH + W_LORA + A_LORA + G_LORA, ATTN_WIDTH + 2 * KV_WIDTH, ATTN_WIDTH + 2 * KV_WIDTH),
)


def _in_proj_cols(a):
    return jnp.concatenate([_pad_cols(a[:, src:src + width], slot) for src, width, slot in _IN_PROJ_SEGMENTS],
                           axis=1)


def _regroup_kernel(wt_ref, o_ref):
    lane = lax.broadcasted_iota(jnp.int32, (o_ref.shape[0], LANES), 1)
    dst = 0
    for src, width, slot in _IN_PROJ_SEGMENTS:
        for p in range(0, slot, LANES):
            valid = min(max(width - p, 0), LANES)
            if valid == 0:
                o_ref[:, dst + p:dst + p + LANES] = jnp.zeros((o_ref.shape[0], LANES), o_ref.dtype)
                continue
            piece = wt_ref[src + p:src + p + LANES, :].T
            if valid < LANES:
                piece = jnp.where(lane < valid, piece, 0.0)
            o_ref[:, dst + p:dst + p + LANES] = piece.astype(o_ref.dtype)
        dst += slot


def _regroup_w_in_t(w_t, *, tc=512):
    n_out, n_in = w_t.shape
    out_cols = sum(slot for _, _, slot in _IN_PROJ_SEGMENTS)
    return pl.pallas_call(
        _regroup_kernel,
        grid=(n_in // tc,),
        in_specs=[pl.BlockSpec((n_out, tc), lambda i: (0, i))],
        out_specs=pl.BlockSpec((tc, out_cols), lambda i: (i, 0)),
        out_shape=jax.ShapeDtypeStruct((n_in, out_cols), BF16),
        compiler_params=pltpu.CompilerParams(dimension_semantics=("arbitrary",), vmem_limit_bytes=VMEM_LIMIT),
        name="regroup_w_in",
    )(w_t)


def kernel(x, ln_mix_g, w_in, b_attn_qkv, rwkv_shift_mu, rwkv_w0, rwkv_w2, rwkv_a0, rwkv_a2, rwkv_g2, rwkv_k_k, rwkv_k_a, rwkv_r_k, rwkv_lnx_g, rwkv_lnx_b, attn_sinks, attn_out_g, w_out, ln_ffn_g, ffn_w_up, ffn_w_gate, ffn_conv_w, ffn_conv_b, ffn_w_down, ln_final_g):
    B, T, _ = x.shape
    assert B == 1 and ln_mix_g.shape[0] == 1
    l = 0
    x2d = x.reshape(T, D_MODEL)
    row = lambda a: a.reshape(1, -1)

    w_all = _regroup_w_in_t(jnp.swapaxes(w_in[l], 0, 1))
    mu_shift = _in_proj_cols(row(rwkv_shift_mu[l]))
    (a_t, k_h, b_h, r_t, k_e, b_e, v_b, bonus, gate, p_c, attn_in, w_up, w_gate) = _in_proj(
        x2d, row(ln_mix_g[l]), w_all, mu_shift, row(b_attn_qkv[l]),
        row(rwkv_w0[l]), _pad_rows(rwkv_w2[l], LANES).astype(BF16),
        row(rwkv_a0[l]), _pad_rows(rwkv_a2[l], LANES).astype(BF16),
        _pad_rows(rwkv_g2[l], 2 * LANES).astype(BF16), row(rwkv_k_k[l]), row(rwkv_k_a[l]), row(rwkv_r_k[l]),
        _rope_angle_tables(T, SUB_BLOCK), side_casts=(ffn_w_up[l], ffn_w_gate[l]))

    y_rwkv, w_down, w_o = _rwkv(a_t, k_h, b_h, r_t, k_e, b_e, v_b, bonus, gate, p_c,
                                row(rwkv_lnx_g[l]), row(rwkv_lnx_b[l]), side_casts=(ffn_w_down[l], w_out[l]))

    y_attn = _attn(attn_in, attn_sinks[l], row(attn_out_g[l]))

    x1 = _out_proj(x2d, y_rwkv, y_attn, w_o)

    out = _ffn(x1, row(ln_ffn_g[l]), w_up, w_gate, ffn_conv_w[l], row(ffn_conv_b[l]), w_down, row(ln_final_g))
    return out.reshape(B, T, D_MODEL)
```
